```python
import math
import jax
import jax.numpy as jnp
from jax import lax
import numpy as np

D_MODEL = 1024
BATCH = 4
SEQ = 4096
DEPTH = 1

GRID_W = 64
CTX_LEN = 256
N_DIR = 2
DN_HEADS = 4
DN_HEAD_DIM = 128
DN_WIDTH = DN_HEADS * DN_HEAD_DIM
DN_CHUNK = 64
SHORT_CONV = 3
S5_WIDTH = 512
S5_GROUP = 16
S5_GROUPS = S5_WIDTH // S5_GROUP
S5_STATE = 64
D_FF = 2816
FFN_CONV = 3
N_BRANCH = 2
N_MOD = 6
RMS_EPS = 1e-6
L2_EPS = 1e-6
IN_SIZES = (DN_WIDTH, DN_WIDTH, DN_WIDTH, DN_WIDTH, N_DIR * DN_HEADS, N_DIR * DN_HEADS, S5_WIDTH, D_MODEL, D_MODEL)
IN_COLS = 4 * DN_WIDTH + 2 * N_DIR * DN_HEADS + S5_WIDTH + N_BRANCH * D_MODEL

kernel_name = "hybrid_gdn_s5_convffn_block"


def rmsnorm(x, w):
    xf = x.astype(jnp.float32)
    y = xf * lax.rsqrt(jnp.mean(xf * xf, axis=-1, keepdims=True) + RMS_EPS)
    return (y * w.astype(jnp.float32)).astype(x.dtype)


def l2norm(x):
    xf = x.astype(jnp.float32)
    return xf * lax.rsqrt(jnp.sum(xf * xf, axis=-1, keepdims=True) + L2_EPS)


def modulate(h, shift, scale):
    return h * (1 + scale) + shift


def flip_if(a, rev, axis):
    return jnp.flip(a, axis=axis) if rev else a


def dw_conv1d(x, w):
    k = w.shape[0]
    pad = k // 2
    t = x.shape[1]
    xp = jnp.pad(x, ((0, 0), (pad, pad), (0, 0)))
    out = xp[:, 0:t] * w[0]
    for j in range(1, k):
        out = out + xp[:, j:j + t] * w[j]
    return out


def dw_conv2d(x, w, rows):
    b, t, ch = x.shape
    img = x.reshape(b, rows, t // rows, ch)
    y = lax.conv_general_dilated(img, w[:, :, None, :].astype(x.dtype), (1, 1), 'SAME',
                                 dimension_numbers=('NHWC', 'HWIO', 'NHWC'), feature_group_count=ch)
    return y.reshape(b, t, ch)


def gated_delta_chunked(q, k, v, g, beta, s0):
    bsz, h, t, _ = q.shape
    dv = v.shape[-1]
    n = t // DN_CHUNK

    def chunks(a):
        return jnp.moveaxis(a.reshape(bsz, h, n, DN_CHUNK, *a.shape[3:]), 2, 0)

    q, k, v, beta = chunks(q), chunks(k), chunks(v), chunks(beta)
    g = jnp.cumsum(chunks(g), axis=-1)
    kb = k * beta[..., None]
    vb = v * beta[..., None]
    idx = jnp.arange(DN_CHUNK)
    incl = idx[:, None] >= idx[None, :]
    strict = idx[:, None] > idx[None, :]
    diff = g[..., :, None] - g[..., None, :]
    decay = jnp.where(incl, jnp.exp(jnp.where(incl, diff, 0.0)), 0.0)
    lmat = jnp.where(strict, jnp.einsum('nbhcd,nbhsd->nbhcs', kb, k) * decay, 0.0)
    rhs = jnp.concatenate([vb, kb * jnp.exp(g)[..., None]], axis=-1)
    sol = lax.linalg.triangular_solve(lmat, rhs, left_side=True, lower=True, unit_diagonal=True)
    u, w = sol[..., :dv], sol[..., dv:]
    a_intra = jnp.einsum('nbhcd,nbhsd->nbhcs', q, k) * decay

    def step(state, xs):
        q_i, k_i, u_i, w_i, g_i, a_i = xs
        v_new = u_i - jnp.einsum('bhck,bhkv->bhcv', w_i, state)
        o = (jnp.einsum('bhck,bhkv->bhcv', q_i * jnp.exp(g_i)[..., None], state)
             + jnp.einsum('bhcs,bhsv->bhcv', a_i, v_new))
        g_last = g_i[..., -1:]
        state = (state * jnp.exp(g_last)[..., None]
                 + jnp.einsum('bhck,bhcv->bhkv', k_i * jnp.exp(g_last - g_i)[..., None], v_new))
        return state, o

    s_fin, o = lax.scan(step, s0, (q, k, u, w, g, a_intra))
    o = jnp.moveaxis(o, 0, 2).reshape(bsz, h, t, dv)
    return s_fin, o


def delta_prep(q, k, v, beta_logit, alpha, conv_w, a_log, dt_bias):
    b, t, _ = q.shape
    qkv = jax.nn.silu(dw_conv1d(jnp.concatenate([q, k, v], axis=-1), conv_w))

    def heads(a):
        return a.reshape(b, t, DN_HEADS, DN_HEAD_DIM).transpose(0, 2, 1, 3)

    qh, kh, vh = [heads(a) for a in jnp.split(qkv, 3, axis=-1)]
    qh = l2norm(qh) * (DN_HEAD_DIM ** -0.5)
    kh = l2norm(kh)
    vh = vh.astype(jnp.float32)

    def per_dir(a):
        return a.astype(jnp.float32).reshape(b, t, N_DIR, DN_HEADS).transpose(0, 2, 3, 1)

    beta = jax.nn.sigmoid(per_dir(beta_logit))
    g = (-jnp.exp(a_log.astype(jnp.float32))[None, :, :, None]
         * jax.nn.softplus(per_dir(alpha) + dt_bias.astype(jnp.float32)[None, :, :, None]))
    return qh, kh, vh, g, beta


def delta_branch(p_ctx, p_lat, conv_w, a_log, dt_bias, norm_w, w_out, need_ctx):
    qc, kc, vc, zc, bc, ac = p_ctx
    ql, kl, vl, zl, bl, al = p_lat
    ctx_in = delta_prep(qc, kc, vc, bc, ac, conv_w, a_log, dt_bias)
    lat_in = delta_prep(ql, kl, vl, bl, al, conv_w, a_log, dt_bias)
    b = ql.shape[0]
    s0 = jnp.zeros((b, DN_HEADS, DN_HEAD_DIM, DN_HEAD_DIM), jnp.float32)
    o_ctx = jnp.zeros_like(ctx_in[2])
    o_lat = jnp.zeros_like(lat_in[2])
    for r in range(N_DIR):
        rev = r == 1
        qh, kh, vh, g, beta = ctx_in
        s_ctx, o = gated_delta_chunked(flip_if(qh, rev, 2), flip_if(kh, rev, 2), flip_if(vh, rev, 2),
                                       flip_if(g[:, r], rev, 2), flip_if(beta[:, r], rev, 2), s0)
        if need_ctx:
            o_ctx = o_ctx + flip_if(o, rev, 2)
        qh, kh, vh, g, beta = lat_in
        _, o = gated_delta_chunked(flip_if(qh, rev, 2), flip_if(kh, rev, 2), flip_if(vh, rev, 2),
                                   flip_if(g[:, r], rev, 2), flip_if(beta[:, r], rev, 2), s_ctx)
        o_lat = o_lat + flip_if(o, rev, 2)

    def post(o, z):
        t = o.shape[2]
        on = rmsnorm(o.transpose(0, 2, 1, 3), norm_w)
        zf = z.astype(jnp.float32).reshape(b, t, DN_HEADS, DN_HEAD_DIM)
        y = (on * jax.nn.silu(zf)).reshape(b, t, DN_WIDTH).astype(z.dtype)
        return y @ w_out

    out_lat = post(o_lat, zl)
    out_ctx = post(o_ctx, zc) if need_ctx else None
    return out_ctx, out_lat


def s5_discretize(a_re, a_im, log_step, b_re, b_im):
    lam = lax.complex(a_re.astype(jnp.float32), a_im.astype(jnp.float32))
    step = jnp.exp(log_step.astype(jnp.float32))[:, None]
    lam_bar = jnp.exp(lam * step)
    bmat = lax.complex(b_re.astype(jnp.float32), b_im.astype(jnp.float32))
    b_bar = ((lam_bar - 1.0) / lam)[..., None] * bmat
    return lam_bar, b_bar


def s5_combine(left, right):
    a_l, b_l = left
    a_r, b_r = right
    return a_r * a_l, a_r * b_l + b_r


def s5_scan(u, lam_bar, b_bar, x0):
    bu = lax.complex(jnp.einsum('btgp,gnp->btgn', u, jnp.real(b_bar)),
                     jnp.einsum('btgp,gnp->btgn', u, jnp.imag(b_bar)))
    bu = bu.at[:, 0].add(lam_bar * x0)
    lam_el = jnp.broadcast_to(lam_bar, bu.shape)
    _, xs = lax.associative_scan(s5_combine, (lam_el, bu), axis=1)
    return xs


def s5_readout(c_re, c_im, xs):
    return (jnp.einsum('gpn,btgn->btgp', c_re, jnp.real(xs))
            - jnp.einsum('gpn,btgn->btgp', c_im, jnp.imag(xs)))


def s5_branch(u_ctx, u_lat, a_re, a_im, log_step, b_re, b_im, c_re, c_im, d_skip,
              w_glu, b_glu, w_out, need_ctx):
    def groups(u):
        return u.astype(jnp.float32).reshape(u.shape[0], u.shape[1], S5_GROUPS, S5_GROUP)

    gc, gl = groups(u_ctx), groups(u_lat)
    dsk = d_skip.astype(jnp.float32).reshape(S5_GROUPS, S5_GROUP)
    y_ctx = dsk * gc
    y_lat = dsk * gl
    x0 = jnp.zeros((gl.shape[0], S5_GROUPS, S5_STATE), jnp.complex64)
    for r in range(N_DIR):
        rev = r == 1
        lam_bar, b_bar = s5_discretize(a_re[r], a_im[r], log_step[r], b_re[r], b_im[r])
        cr = c_re[r].astype(jnp.float32)
        ci = c_im[r].astype(jnp.float32)
        xs_ctx = s5_scan(flip_if(gc, rev, 1), lam_bar, b_bar, x0)
        xs_lat = s5_scan(flip_if(gl, rev, 1), lam_bar, b_bar, xs_ctx[:, -1])
        y_lat = y_lat + flip_if(s5_readout(cr, ci, xs_lat), rev, 1)
        if need_ctx:
            y_ctx = y_ctx + flip_if(s5_readout(cr, ci, xs_ctx), rev, 1)

    def glu_out(y, dtype):
        y = jax.nn.gelu(y.reshape(y.shape[0], y.shape[1], S5_WIDTH)).astype(dtype)
        z = y @ w_glu + b_glu
        y = z[..., :S5_WIDTH] * jax.nn.sigmoid(z[..., S5_WIDTH:])
        return y @ w_out

    out_lat = glu_out(y_lat, u_lat.dtype)
    out_ctx = glu_out(y_ctx, u_ctx.dtype) if need_ctx else None
    return out_ctx, out_lat


def conv_ffn(h, w_up, conv_w, w_down, rows):
    u = dw_conv2d(h @ w_up, conv_w, rows)
    gate, val = jnp.split(u, 2, axis=-1)
    return (jax.nn.silu(gate) * val) @ w_down


def setup_inputs(seed: int = 0) -> dict:
    key = jax.random.key(seed)
    ks = iter(list(jax.random.split(key, 48)))
    f32 = jnp.float32
    L = DEPTH

    def nrm(shape, scale):
        return jax.random.normal(next(ks), shape, f32) * scale

    def gain(shape):
        return 1.0 + nrm(shape, 0.02)

    x = nrm((BATCH, SEQ, D_MODEL), 1.0)
    c = nrm((BATCH, D_MODEL), 1.0)
    ctx = nrm((BATCH, CTX_LEN, D_MODEL), 1.0)
    c_ctx = nrm((D_MODEL,), 1.0)
    w_ada = nrm((L, D_MODEL, N_MOD * D_MODEL), 0.5 * D_MODEL ** -0.5)
    b_ada = nrm((L, N_MOD * D_MODEL), 0.02)
    norm1_w = gain((L, D_MODEL))
    w_in = nrm((L, D_MODEL, IN_COLS), D_MODEL ** -0.5)
    dn_conv_w = nrm((L, SHORT_CONV, 3 * DN_WIDTH), SHORT_CONV ** -0.5)
    dn_a_log = jnp.log(jax.random.uniform(next(ks), (L, N_DIR, DN_HEADS), f32, 1.0, 16.0))
    dt = jnp.exp(jax.random.uniform(next(ks), (L, N_DIR, DN_HEADS), f32, math.log(1e-3), math.log(1e-1)))
    dn_dt_bias = dt + jnp.log(-jnp.expm1(-dt))
    dn_norm_w = gain((L, DN_HEAD_DIM))
    w_a_out = nrm((L, DN_WIDTH, D_MODEL), DN_WIDTH ** -0.5)
    s5_a_re = -0.5 + nrm((L, N_DIR, S5_GROUPS, S5_STATE), 0.01)
    s5_a_im = jnp.pi * jnp.arange(S5_STATE, dtype=f32) + nrm((L, N_DIR, S5_GROUPS, S5_STATE), 0.01)
    s5_log_step = jax.random.uniform(next(ks), (L, N_DIR, S5_GROUPS), f32, math.log(1e-3), math.log(1e-1))
    s5_b_re = nrm((L, N_DIR, S5_GROUPS, S5_STATE, S5_GROUP), (2 * S5_GROUP) ** -0.5)
    s5_b_im = nrm((L, N_DIR, S5_GROUPS, S5_STATE, S5_GROUP), (2 * S5_GROUP) ** -0.5)
    s5_c_re = nrm((L, N_DIR, S5_GROUPS, S5_GROUP, S5_STATE), S5_STATE ** -0.5)
    s5_c_im = nrm((L, N_DIR, S5_GROUPS, S5_GROUP, S5_STATE), S5_STATE ** -0.5)
    s5_d = nrm((L, S5_WIDTH), 0.5)
    w_glu = nrm((L, S5_WIDTH, 2 * S5_WIDTH), S5_WIDTH ** -0.5)
    b_glu = nrm((L, 2 * S5_WIDTH), 0.02)
    w_b_out = nrm((L, S5_WIDTH, D_MODEL), S5_WIDTH ** -0.5)
    w_o = nrm((L, D_MODEL, D_MODEL), D_MODEL ** -0.5)
    norm2_w = gain((L, D_MODEL))
    w_up = nrm((L, D_MODEL, 2 * D_FF), D_MODEL ** -0.5)
    ffn_conv_w = nrm((L, FFN_CONV, FFN_CONV, 2 * D_FF), 1.0 / FFN_CONV)
    w_down = nrm((L, D_FF, D_MODEL), D_FF ** -0.5)
    norm_f_w = gain((D_MODEL,))
    return {"x": x, "c": c, "ctx": ctx, "c_ctx": c_ctx, "w_ada": w_ada, "b_ada": b_ada,
            "norm1_w": norm1_w, "w_in": w_in, "dn_conv_w": dn_conv_w, "dn_a_log": dn_a_log,
            "dn_dt_bias": dn_dt_bias, "dn_norm_w": dn_norm_w, "w_a_out": w_a_out,
            "s5_a_re": s5_a_re, "s5_a_im": s5_a_im, "s5_log_step": s5_log_step,
            "s5_b_re": s5_b_re, "s5_b_im": s5_b_im, "s5_c_re": s5_c_re, "s5_c_im": s5_c_im,
            "s5_d": s5_d, "w_glu": w_glu, "b_glu": b_glu, "w_b_out": w_b_out, "w_o": w_o,
            "norm2_w": norm2_w, "w_up": w_up, "ffn_conv_w": ffn_conv_w, "w_down": w_down,
            "norm_f_w": norm_f_w}


def reference(x, c, ctx, c_ctx, w_ada, b_ada, norm1_w, w_in, dn_conv_w, dn_a_log, dn_dt_bias,
              dn_norm_w, w_a_out, s5_a_re, s5_a_im, s5_log_step, s5_b_re, s5_b_im, s5_c_re,
              s5_c_im, s5_d, w_glu, b_glu, w_b_out, w_o, norm2_w, w_up, ffn_conv_w, w_down,
              norm_f_w):
    rows = x.shape[1] // GRID_W
    split_idx = np.cumsum(IN_SIZES)[:-1].tolist()
    xl, xc = x, ctx
    sc = jax.nn.silu(c)
    scc = jax.nn.silu(c_ctx)
    for l in range(DEPTH):
        need_ctx = l < DEPTH - 1
        mod_l = jnp.split((sc @ w_ada[l] + b_ada[l])[:, None, :], N_MOD, axis=-1)
        mod_c = jnp.split(scc @ w_ada[l] + b_ada[l], N_MOD, axis=-1)

        hl = modulate(rmsnorm(xl, norm1_w[l]), mod_l[0], mod_l[1])
        hc = modulate(rmsnorm(xc, norm1_w[l]), mod_c[0], mod_c[1])
        pl = jnp.split(hl @ w_in[l], split_idx, axis=-1)
        pc = jnp.split(hc @ w_in[l], split_idx, axis=-1)
        ya_c, ya_l = delta_branch(pc[:6], pl[:6], dn_conv_w[l], dn_a_log[l], dn_dt_bias[l],
                                  dn_norm_w[l], w_a_out[l], need_ctx)
        yb_c, yb_l = s5_branch(pc[6], pl[6], s5_a_re[l], s5_a_im[l], s5_log_step[l], s5_b_re[l],
                               s5_b_im[l], s5_c_re[l], s5_c_im[l], s5_d[l], w_glu[l], b_glu[l],
                               w_b_out[l], need_ctx)
        mix_l = (jax.nn.sigmoid(pl[7]) * ya_l + jax.nn.sigmoid(pl[8]) * yb_l) @ w_o[l]
        xl = xl + mod_l[2] * mix_l
        if need_ctx:
            mix_c = (jax.nn.sigmoid(pc[7]) * ya_c + jax.nn.sigmoid(pc[8]) * yb_c) @ w_o[l]
            xc = xc + mod_c[2] * mix_c

        hl = modulate(rmsnorm(xl, norm2_w[l]), mod_l[3], mod_l[4])
        xl = xl + mod_l[5] * conv_ffn(hl, w_up[l], ffn_conv_w[l], w_down[l], rows)
        if need_ctx:
            hc = modulate(rmsnorm(xc, norm2_w[l]), mod_c[3], mod_c[4])
            xc = xc + mod_c[5] * conv_ffn(hc, w_up[l], ffn_conv_w[l], w_down[l], 1)
    return rmsnorm(xl, norm_f_w)
```

```python
import functools

import jax
import jax.numpy as jnp
from jax import lax
from jax.experimental import pallas as pl
from jax.experimental.pallas import tpu as pltpu

F32 = jnp.float32
BF16 = jnp.bfloat16

GRID_W = 64
N_DIR = 2
DN_HEADS = 4
DN_HEAD_DIM = 128
DN_WIDTH = DN_HEADS * DN_HEAD_DIM
DN_CHUNK = 64
S5_WIDTH = 512
S5_GROUP = 16
S5_GROUPS = S5_WIDTH // S5_GROUP
S5_STATE = 64
S5_CHUNK = 16
S5_PAIRS = S5_GROUPS // 2
N_MOD = 6
RMS_EPS = 1e-6
L2_EPS = 1e-6
LANES = 128
VMEM_LIMIT = 56 * 1024 * 1024


def _dot(a, b):
    return jnp.dot(a, b, preferred_element_type=F32)


def _dot_f32(a, b):
    return jnp.dot(a, b, preferred_element_type=F32, precision=lax.Precision.HIGHEST)


def _dot_nt(a, b):
    return lax.dot_general(a, b, (((1,), (1,)), ((), ())), preferred_element_type=F32)


def _silu(x):
    return x * jax.nn.sigmoid(x)


def _softplus(x):
    return jnp.maximum(x, 0.0) + jnp.log(1.0 + jnp.exp(-jnp.abs(x)))


def _params(sem, vmem=VMEM_LIMIT):
    return pltpu.CompilerParams(dimension_semantics=sem, vmem_limit_bytes=vmem)


def _resident(shape):
    nd = len(shape)
    return pl.BlockSpec(shape, lambda *_: (0,) * nd, pipeline_mode=pl.Buffered(1))


def _mod_kernel(c_ref, w_ref, b_ref, o_ref):
    sc = _silu(c_ref[...])
    o_ref[...] = _dot(sc.astype(BF16), w_ref[...].astype(BF16)) + b_ref[...]


def _modulation(c_rows, w_ada, b_ada):
    d, n = w_ada.shape
    tn = n // 4
    return pl.pallas_call(
        _mod_kernel,
        grid=(n // tn,),
        in_specs=[pl.BlockSpec(c_rows.shape, lambda j: (0, 0)),
                  pl.BlockSpec((d, tn), lambda j: (0, j)),
                  pl.BlockSpec((1, tn), lambda j: (0, j))],
        out_specs=pl.BlockSpec((c_rows.shape[0], tn), lambda j: (0, j)),
        out_shape=jax.ShapeDtypeStruct((c_rows.shape[0], n), F32),
        compiler_params=_params(("arbitrary",)),
        name="adaln_mod",
    )(c_rows, w_ada, b_ada.reshape(1, n))


def _inproj_kernel(x_ref, mod_ref, nw_ref, wqkv_ref, wba_ref, wu_ref, wz_ref, wg_ref,
                   qkv_ref, ba_ref, u_ref, z_ref, g_ref, *, d):
    x = x_ref[0]
    ms = jnp.mean(x * x, axis=-1, keepdims=True)
    h = x * lax.rsqrt(ms + RMS_EPS) * nw_ref[...]
    shift = mod_ref[0, :, 0:d]
    scale = mod_ref[0, :, d:2 * d]
    hb = (h * (1.0 + scale) + shift).astype(BF16)
    qkv_ref[0] = _dot(hb, wqkv_ref[...]).astype(BF16)
    ba_ref[0] = _dot(hb, wba_ref[...])
    u_ref[0] = _dot(hb, wu_ref[...]).astype(BF16)
    if z_ref is not None:
        z_ref[0] = _dot(hb, wz_ref[...]).astype(BF16)
        g_ref[0] = _dot(hb, wg_ref[...]).astype(BF16)


def _inproj_ctx_kernel(x_ref, mod_ref, nw_ref, wqkv_ref, wba_ref, wu_ref,
                       qkv_ref, ba_ref, u_ref, *, d):
    _inproj_kernel(x_ref, mod_ref, nw_ref, wqkv_ref, wba_ref, wu_ref, None, None,
                   qkv_ref, ba_ref, u_ref, None, None, d=d)


def _in_proj(x, mods, mod_row0, norm_w, wqkv, wba, wu, wz=None, wg=None, *, tm):
    b, t, d = x.shape
    full = wz is not None
    tok = lambda n: pl.BlockSpec((1, tm, n), lambda i, j: (i, j, 0))
    in_specs = [tok(d),
                pl.BlockSpec((1, 1, mods.shape[-1]), lambda i, j: (mod_row0(i), 0, 0)),
                _resident((1, d)), _resident(wqkv.shape), _resident(wba.shape),
                _resident(wu.shape)]
    args = [x, mods, norm_w.reshape(1, d), wqkv, wba, wu]
    out_specs = [tok(wqkv.shape[1]), tok(wba.shape[1]), tok(wu.shape[1])]
    out_shape = [jax.ShapeDtypeStruct((b, t, wqkv.shape[1]), BF16),
                 jax.ShapeDtypeStruct((b, t, wba.shape[1]), F32),
                 jax.ShapeDtypeStruct((b, t, wu.shape[1]), BF16)]
    if full:
        in_specs += [_resident(wz.shape), _resident(wg.shape)]
        args += [wz, wg]
        out_specs += [tok(wz.shape[1]), tok(wg.shape[1])]
        out_shape += [jax.ShapeDtypeStruct((b, t, wz.shape[1]), BF16),
                      jax.ShapeDtypeStruct((b, t, wg.shape[1]), BF16)]
    body = functools.partial(_inproj_kernel if full else _inproj_ctx_kernel, d=d)
    return pl.pallas_call(
        body, grid=(b, t // tm), in_specs=in_specs, out_specs=out_specs, out_shape=out_shape,
        compiler_params=_params(("parallel", "parallel")),
        name="in_proj" if full else "in_proj_ctx",
    )(*args)


HALO = 16


def _dprep_kernel(x_ref, xp_ref, xn_ref, ba_ref, cw_ref, alog_ref, dtb_ref,
                  q_ref, k_ref, v_ref, beta_ref, g_ref, *, tm, nt):
    t = pl.program_id(1)
    x = x_ref[0].astype(F32)
    prow = jnp.where(t == 0, 0.0, xp_ref[0, HALO - 1:HALO, :].astype(F32))
    nrow = jnp.where(t == nt - 1, 0.0, xn_ref[0, 0:1, :].astype(F32))
    rows = lax.broadcasted_iota(jnp.int32, x.shape, 0)
    xprev = jnp.where(rows == 0, prow, pltpu.roll(x, 1, 0))
    xnext = jnp.where(rows == tm - 1, nrow, pltpu.roll(x, tm - 1, 0))
    y = _silu(xprev * cw_ref[0:1, :] + x * cw_ref[1:2, :] + xnext * cw_ref[2:3, :])
    for h in range(DN_HEADS):
        sl = slice(h * DN_HEAD_DIM, (h + 1) * DN_HEAD_DIM)
        qh = y[:, h * DN_HEAD_DIM:(h + 1) * DN_HEAD_DIM]
        kh = y[:, DN_WIDTH + h * DN_HEAD_DIM:DN_WIDTH + (h + 1) * DN_HEAD_DIM]
        qn = qh * lax.rsqrt(jnp.sum(qh * qh, axis=-1, keepdims=True) + L2_EPS)
        kn = kh * lax.rsqrt(jnp.sum(kh * kh, axis=-1, keepdims=True) + L2_EPS)
        q_ref[0, :, sl] = (qn * (DN_HEAD_DIM ** -0.5)).astype(BF16)
        k_ref[0, :, sl] = kn.astype(BF16)
    v_ref[0] = y[:, 2 * DN_WIDTH:3 * DN_WIDTH].astype(BF16)
    ba = ba_ref[0]
    beta_ref[0] = jax.nn.sigmoid(ba[:, 0:LANES])
    g_ref[0] = -jnp.exp(alog_ref[...]) * _softplus(ba[:, LANES:2 * LANES] + dtb_ref[...])


def _delta_prep(qkv, ba, conv_w, alog_row, dtb_row, *, tm):
    b, t, c = qkv.shape
    nt = t // tm
    r = tm // HALO
    tok = lambda n: pl.BlockSpec((1, tm, n), lambda i, j: (i, j, 0))
    return pl.pallas_call(
        functools.partial(_dprep_kernel, tm=tm, nt=nt),
        grid=(b, nt),
        in_specs=[tok(c),
                  pl.BlockSpec((1, HALO, c), lambda i, j: (i, jnp.maximum(j * r - 1, 0), 0)),
                  pl.BlockSpec((1, HALO, c), lambda i, j: (i, jnp.minimum((j + 1) * r, t // HALO - 1), 0)),
                  tok(ba.shape[-1]),
                  _resident(conv_w.shape), _resident(alog_row.shape), _resident(dtb_row.shape)],
        out_specs=[tok(DN_WIDTH), tok(DN_WIDTH), tok(DN_WIDTH), tok(LANES), tok(LANES)],
        out_shape=[jax.ShapeDtypeStruct((b, t, DN_WIDTH), BF16)] * 3
        + [jax.ShapeDtypeStruct((b, t, LANES), F32)] * 2,
        compiler_params=_params(("parallel", "parallel")),
        name="delta_prep",
    )(qkv, qkv, qkv, ba, conv_w, alog_row, dtb_row)


def _delta_chunk(q, k, v, beta_c, g_c, g_r, g_end, s, incl, strict, eye):
    kf = k.astype(F32)
    decay = jnp.where(incl, jnp.exp(jnp.where(incl, g_c - g_r, 0.0)), 0.0)
    eg = jnp.exp(g_c)
    kb = kf * beta_c
    kq = jnp.concatenate([kb.astype(BF16), q], axis=0)
    kk_qk = _dot_nt(kq, k)
    c = k.shape[0]
    lmat = jnp.where(strict, kk_qk[:c] * decay, 0.0)
    a_intra = kk_qk[c:] * decay
    p = -lmat
    tinv = eye + p
    n = 2
    while n < c:
        p = _dot_f32(p, p)
        tinv = tinv + _dot_f32(tinv, p)
        n *= 2
    rhs = jnp.concatenate([(v.astype(F32) * beta_c).astype(BF16), (kb * eg).astype(BF16)], axis=1)
    sol = _dot(tinv.astype(BF16), rhs)
    dv = v.shape[1]
    u, w = sol[:, :dv], sol[:, dv:]
    sb = s.astype(BF16)
    wq = jnp.concatenate([w.astype(BF16), (q.astype(F32) * eg).astype(BF16)], axis=0)
    ws_qs = _dot(wq, sb)
    v_new = (u - ws_qs[:c]).astype(BF16)
    o = ws_qs[c:] + _dot(a_intra.astype(BF16), v_new)
    k_dec = (kf * jnp.exp(g_end - g_c)).T.astype(BF16)
    s_new = s * jnp.exp(g_end) + _dot(k_dec, v_new)
    return o, s_new


def _dscan_kernel(qf_ref, kf_ref, vf_ref, bf_ref, gf_ref, qr_ref, kr_ref, vr_ref, br_ref, gr_ref,
                  s0_ref, of_ref, or_ref, sfin_ref, s_ref, *, nc):
    i = pl.program_id(1)

    @pl.when(i == 0)
    def _():
        s_ref[...] = s0_ref[0]

    c = DN_CHUNK
    ii = lax.broadcasted_iota(jnp.int32, (c, c), 0)
    jj = lax.broadcasted_iota(jnp.int32, (c, c), 1)
    eye = (ii == jj).astype(F32)
    dirs = (
        (qf_ref, kf_ref, vf_ref, bf_ref, gf_ref, of_ref, ii >= jj, ii > jj, False),
        (qr_ref, kr_ref, vr_ref, br_ref, gr_ref, or_ref, ii <= jj, ii < jj, True),
    )

    def chunk_step(j, carry):
        for r, (q_ref, k_ref, v_ref, b_ref, g_ref, o_ref, incl, strict, rev) in enumerate(dirs):
            row0 = pl.multiple_of((nc - 1 - j) * c if rev else j * c, c)
            rs = pl.ds(row0, c)
            g_cum = _dot_f32(incl.astype(F32), g_ref[0, rs, :])
            g_cum_t = g_cum.T
            beta = b_ref[0, rs, :]
            for h in range(DN_HEADS):
                lane = r * DN_HEADS + h
                hs = slice(h * DN_HEAD_DIM, (h + 1) * DN_HEAD_DIM)
                g_c = g_cum[:, lane:lane + 1]
                g_end = g_c[0:1, :] if rev else g_c[c - 1:c, :]
                o, s_new = _delta_chunk(
                    q_ref[0, rs, hs], k_ref[0, rs, hs], v_ref[0, rs, hs],
                    beta[:, lane:lane + 1], g_c, g_cum_t[lane:lane + 1, :], g_end,
                    s_ref[lane], incl, strict, eye)
                o_ref[0, rs, hs] = o.astype(BF16)
                s_ref[lane] = s_new
        return carry

    lax.fori_loop(0, nc, chunk_step, 0)

    @pl.when(i == pl.num_programs(1) - 1)
    def _():
        sfin_ref[0] = s_ref[...]


def _delta_scan(q, k, v, beta, g, s0, *, ts):
    b, t, w = q.shape
    n = t // ts
    fwd = lambda width: pl.BlockSpec((1, ts, width), lambda i, j: (i, j, 0))
    rev = lambda width: pl.BlockSpec((1, ts, width), lambda i, j: (i, n - 1 - j, 0))
    st = pl.BlockSpec((1,) + s0.shape[1:], lambda i, j: (i, 0, 0, 0))
    return pl.pallas_call(
        functools.partial(_dscan_kernel, nc=ts // DN_CHUNK),
        grid=(b, n),
        in_specs=[fwd(w), fwd(w), fwd(w), fwd(LANES), fwd(LANES),
                  rev(w), rev(w), rev(w), rev(LANES), rev(LANES), st],
        out_specs=[fwd(w), rev(w), st],
        out_shape=[jax.ShapeDtypeStruct((b, t, w), BF16)] * 2
        + [jax.ShapeDtypeStruct(s0.shape, F32)],
        scratch_shapes=[pltpu.VMEM(s0.shape[1:], F32)],
        compiler_params=_params(("parallel", "arbitrary")),
        name="delta_scan",
    )(q, k, v, beta, g, q, k, v, beta, g, s0)


def _cexp(re, im):
    m = jnp.exp(re)
    return m * jnp.cos(im), m * jnp.sin(im)


def _s5_param_kernel(arw_ref, aiw_ref, lsw_ref, brw_ref, biw_ref,
                     arm_ref, aim_ref, lsm_ref, crm_ref, cim_ref,
                     ctr_ref, cti_ref, arr_ref, air_ref, lsr_ref,
                     wkr_ref, wki_ref, kall_ref, k0_ref, mtop_ref, mbot_ref, llr_ref, lli_ref):
    k0 = None
    for r in range(N_DIR):
        are, aim = arw_ref[r, 0], aiw_ref[r, 0]
        dt = jnp.exp(lsw_ref[r, 0])
        kk = (lax.broadcasted_iota(jnp.int32, are.shape, 0) // S5_GROUP).astype(F32)
        lbr, lbi = _cexp(are * dt, aim * dt)
        den = are * are + aim * aim
        nr, ni = lbr - 1.0, lbi
        cr = (nr * are + ni * aim) / den
        ci = (ni * are - nr * aim) / den
        bre, bim = brw_ref[r, 0], biw_ref[r, 0]
        bbr = cr * bre - ci * bim
        bbi = cr * bim + ci * bre
        pr, pi = _cexp(kk * are * dt, kk * aim * dt)
        wr = bbr * pr - bbi * pi
        wi = bbr * pi + bbi * pr
        wkr_ref[r, 0] = wr
        wki_ref[r, 0] = wi
        kall = _dot_f32(wr, ctr_ref[r, 0]) - _dot_f32(wi, cti_ref[r, 0])
        kall_ref[r, 0] = kall
        k0 = kall[0:S5_GROUP] if k0 is None else k0 + kall[0:S5_GROUP]
        are, aim = arm_ref[r, 0], aim_ref[r, 0]
        dt = jnp.exp(lsm_ref[r, 0])
        ee = (lax.broadcasted_iota(jnp.int32, are.shape, 1) // S5_GROUP + 1).astype(F32)
        pr, pi = _cexp(ee * are * dt, ee * aim * dt)
        cre, cim = crm_ref[r, 0], cim_ref[r, 0]
        mtop_ref[r, 0] = cre * pr - cim * pi
        mbot_ref[r, 0] = -(cre * pi + cim * pr)
        are, aim = arr_ref[r, 0], air_ref[r, 0]
        dt = jnp.exp(lsr_ref[r, 0])
        lr, li = _cexp(S5_CHUNK * are * dt, S5_CHUNK * aim * dt)
        llr_ref[r, 0] = lr
        lli_ref[r, 0] = li
    k0_ref[0] = k0


def _s5_operators(a_re, a_im, log_step, b_re, b_im, c_re, c_im):
    g, n, p, l = S5_GROUPS, S5_STATE, S5_GROUP, S5_CHUNK
    lp = l * p
    bc = lambda a, shape: jnp.broadcast_to(a, shape)
    w_shape, m_shape = (N_DIR, g, lp, n), (N_DIR, g, n, lp)
    ls = log_step[:, :, None, None]
    arw, aiw, lsw = bc(a_re[:, :, None, :], w_shape), bc(a_im[:, :, None, :], w_shape), bc(ls, w_shape)
    brw = jnp.tile(jnp.swapaxes(b_re, 2, 3), (1, 1, l, 1))
    biw = jnp.tile(jnp.swapaxes(b_im, 2, 3), (1, 1, l, 1))
    arm, aim, lsm = bc(a_re[:, :, :, None], m_shape), bc(a_im[:, :, :, None], m_shape), bc(ls, m_shape)
    ctr, cti = jnp.swapaxes(c_re, 2, 3), jnp.swapaxes(c_im, 2, 3)
    crm, cim = jnp.tile(ctr, (1, 1, 1, l)), jnp.tile(cti, (1, 1, 1, l))
    arr, air, lsr = a_re[:, :, None, :], a_im[:, :, None, :], bc(ls, (N_DIR, g, 1, n))
    blk = lambda r, c: pl.BlockSpec((N_DIR, 1, r, c), lambda i: (0, i, 0, 0))
    sds = lambda r, c: jax.ShapeDtypeStruct((N_DIR, g, r, c), F32)
    wkr, wki, kall, k0, mtop, mbot, llr, lli = pl.pallas_call(
        _s5_param_kernel,
        grid=(g,),
        in_specs=[blk(lp, n)] * 5 + [blk(n, lp)] * 5 + [blk(n, p)] * 2 + [blk(1, n)] * 3,
        out_specs=[blk(lp, n), blk(lp, n), blk(lp, p), pl.BlockSpec((1, p, p), lambda i: (i, 0, 0)),
                   blk(n, lp), blk(n, lp), blk(1, n), blk(1, n)],
        out_shape=[sds(lp, n), sds(lp, n), sds(lp, p), jax.ShapeDtypeStruct((g, p, p), F32),
                   sds(n, lp), sds(n, lp), sds(1, n), sds(1, n)],
        compiler_params=_params(("parallel",)),
        name="s5_params",
    )(arw, aiw, lsw, brw, biw, arm, aim, lsm, crm, cim, ctr, cti, arr, air, lsr)

    def flip_blocks(a, axis):
        shp = a.shape
        a = a.reshape(shp[:axis] + (l, p) + shp[axis + 1:])
        return jnp.flip(a, axis=axis).reshape(shp)

    def pair_diag(a):
        gg, r, c = a.shape
        a = a.reshape(gg // 2, 2, r, c)
        z = jnp.zeros_like(a[:, 0])
        return jnp.concatenate([jnp.concatenate([a[:, 0], z], axis=2),
                                jnp.concatenate([z, a[:, 1]], axis=2)], axis=1)

    wz = [pair_diag(flip_blocks(w[0], 1)) for w in (wkr, wki)] + [pair_diag(w[1]) for w in (wkr, wki)]
    w_in = jnp.concatenate(wz, axis=2).astype(BF16)
    m_f = [pair_diag(m[0]) for m in (mtop, mbot)]
    m_r = [pair_diag(flip_blocks(m[1], 2)) for m in (mtop, mbot)]
    kf = kall[0].reshape(g, l, p, p)
    kr = kall[1].reshape(g, l, p, p)
    d = jnp.arange(l)[None, :] - jnp.arange(l)[:, None]
    tf = kf[:, jnp.clip(d, 0, l - 1)]
    tr = kr[:, jnp.clip(-d, 0, l - 1)]
    t0 = jnp.broadcast_to(k0[:, None, None], tf.shape)
    dd = d[None, :, :, None, None]
    toep = jnp.where(dd > 0, tf, jnp.where(dd < 0, tr, t0))
    toep = jnp.transpose(toep, (0, 1, 3, 2, 4)).reshape(g, lp, lp)
    w_out = jnp.concatenate([pair_diag(toep)] + m_f + m_r, axis=1).astype(BF16)
    lam = jnp.stack([llr.reshape(N_DIR, g * n), lli.reshape(N_DIR, g * n)])
    return w_in, w_out, lam


def _s5_in_kernel(a_ref, w_ref, zfr_ref, zfi_ref, zrr_ref, zri_ref):
    a2 = jnp.concatenate([a_ref[0], a_ref[1]], axis=1)
    z = _dot(a2, w_ref[0])
    for k, ref in enumerate((zfr_ref, zfi_ref, zrr_ref, zri_ref)):
        ref[...] = z[:, k * LANES:(k + 1) * LANES]


def _s5_chunk_in(a, w_in, *, tr):
    g, rows, lp = a.shape
    out = pl.BlockSpec((tr, LANES), lambda i, j: (j, i))
    return pl.pallas_call(
        _s5_in_kernel,
        grid=(g // 2, rows // tr),
        in_specs=[pl.BlockSpec((2, tr, lp), lambda i, j: (i, j, 0)),
                  pl.BlockSpec((1,) + w_in.shape[1:], lambda i, j: (i, 0, 0))],
        out_specs=[out] * 4,
        out_shape=[jax.ShapeDtypeStruct((rows, g // 2 * LANES), F32)] * 4,
        compiler_params=_params(("parallel", "parallel")),
        name="s5_chunk_in",
    )(a, w_in)


def _s5_scan_kernel(zfr_ref, zfi_ref, zrr_ref, zri_ref, lam_ref, x0fr_ref, x0fi_ref, x0rr_ref, x0ri_ref,
                    xfr_ref, xfi_ref, xrr_ref, xri_ref, efr_ref, efi_ref, err_ref, eri_ref, *, npairs):
    half = lax.broadcasted_iota(jnp.int32, x0fr_ref.shape, 0) < 4
    lfr, lrr = lam_ref[0, 0:1, :], lam_ref[0, 1:2, :]
    lfi, lri = lam_ref[1, 0:1, :], lam_ref[1, 1:2, :]

    def advance(lr, li, xr, xi, zr, zi):
        return lr * xr - li * xi + zr, lr * xi + li * xr + zi

    def step(k, carry):
        cfr, cfi, crr, cri = carry
        zr, zi = zfr_ref[k], zfi_ref[k]
        tr_, ti_ = advance(lfr, lfi, cfr, cfi, zr, zi)
        pr = jnp.where(half, cfr, pltpu.roll(tr_, 4, 0))
        pi = jnp.where(half, cfi, pltpu.roll(ti_, 4, 0))
        xfr_ref[k], xfi_ref[k] = pr, pi
        tr_, ti_ = advance(lfr, lfi, pr, pi, zr, zi)
        cfr, cfi = pltpu.roll(tr_, 4, 0), pltpu.roll(ti_, 4, 0)
        kk = npairs - 1 - k
        zr, zi = zrr_ref[kk], zri_ref[kk]
        tr_, ti_ = advance(lrr, lri, crr, cri, zr, zi)
        pr = jnp.where(half, pltpu.roll(tr_, 4, 0), crr)
        pi = jnp.where(half, pltpu.roll(ti_, 4, 0), cri)
        xrr_ref[kk], xri_ref[kk] = pr, pi
        tr_, ti_ = advance(lrr, lri, pr, pi, zr, zi)
        crr, cri = pltpu.roll(tr_, 4, 0), pltpu.roll(ti_, 4, 0)
        return cfr, cfi, crr, cri

    ends = lax.fori_loop(0, npairs, step,
                         (x0fr_ref[...], x0fi_ref[...], x0rr_ref[...], x0ri_ref[...]))
    for ref, val in zip((efr_ref, efi_ref, err_ref, eri_ref), ends):
        ref[...] = val


def _s5_state_scan(z4, lam, x0, *, tl):
    rows, w = z4[0].shape
    npairs = rows // 8
    z4 = [z.reshape(npairs, 8, w) for z in z4]
    zb = pl.BlockSpec((npairs, 8, tl), lambda j: (0, 0, j))
    cb = pl.BlockSpec((8, tl), lambda j: (0, j))
    outs = pl.pallas_call(
        functools.partial(_s5_scan_kernel, npairs=npairs),
        grid=(w // tl,),
        in_specs=[zb] * 4 + [pl.BlockSpec((2, N_DIR, tl), lambda j: (0, 0, j))] + [cb] * 4,
        out_specs=[zb] * 4 + [cb] * 4,
        out_shape=[jax.ShapeDtypeStruct((npairs, 8, w), F32)] * 4
        + [jax.ShapeDtypeStruct((8, w), F32)] * 4,
        compiler_params=_params(("parallel",)),
        name="s5_state_scan",
    )(*z4, lam, *x0)
    return [x.reshape(rows, w) for x in outs[:4]], list(outs[4:])


def _s5_out_kernel(a_ref, xfr_ref, xfi_ref, xrr_ref, xri_ref, w_ref, y_ref):
    lhs = jnp.concatenate([a_ref[0], a_ref[1]] + [r[...].astype(BF16) for r in
                                                  (xfr_ref, xfi_ref, xrr_ref, xri_ref)], axis=1)
    y = _dot(lhs, w_ref[0])
    lp = y_ref.shape[-1]
    y_ref[0] = y[:, :lp].astype(BF16)
    y_ref[1] = y[:, lp:].astype(BF16)


def _s5_chunk_out(a, x4, w_out, *, tr):
    g, rows, lp = a.shape
    xb = pl.BlockSpec((tr, LANES), lambda i, j: (j, i))
    ab = pl.BlockSpec((2, tr, lp), lambda i, j: (i, j, 0))
    return pl.pallas_call(
        _s5_out_kernel,
        grid=(g // 2, rows // tr),
        in_specs=[ab] + [xb] * 4 + [pl.BlockSpec((1,) + w_out.shape[1:], lambda i, j: (i, 0, 0))],
        out_specs=ab,
        out_shape=jax.ShapeDtypeStruct(a.shape, BF16),
        compiler_params=_params(("parallel", "parallel")),
        name="s5_chunk_out",
    )(a, *x4, w_out)


def _to_chunks(u):
    b, t, _ = u.shape
    a = u.reshape(b, t // S5_CHUNK, S5_CHUNK, S5_GROUPS, S5_GROUP)
    a = jnp.transpose(a, (3, 1, 0, 2, 4))
    return a.reshape(S5_GROUPS, (t // S5_CHUNK) * b, S5_CHUNK * S5_GROUP)


def _from_chunks(y, b):
    g, rows, _ = y.shape
    nc = rows // b
    a = y.reshape(g, nc, b, S5_CHUNK, S5_GROUP)
    a = jnp.transpose(a, (2, 1, 3, 0, 4))
    return a.reshape(b, nc * S5_CHUNK, g * S5_GROUP)


def _gelu_tanh(x):
    return 0.5 * x * (1.0 + jnp.tanh(0.7978845608028654 * (x + 0.044715 * x * x * x)))


def _merge_kernel(of_ref, or_ref, z_ref, ys_ref, u_ref, g_ref, x_ref, mod_ref,
                  dnw_ref, wa_ref, dsk_ref, wglu_ref, bglu_ref, wb_ref, wo_ref, n2w_ref,
                  xl_ref, h2_ref, *, d):
    o = of_ref[0].astype(F32) + or_ref[0].astype(F32)
    z = z_ref[0].astype(F32)
    heads = []
    for h in range(DN_HEADS):
        sl = slice(h * DN_HEAD_DIM, (h + 1) * DN_HEAD_DIM)
        oh = o[:, sl]
        on = oh * lax.rsqrt(jnp.mean(oh * oh, axis=-1, keepdims=True) + RMS_EPS) * dnw_ref[...]
        heads.append((on * _silu(z[:, sl])).astype(BF16))
    ya = _dot(jnp.concatenate(heads, axis=1), wa_ref[...])
    ys = ys_ref[0].astype(F32) + dsk_ref[...] * u_ref[0].astype(F32)
    zz = _dot(_gelu_tanh(ys).astype(BF16), wglu_ref[...]) + bglu_ref[...]
    yb = _dot((zz[:, :S5_WIDTH] * jax.nn.sigmoid(zz[:, S5_WIDTH:])).astype(BF16), wb_ref[...])
    gates = g_ref[0].astype(F32)
    mix = jax.nn.sigmoid(gates[:, :d]) * ya + jax.nn.sigmoid(gates[:, d:]) * yb
    xl = x_ref[0] + mod_ref[0, :, 2 * d:3 * d] * _dot(mix.astype(BF16), wo_ref[...])
    xl_ref[0] = xl
    hn = xl * lax.rsqrt(jnp.mean(xl * xl, axis=-1, keepdims=True) + RMS_EPS) * n2w_ref[...]
    h2_ref[0] = (hn * (1.0 + mod_ref[0, :, 4 * d:5 * d]) + mod_ref[0, :, 3 * d:4 * d]).astype(BF16)


def _mix_merge(o_f, o_r, z, ys, u, gates, x, mods, dn_norm_w, w_a_out, s5_d, w_glu, b_glu,
               w_b_out, w_o, norm2_w, *, tm):
    b, t, d = x.shape
    tok = lambda n: pl.BlockSpec((1, tm, n), lambda i, j: (i, j, 0))
    consts = [dn_norm_w.reshape(1, -1), w_a_out, s5_d.reshape(1, -1), w_glu, b_glu.reshape(1, -1),
              w_b_out, w_o, norm2_w.reshape(1, -1)]
    return pl.pallas_call(
        functools.partial(_merge_kernel, d=d),
        grid=(b, t // tm),
        in_specs=[tok(o_f.shape[-1]), tok(o_r.shape[-1]), tok(z.shape[-1]), tok(ys.shape[-1]),
                  tok(u.shape[-1]), tok(gates.shape[-1]), tok(d),
                  pl.BlockSpec((1, 1, mods.shape[-1]), lambda i, j: (i, 0, 0))]
        + [_resident(c.shape) for c in consts],
        out_specs=[tok(d), tok(d)],
        out_shape=[jax.ShapeDtypeStruct((b, t, d), F32), jax.ShapeDtypeStruct((b, t, d), BF16)],
        compiler_params=_params(("parallel", "parallel")),
        name="mix_merge",
    )(o_f, o_r, z, ys, u, gates, x, mods, *consts)


FFN_ROWS = 8
FFN_CB = 256


def _dwconv3x3(e, cw, n_out):
    n = e.shape[0]
    col = lax.broadcasted_iota(jnp.int32, e.shape, 0) % GRID_W
    left = jnp.where(col == 0, 0.0, pltpu.roll(e, 1, 0))
    right = jnp.where(col == GRID_W - 1, 0.0, pltpu.roll(e, n - 1, 0))
    out = None
    for i in range(3):
        rs = slice(i * GRID_W, i * GRID_W + n_out)
        term = (left[rs] * cw[3 * i:3 * i + 1, :] + e[rs] * cw[3 * i + 1:3 * i + 2, :]
                + right[rs] * cw[3 * i + 2:3 * i + 3, :])
        out = term if out is None else out + term
    return out


def _ffn_kernel(h_ref, hp_ref, hn_ref, xl_ref, mod_ref, wg_ref, wv_ref, cg_ref, cv_ref, wd_ref, nfw_ref,
                o_ref, acc_ref, *, d, nt, ncb):
    t = pl.program_id(1)
    top = jnp.where(t == 0, jnp.zeros_like(hp_ref[0]), hp_ref[0])
    bot = jnp.where(t == nt - 1, jnp.zeros_like(hn_ref[0]), hn_ref[0])
    hext = jnp.concatenate([top, h_ref[0], bot], axis=0)
    n_out = h_ref.shape[1]
    acc_ref[...] = jnp.zeros_like(acc_ref)

    def cblock(k, carry):
        gate = _dwconv3x3(_dot(hext, wg_ref[k]), cg_ref[k], n_out)
        val = _dwconv3x3(_dot(hext, wv_ref[k]), cv_ref[k], n_out)
        acc_ref[...] += _dot((_silu(gate) * val).astype(BF16), wd_ref[k])
        return carry

    lax.fori_loop(0, ncb, cblock, 0)
    xo = xl_ref[0] + mod_ref[0, :, 5 * d:6 * d] * acc_ref[...]
    o_ref[0] = xo * lax.rsqrt(jnp.mean(xo * xo, axis=-1, keepdims=True) + RMS_EPS) * nfw_ref[...]


def _conv_ffn(h2, xl, mods, w_up, conv_w, w_down, norm_f_w):
    b, t, d = xl.shape
    dff = w_down.shape[0]
    ncb = dff // FFN_CB
    tm = FFN_ROWS * GRID_W
    nt = t // tm
    nrow = t // GRID_W
    split = lambda w: jnp.transpose(w.reshape(w.shape[0], ncb, FFN_CB), (1, 0, 2))
    wg, wv = split(w_up[:, :dff].astype(BF16)), split(w_up[:, dff:].astype(BF16))
    cw = conv_w.reshape(9, 2 * dff)
    cg, cv = split(cw[:, :dff]), split(cw[:, dff:])
    wd = w_down.astype(BF16).reshape(ncb, FFN_CB, d)
    tok = lambda: pl.BlockSpec((1, tm, d), lambda i, j: (i, j, 0))
    return pl.pallas_call(
        functools.partial(_ffn_kernel, d=d, nt=nt, ncb=ncb),
        grid=(b, nt),
        in_specs=[tok(),
                  pl.BlockSpec((1, GRID_W, d), lambda i, j: (i, jnp.maximum(j * FFN_ROWS - 1, 0), 0)),
                  pl.BlockSpec((1, GRID_W, d), lambda i, j: (i, jnp.minimum((j + 1) * FFN_ROWS, nrow - 1), 0)),
                  tok(),
                  pl.BlockSpec((1, 1, mods.shape[-1]), lambda i, j: (i, 0, 0)),
                  _resident(wg.shape), _resident(wv.shape), _resident(cg.shape), _resident(cv.shape),
                  _resident(wd.shape), _resident((1, d))],
        out_specs=tok(),
        out_shape=jax.ShapeDtypeStruct((b, t, d), F32),
        scratch_shapes=[pltpu.VMEM((tm, d), F32)],
        compiler_params=_params(("parallel", "parallel")),
        name="conv_ffn",
    )(h2, h2, h2, xl, mods, wg, wv, cg, cv, wd, norm_f_w.reshape(1, d))


def _lane_row(a):
    a = a.reshape(-1).astype(F32)
    return jnp.pad(a, (0, LANES - a.shape[0])).reshape(1, LANES)


def _pad_cols(w):
    return jnp.pad(w, ((0, 0), (0, LANES - w.shape[1])))


def kernel(x, c, ctx, c_ctx, w_ada, b_ada, norm1_w, w_in, dn_conv_w, dn_a_log, dn_dt_bias, dn_norm_w,
           w_a_out, s5_a_re, s5_a_im, s5_log_step, s5_b_re, s5_b_im, s5_c_re, s5_c_im, s5_d, w_glu,
           b_glu, w_b_out, w_o, norm2_w, w_up, ffn_conv_w, w_down, norm_f_w):
    assert w_ada.shape[0] == 1, "single-layer block"
    b, t, d = x.shape
    tc = ctx.shape[1]
    nh = N_DIR * DN_HEADS

    c_rows = jnp.zeros((8, d), F32).at[:b].set(c).at[b].set(c_ctx)
    mods = _modulation(c_rows, w_ada[0], b_ada[0]).reshape(8, 1, N_MOD * d)

    w = w_in[0]
    o_z, o_b, o_a = 3 * DN_WIDTH, 4 * DN_WIDTH, 4 * DN_WIDTH + nh
    o_u = o_a + nh
    o_g = o_u + S5_WIDTH
    wqkv = w[:, :o_z].astype(BF16)
    wz = w[:, o_z:o_b].astype(BF16)
    wba = jnp.concatenate([_pad_cols(w[:, o_b:o_a]), _pad_cols(w[:, o_a:o_u])], axis=1).astype(BF16)
    wu = w[:, o_u:o_g].astype(BF16)
    wg = w[:, o_g:].astype(BF16)

    qkv_l, ba_l, u_l, z_l, gates_l = _in_proj(x, mods, lambda i: i, norm1_w[0], wqkv, wba, wu, wz, wg, tm=512)
    qkv_c, ba_c, u_c = _in_proj(ctx, mods, lambda i: b, norm1_w[0], wqkv, wba, wu, tm=tc)

    alog_row, dtb_row = _lane_row(dn_a_log[0]), _lane_row(dn_dt_bias[0])
    prep_c = _delta_prep(qkv_c, ba_c, dn_conv_w[0], alog_row, dtb_row, tm=tc)
    prep_l = _delta_prep(qkv_l, ba_l, dn_conv_w[0], alog_row, dtb_row, tm=512)
    s0 = jnp.zeros((b, nh, DN_HEAD_DIM, DN_HEAD_DIM), F32)
    _, _, s_ctx = _delta_scan(*prep_c, s0, ts=tc)
    o_f, o_r, _ = _delta_scan(*prep_l, s_ctx, ts=256)

    w_s5in, w_s5out, lam = _s5_operators(s5_a_re[0], s5_a_im[0], s5_log_step[0], s5_b_re[0], s5_b_im[0],
                                         s5_c_re[0], s5_c_im[0])
    a_c, a_l = _to_chunks(u_c), _to_chunks(u_l)
    zero = jnp.zeros((8, S5_GROUPS * S5_STATE), F32)
    zs_c = _s5_chunk_in(a_c, w_s5in, tr=a_c.shape[1])
    _, x_ctx = _s5_state_scan(zs_c, lam, [zero] * 4, tl=512)
    zs_l = _s5_chunk_in(a_l, w_s5in, tr=512)
    x_l, _ = _s5_state_scan(zs_l, lam, x_ctx, tl=512)
    y_s5 = _from_chunks(_s5_chunk_out(a_l, x_l, w_s5out, tr=512), b)

    xl, h2 = _mix_merge(o_f, o_r, z_l, y_s5, u_l, gates_l, x, mods, dn_norm_w[0], w_a_out[0].astype(BF16),
                        s5_d[0], w_glu[0].astype(BF16), b_glu[0], w_b_out[0].astype(BF16),
                        w_o[0].astype(BF16), norm2_w[0], tm=512)
    return _conv_ffn(h2, xl, mods, w_up[0], ffn_conv_w[0], w_down[0], norm_f_w)
```

```python
import functools

import jax
import jax.numpy as jnp
from jax import lax
from jax.experimental import pallas as pl
from jax.experimental.pallas import tpu as pltpu

F32 = jnp.float32
BF16 = jnp.bfloat16

GRID_W = 64
N_DIR = 2
DN_HEADS = 4
DN_HEAD_DIM = 128
DN_WIDTH = DN_HEADS * DN_HEAD_DIM
DN_CHUNK = 64
S5_WIDTH = 512
S5_GROUP = 16
S5_GROUPS = S5_WIDTH // S5_GROUP
S5_STATE = 64
S5_CHUNK = 16
S5_PAIRS = S5_GROUPS // 2
N_MOD = 6
RMS_EPS = 1e-6
L2_EPS = 1e-6
LANES = 128
VMEM_LIMIT = 56 * 1024 * 1024


def _dot(a, b):
    return jnp.dot(a, b, preferred_element_type=F32)


def _dot_f32(a, b):
    return jnp.dot(a, b, preferred_element_type=F32, precision=lax.Precision.HIGHEST)


def _dot_nt(a, b):
    return lax.dot_general(a, b, (((1,), (1,)), ((), ())), preferred_element_type=F32)


def _silu(x):
    return x * jax.nn.sigmoid(x)


def _softplus(x):
    return jnp.maximum(x, 0.0) + jnp.log(1.0 + jnp.exp(-jnp.abs(x)))


def _params(sem, vmem=VMEM_LIMIT):
    return pltpu.CompilerParams(dimension_semantics=sem, vmem_limit_bytes=vmem)


def _resident(shape):
    nd = len(shape)
    return pl.BlockSpec(shape, lambda *_: (0,) * nd, pipeline_mode=pl.Buffered(1))


def _mod_kernel(c_ref, w_ref, b_ref, o_ref):
    sc = _silu(c_ref[...])
    o_ref[...] = _dot(sc.astype(BF16), w_ref[...].astype(BF16)) + b_ref[...]


def _modulation(c_rows, w_ada, b_ada):
    d, n = w_ada.shape
    tn = n // 4
    return pl.pallas_call(
        _mod_kernel,
        grid=(n // tn,),
        in_specs=[pl.BlockSpec(c_rows.shape, lambda j: (0, 0)),
                  pl.BlockSpec((d, tn), lambda j: (0, j)),
                  pl.BlockSpec((1, tn), lambda j: (0, j))],
        out_specs=pl.BlockSpec((c_rows.shape[0], tn), lambda j: (0, j)),
        out_shape=jax.ShapeDtypeStruct((c_rows.shape[0], n), F32),
        compiler_params=_params(("arbitrary",)),
        name="adaln_mod",
    )(c_rows, w_ada, b_ada.reshape(1, n))


def _inproj_kernel(x_ref, mod_ref, nw_ref, wqkv_ref, wba_ref, wu_ref, wz_ref, wg_ref,
                   qkv_ref, ba_ref, u_ref, z_ref, g_ref, *, d):
    x = x_ref[0]
    ms = jnp.mean(x * x, axis=-1, keepdims=True)
    h = x * lax.rsqrt(ms + RMS_EPS) * nw_ref[...]
    shift = mod_ref[0, :, 0:d]
    scale = mod_ref[0, :, d:2 * d]
    hb = (h * (1.0 + scale) + shift).astype(BF16)
    qkv_ref[0] = _dot(hb, wqkv_ref[...]).astype(BF16)
    ba_ref[0] = _dot(hb, wba_ref[...])
    u_ref[0] = _dot(hb, wu_ref[...]).astype(BF16)
    if z_ref is not None:
        z_ref[0] = _dot(hb, wz_ref[...]).astype(BF16)
        g_ref[0] = _dot(hb, wg_ref[...]).astype(BF16)


def _inproj_ctx_kernel(x_ref, mod_ref, nw_ref, wqkv_ref, wba_ref, wu_ref,
                       qkv_ref, ba_ref, u_ref, *, d):
    _inproj_kernel(x_ref, mod_ref, nw_ref, wqkv_ref, wba_ref, wu_ref, None, None,
                   qkv_ref, ba_ref, u_ref, None, None, d=d)


def _in_proj(x, mods, mod_row0, norm_w, wqkv, wba, wu, wz=None, wg=None, *, tm):
    b, t, d = x.shape
    full = wz is not None
    tok = lambda n: pl.BlockSpec((1, tm, n), lambda i, j: (i, j, 0))
    in_specs = [tok(d),
                pl.BlockSpec((1, 1, mods.shape[-1]), lambda i, j: (mod_row0(i), 0, 0)),
                _resident((1, d)), _resident(wqkv.shape), _resident(wba.shape),
                _resident(wu.shape)]
    args = [x, mods, norm_w.reshape(1, d), wqkv, wba, wu]
    out_specs = [tok(wqkv.shape[1]), tok(wba.shape[1]), tok(wu.shape[1])]
    out_shape = [jax.ShapeDtypeStruct((b, t, wqkv.shape[1]), BF16),
                 jax.ShapeDtypeStruct((b, t, wba.shape[1]), F32),
                 jax.ShapeDtypeStruct((b, t, wu.shape[1]), BF16)]
    if full:
        in_specs += [_resident(wz.shape), _resident(wg.shape)]
        args += [wz, wg]
        out_specs += [tok(wz.shape[1]), tok(wg.shape[1])]
        out_shape += [jax.ShapeDtypeStruct((b, t, wz.shape[1]), BF16),
                      jax.ShapeDtypeStruct((b, t, wg.shape[1]), BF16)]
    body = functools.partial(_inproj_kernel if full else _inproj_ctx_kernel, d=d)
    return pl.pallas_call(
        body, grid=(b, t // tm), in_specs=in_specs, out_specs=out_specs, out_shape=out_shape,
        compiler_params=_params(("parallel", "parallel")),
        name="in_proj" if full else "in_proj_ctx",
    )(*args)


HALO = 16


def _dprep_kernel(x_ref, xp_ref, xn_ref, ba_ref, cw_ref, alog_ref, dtb_ref,
                  q_ref, k_ref, v_ref, beta_ref, g_ref, *, tm, nt):
    t = pl.program_id(1)
    x = x_ref[0].astype(F32)
    prow = jnp.where(t == 0, 0.0, xp_ref[0, HALO - 1:HALO, :].astype(F32))
    nrow = jnp.where(t == nt - 1, 0.0, xn_ref[0, 0:1, :].astype(F32))
    rows = lax.broadcasted_iota(jnp.int32, x.shape, 0)
    xprev = jnp.where(rows == 0, prow, pltpu.roll(x, 1, 0))
    xnext = jnp.where(rows == tm - 1, nrow, pltpu.roll(x, tm - 1, 0))
    y = _silu(xprev * cw_ref[0:1, :] + x * cw_ref[1:2, :] + xnext * cw_ref[2:3, :])
    for h in range(DN_HEADS):
        sl = slice(h * DN_HEAD_DIM, (h + 1) * DN_HEAD_DIM)
        qh = y[:, h * DN_HEAD_DIM:(h + 1) * DN_HEAD_DIM]
        kh = y[:, DN_WIDTH + h * DN_HEAD_DIM:DN_WIDTH + (h + 1) * DN_HEAD_DIM]
        qn = qh * lax.rsqrt(jnp.sum(qh * qh, axis=-1, keepdims=True) + L2_EPS)
        kn = kh * lax.rsqrt(jnp.sum(kh * kh, axis=-1, keepdims=True) + L2_EPS)
        q_ref[0, :, sl] = (qn * (DN_HEAD_DIM ** -0.5)).astype(BF16)
        k_ref[0, :, sl] = kn.astype(BF16)
    v_ref[0] = y[:, 2 * DN_WIDTH:3 * DN_WIDTH].astype(BF16)
    ba = ba_ref[0]
    beta_ref[0] = jax.nn.sigmoid(ba[:, 0:LANES])
    g_ref[0] = -jnp.exp(alog_ref[...]) * _softplus(ba[:, LANES:2 * LANES] + dtb_ref[...])


def _delta_prep(qkv, ba, conv_w, alog_row, dtb_row, *, tm):
    b, t, c = qkv.shape
    nt = t // tm
    r = tm // HALO
    tok = lambda n: pl.BlockSpec((1, tm, n), lambda i, j: (i, j, 0))
    return pl.pallas_call(
        functools.partial(_dprep_kernel, tm=tm, nt=nt),
        grid=(b, nt),
        in_specs=[tok(c),
                  pl.BlockSpec((1, HALO, c), lambda i, j: (i, jnp.maximum(j * r - 1, 0), 0)),
                  pl.BlockSpec((1, HALO, c), lambda i, j: (i, jnp.minimum((j + 1) * r, t // HALO - 1), 0)),
                  tok(ba.shape[-1]),
                  _resident(conv_w.shape), _resident(alog_row.shape), _resident(dtb_row.shape)],
        out_specs=[tok(DN_WIDTH), tok(DN_WIDTH), tok(DN_WIDTH), tok(LANES), tok(LANES)],
        out_shape=[jax.ShapeDtypeStruct((b, t, DN_WIDTH), BF16)] * 3
        + [jax.ShapeDtypeStruct((b, t, LANES), F32)] * 2,
        compiler_params=_params(("parallel", "parallel")),
        name="delta_prep",
    )(qkv, qkv, qkv, ba, conv_w, alog_row, dtb_row)


def _bmm(a, b):
    return jnp.einsum('nik,nkj->nij', a, b, preferred_element_type=F32)


def _bmm_nt(a, b):
    return jnp.einsum('nik,njk->nij', a, b, preferred_element_type=F32)


def _dwy_kernel(q_ref, k_ref, v_ref, b_ref, g_ref, u_ref, w_ref, qg_ref, a_ref, kdt_ref, eg_ref, *, nc):
    c, dk = DN_CHUNK, DN_HEAD_DIM
    ts = nc * c
    ii = lax.broadcasted_iota(jnp.int32, (c, c), 0)
    jj = lax.broadcasted_iota(jnp.int32, (c, c), 1)
    eye = (ii == jj).astype(F32)
    ti = lax.broadcasted_iota(jnp.int32, (ts, ts), 0)
    tj = lax.broadcasted_iota(jnp.int32, (ts, ts), 1)
    same_chunk = (ti // c) == (tj // c)
    blocks = [(ch, h) for ch in range(nc) for h in range(DN_HEADS)]
    tile = lambda ref, ch, h: ref[0, ch * c:(ch + 1) * c, h * dk:(h + 1) * dk]
    k_l = [tile(k_ref, ch, h) for ch, h in blocks]
    q_l = [tile(q_ref, ch, h) for ch, h in blocks]
    v_l = [tile(v_ref, ch, h) for ch, h in blocks]
    kkqk = _bmm_nt(jnp.stack([jnp.concatenate([k_, q_], axis=0) for k_, q_ in zip(k_l, q_l)]),
                   jnp.stack(k_l))
    beta = b_ref[0]
    g = g_ref[0]
    a_pad = jnp.zeros((c, dk - c), BF16)
    for r in range(N_DIR):
        incl = ii <= jj if r else ii >= jj
        strict = ii < jj if r else ii > jj
        tri = (same_chunk & (ti <= tj if r else ti >= tj)).astype(F32)
        g_cum = _dot_f32(tri, g)
        g_cum_t = g_cum.T
        neg_l, rhs_l = [], []
        for n_, (ch, h) in enumerate(blocks):
            s = r * DN_HEADS + h
            rs = slice(ch * c, (ch + 1) * c)
            ls = slice(s * dk, (s + 1) * dk)
            g_c = g_cum[rs, s:s + 1]
            g_r = g_cum_t[s:s + 1, rs]
            g_end = g_c[0:1] if r else g_c[c - 1:c]
            b_c = beta[rs, s:s + 1]
            decay = jnp.where(incl, jnp.exp(jnp.where(incl, g_c - g_r, 0.0)), 0.0)
            neg_l.append(jnp.where(strict, kkqk[n_, :c] * (-b_c) * decay, 0.0))
            eg = jnp.exp(g_c)
            kf = k_l[n_].astype(F32)
            rhs_l.append(jnp.concatenate([(v_l[n_].astype(F32) * b_c).astype(BF16),
                                          (kf * (b_c * eg)).astype(BF16)], axis=1))
            qg_ref[0, rs, ls] = (q_l[n_].astype(F32) * eg).astype(BF16)
            a_ref[0, rs, ls] = jnp.concatenate([(kkqk[n_, c:] * decay).astype(BF16), a_pad], axis=1)
            kdt_ref[0, ch, s] = (kf * jnp.exp(g_end - g_c)).T.astype(BF16)
            eg_ref[0, ch, s:s + 1, :] = jnp.broadcast_to(jnp.exp(g_end), (1, LANES))
        p = jnp.stack(neg_l)
        pb = p.astype(BF16)
        tinv = eye + p
        m = 2
        while m < c:
            pb = _bmm(pb, pb).astype(BF16)
            tinv = tinv + _bmm(tinv.astype(BF16), pb)
            m *= 2
        sol = _bmm(tinv.astype(BF16), jnp.stack(rhs_l))
        for n_, (ch, h) in enumerate(blocks):
            s = r * DN_HEADS + h
            rs = slice(ch * c, (ch + 1) * c)
            ls = slice(s * dk, (s + 1) * dk)
            u_ref[0, rs, ls] = sol[n_, :, :dk].astype(BF16)
            w_ref[0, rs, ls] = sol[n_, :, dk:].astype(BF16)


def _delta_wy(q, k, v, beta, g, *, ts):
    b, t, w = q.shape
    nc = ts // DN_CHUNK
    ns = N_DIR * DN_HEADS
    tok = lambda n: pl.BlockSpec((1, ts, n), lambda i, j: (i, j, 0))
    return pl.pallas_call(
        functools.partial(_dwy_kernel, nc=nc),
        grid=(b, t // ts),
        in_specs=[tok(w), tok(w), tok(w), tok(LANES), tok(LANES)],
        out_specs=[tok(N_DIR * w)] * 4
        + [pl.BlockSpec((1, nc, ns, DN_HEAD_DIM, DN_CHUNK), lambda i, j: (i, j, 0, 0, 0)),
           pl.BlockSpec((1, nc, ns, LANES), lambda i, j: (i, j, 0, 0))],
        out_shape=[jax.ShapeDtypeStruct((b, t, N_DIR * w), BF16)] * 4
        + [jax.ShapeDtypeStruct((b, t // DN_CHUNK, ns, DN_HEAD_DIM, DN_CHUNK), BF16),
           jax.ShapeDtypeStruct((b, t // DN_CHUNK, ns, LANES), F32)],
        compiler_params=_params(("parallel", "parallel")),
        name="delta_wy",
    )(q, k, v, beta, g)


def _dscan_kernel(uf_ref, wf_ref, qgf_ref, af_ref, kdf_ref, egf_ref,
                  ur_ref, wr_ref, qgr_ref, ar_ref, kdr_ref, egr_ref, s0_ref,
                  of_ref, or_ref, sfin_ref, s_ref, *, nc, nb):
    i = pl.program_id(0)

    @pl.when(i == 0)
    def _():
        s_ref[...] = s0_ref[...]

    c, dk = DN_CHUNK, DN_HEAD_DIM
    dirs = ((uf_ref, wf_ref, qgf_ref, af_ref, kdf_ref, egf_ref, of_ref),
            (ur_ref, wr_ref, qgr_ref, ar_ref, kdr_ref, egr_ref, or_ref))

    def chunk_step(j, carry):
        wq_l, u_l, a_l, kd_l, eg_l, s_l, dst = [], [], [], [], [], [], []
        for r, (u_ref, w_ref, qg_ref, a_ref, kd_ref, eg_ref, o_ref) in enumerate(dirs):
            ch = nc - 1 - j if r else j
            rs = pl.ds(pl.multiple_of(ch * c, c), c)
            for b in range(nb):
                for h in range(DN_HEADS):
                    hs = slice(h * dk, (h + 1) * dk)
                    s = r * DN_HEADS + h
                    wq_l.append(jnp.concatenate([w_ref[b, rs, hs], qg_ref[b, rs, hs]], axis=0))
                    u_l.append(u_ref[b, rs, hs])
                    a_l.append(a_ref[b, rs, hs][:, :c])
                    kd_l.append(kd_ref[b, ch, h])
                    eg_l.append(eg_ref[b, ch, s:s + 1, :])
                    s_l.append(s_ref[b, s])
                    dst.append((o_ref, b, rs, hs, s))
        st = jnp.stack(s_l)
        ws_qs = _bmm(jnp.stack(wq_l), st.astype(BF16))
        v_new = (jnp.stack(u_l).astype(F32) - ws_qs[:, :c]).astype(BF16)
        o = ws_qs[:, c:] + _bmm(jnp.stack(a_l), v_new)
        s_new = st * jnp.stack(eg_l) + _bmm(jnp.stack(kd_l), v_new)
        for n_, (o_ref, b, rs, hs, s) in enumerate(dst):
            o_ref[b, rs, hs] = o[n_].astype(BF16)
            s_ref[b, s] = s_new[n_]
        return carry

    lax.fori_loop(0, nc, chunk_step, 0)

    @pl.when(i == pl.num_programs(0) - 1)
    def _():
        sfin_ref[...] = s_ref[...]


def _delta_scan(u, w, qg, a, kdt, eg, s0, *, ts):
    b, t, w2 = u.shape
    wd = w2 // N_DIR
    n = t // ts
    nc = ts // DN_CHUNK
    fwd = pl.BlockSpec((b, ts, wd), lambda i: (0, i, 0))
    rev = pl.BlockSpec((b, ts, wd), lambda i: (0, n - 1 - i, 1))
    kd_f = pl.BlockSpec((b, nc, DN_HEADS) + kdt.shape[3:], lambda i: (0, i, 0, 0, 0))
    kd_r = pl.BlockSpec((b, nc, DN_HEADS) + kdt.shape[3:], lambda i: (0, n - 1 - i, 1, 0, 0))
    eg_f = pl.BlockSpec((b, nc) + eg.shape[2:], lambda i: (0, i, 0, 0))
    eg_r = pl.BlockSpec((b, nc) + eg.shape[2:], lambda i: (0, n - 1 - i, 0, 0))
    out_f = pl.BlockSpec((b, ts, wd), lambda i: (0, i, 0))
    out_r = pl.BlockSpec((b, ts, wd), lambda i: (0, n - 1 - i, 0))
    return pl.pallas_call(
        functools.partial(_dscan_kernel, nc=nc, nb=b),
        grid=(n,),
        in_specs=[fwd, fwd, fwd, fwd, kd_f, eg_f, rev, rev, rev, rev, kd_r, eg_r, _resident(s0.shape)],
        out_specs=[out_f, out_r, pl.BlockSpec(s0.shape, lambda i: (0, 0, 0, 0))],
        out_shape=[jax.ShapeDtypeStruct((b, t, wd), BF16)] * 2 + [jax.ShapeDtypeStruct(s0.shape, F32)],
        scratch_shapes=[pltpu.VMEM(s0.shape, F32)],
        compiler_params=_params(("arbitrary",)),
        name="delta_scan",
    )(u, w, qg, a, kdt, eg, u, w, qg, a, kdt, eg, s0)


def _cexp(re, im):
    m = jnp.exp(re)
    return m * jnp.cos(im), m * jnp.sin(im)


def _s5_param_kernel(arw_ref, aiw_ref, lsw_ref, brw_ref, biw_ref,
                     arm_ref, aim_ref, lsm_ref, crm_ref, cim_ref,
                     ctr_ref, cti_ref, arr_ref, air_ref, lsr_ref,
                     wkr_ref, wki_ref, kall_ref, k0_ref, mtop_ref, mbot_ref, llr_ref, lli_ref):
    k0 = None
    for r in range(N_DIR):
        are, aim = arw_ref[r, 0], aiw_ref[r, 0]
        dt = jnp.exp(lsw_ref[r, 0])
        kk = (lax.broadcasted_iota(jnp.int32, are.shape, 0) // S5_GROUP).astype(F32)
        lbr, lbi = _cexp(are * dt, aim * dt)
        den = are * are + aim * aim
        nr, ni = lbr - 1.0, lbi
        cr = (nr * are + ni * aim) / den
        ci = (ni * are - nr * aim) / den
        bre, bim = brw_ref[r, 0], biw_ref[r, 0]
        bbr = cr * bre - ci * bim
        bbi = cr * bim + ci * bre
        pr, pi = _cexp(kk * are * dt, kk * aim * dt)
        wr = bbr * pr - bbi * pi
        wi = bbr * pi + bbi * pr
        wkr_ref[r, 0] = wr
        wki_ref[r, 0] = wi
        kall = _dot_f32(wr, ctr_ref[r, 0]) - _dot_f32(wi, cti_ref[r, 0])
        kall_ref[r, 0] = kall
        k0 = kall[0:S5_GROUP] if k0 is None else k0 + kall[0:S5_GROUP]
        are, aim = arm_ref[r, 0], aim_ref[r, 0]
        dt = jnp.exp(lsm_ref[r, 0])
        ee = (lax.broadcasted_iota(jnp.int32, are.shape, 1) // S5_GROUP + 1).astype(F32)
        pr, pi = _cexp(ee * are * dt, ee * aim * dt)
        cre, cim = crm_ref[r, 0], cim_ref[r, 0]
        mtop_ref[r, 0] = cre * pr - cim * pi
        mbot_ref[r, 0] = -(cre * pi + cim * pr)
        are, aim = arr_ref[r, 0], air_ref[r, 0]
        dt = jnp.exp(lsr_ref[r, 0])
        lr, li = _cexp(S5_CHUNK * are * dt, S5_CHUNK * aim * dt)
        llr_ref[r, 0] = lr
        lli_ref[r, 0] = li
    k0_ref[0] = k0


def _s5_operators(a_re, a_im, log_step, b_re, b_im, c_re, c_im):
    g, n, p, l = S5_GROUPS, S5_STATE, S5_GROUP, S5_CHUNK
    lp = l * p
    bc = lambda a, shape: jnp.broadcast_to(a, shape)
    w_shape, m_shape = (N_DIR, g, lp, n), (N_DIR, g, n, lp)
    ls = log_step[:, :, None, None]
    arw, aiw, lsw = bc(a_re[:, :, None, :], w_shape), bc(a_im[:, :, None, :], w_shape), bc(ls, w_shape)
    brw = jnp.tile(jnp.swapaxes(b_re, 2, 3), (1, 1, l, 1))
    biw = jnp.tile(jnp.swapaxes(b_im, 2, 3), (1, 1, l, 1))
    arm, aim, lsm = bc(a_re[:, :, :, None], m_shape), bc(a_im[:, :, :, None], m_shape), bc(ls, m_shape)
    ctr, cti = jnp.swapaxes(c_re, 2, 3), jnp.swapaxes(c_im, 2, 3)
    crm, cim = jnp.tile(ctr, (1, 1, 1, l)), jnp.tile(cti, (1, 1, 1, l))
    arr, air, lsr = a_re[:, :, None, :], a_im[:, :, None, :], bc(ls, (N_DIR, g, 1, n))
    blk = lambda r, c: pl.BlockSpec((N_DIR, 1, r, c), lambda i: (0, i, 0, 0))
    sds = lambda r, c: jax.ShapeDtypeStruct((N_DIR, g, r, c), F32)
    wkr, wki, kall, k0, mtop, mbot, llr, lli = pl.pallas_call(
        _s5_param_kernel,
        grid=(g,),
        in_specs=[blk(lp, n)] * 5 + [blk(n, lp)] * 5 + [blk(n, p)] * 2 + [blk(1, n)] * 3,
        out_specs=[blk(lp, n), blk(lp, n), blk(lp, p), pl.BlockSpec((1, p, p), lambda i: (i, 0, 0)),
                   blk(n, lp), blk(n, lp), blk(1, n), blk(1, n)],
        out_shape=[sds(lp, n), sds(lp, n), sds(lp, p), jax.ShapeDtypeStruct((g, p, p), F32),
                   sds(n, lp), sds(n, lp), sds(1, n), sds(1, n)],
        compiler_params=_params(("parallel",)),
        name="s5_params",
    )(arw, aiw, lsw, brw, biw, arm, aim, lsm, crm, cim, ctr, cti, arr, air, lsr)

    def flip_blocks(a, axis):
        shp = a.shape
        a = a.reshape(shp[:axis] + (l, p) + shp[axis + 1:])
        return jnp.flip(a, axis=axis).reshape(shp)

    def pair_diag(a):
        gg, r, c = a.shape
        a = a.reshape(gg // 2, 2, r, c)
        z = jnp.zeros_like(a[:, 0])
        return jnp.concatenate([jnp.concatenate([a[:, 0], z], axis=2),
                                jnp.concatenate([z, a[:, 1]], axis=2)], axis=1)

    wz = [pair_diag(flip_blocks(w[0], 1)) for w in (wkr, wki)] + [pair_diag(w[1]) for w in (wkr, wki)]
    w_in = jnp.concatenate(wz, axis=2).astype(BF16)
    m_f = [pair_diag(m[0]) for m in (mtop, mbot)]
    m_r = [pair_diag(flip_blocks(m[1], 2)) for m in (mtop, mbot)]
    kf = kall[0].reshape(g, l, p, p)
    kr = kall[1].reshape(g, l, p, p)
    d = jnp.arange(l)[None, :] - jnp.arange(l)[:, None]
    tf = kf[:, jnp.clip(d, 0, l - 1)]
    tr = kr[:, jnp.clip(-d, 0, l - 1)]
    t0 = jnp.broadcast_to(k0[:, None, None], tf.shape)
    dd = d[None, :, :, None, None]
    toep = jnp.where(dd > 0, tf, jnp.where(dd < 0, tr, t0))
    toep = jnp.transpose(toep, (0, 1, 3, 2, 4)).reshape(g, lp, lp)
    w_out = jnp.concatenate([pair_diag(toep)] + m_f + m_r, axis=1).astype(BF16)
    lam = jnp.stack([llr.reshape(N_DIR, g * n), lli.reshape(N_DIR, g * n)])
    return w_in, w_out, lam


def _s5_in_kernel(a_ref, w_ref, zfr_ref, zfi_ref, zrr_ref, zri_ref):
    a2 = jnp.concatenate([a_ref[0], a_ref[1]], axis=1)
    z = _dot(a2, w_ref[0])
    for k, ref in enumerate((zfr_ref, zfi_ref, zrr_ref, zri_ref)):
        ref[...] = z[:, k * LANES:(k + 1) * LANES]


def _s5_chunk_in(a, w_in, *, tr):
    g, rows, lp = a.shape
    out = pl.BlockSpec((tr, LANES), lambda i, j: (j, i))
    return pl.pallas_call(
        _s5_in_kernel,
        grid=(g // 2, rows // tr),
        in_specs=[pl.BlockSpec((2, tr, lp), lambda i, j: (i, j, 0)),
                  pl.BlockSpec((1,) + w_in.shape[1:], lambda i, j: (i, 0, 0))],
        out_specs=[out] * 4,
        out_shape=[jax.ShapeDtypeStruct((rows, g // 2 * LANES), F32)] * 4,
        compiler_params=_params(("parallel", "parallel")),
        name="s5_chunk_in",
    )(a, w_in)


def _s5_scan_kernel(zfr_ref, zfi_ref, zrr_ref, zri_ref, lam_ref, x0fr_ref, x0fi_ref, x0rr_ref, x0ri_ref,
                    xfr_ref, xfi_ref, xrr_ref, xri_ref, efr_ref, efi_ref, err_ref, eri_ref, *, npairs):
    half = lax.broadcasted_iota(jnp.int32, x0fr_ref.shape, 0) < 4
    lfr, lrr = lam_ref[0, 0:1, :], lam_ref[0, 1:2, :]
    lfi, lri = lam_ref[1, 0:1, :], lam_ref[1, 1:2, :]

    def advance(lr, li, xr, xi, zr, zi):
        return lr * xr - li * xi + zr, lr * xi + li * xr + zi

    def step(k, carry):
        cfr, cfi, crr, cri = carry
        zr, zi = zfr_ref[k], zfi_ref[k]
        tr_, ti_ = advance(lfr, lfi, cfr, cfi, zr, zi)
        pr = jnp.where(half, cfr, pltpu.roll(tr_, 4, 0))
        pi = jnp.where(half, cfi, pltpu.roll(ti_, 4, 0))
        xfr_ref[k], xfi_ref[k] = pr, pi
        tr_, ti_ = advance(lfr, lfi, pr, pi, zr, zi)
        cfr, cfi = pltpu.roll(tr_, 4, 0), pltpu.roll(ti_, 4, 0)
        kk = npairs - 1 - k
        zr, zi = zrr_ref[kk], zri_ref[kk]
        tr_, ti_ = advance(lrr, lri, crr, cri, zr, zi)
        pr = jnp.where(half, pltpu.roll(tr_, 4, 0), crr)
        pi = jnp.where(half, pltpu.roll(ti_, 4, 0), cri)
        xrr_ref[kk], xri_ref[kk] = pr, pi
        tr_, ti_ = advance(lrr, lri, pr, pi, zr, zi)
        crr, cri = pltpu.roll(tr_, 4, 0), pltpu.roll(ti_, 4, 0)
        return cfr, cfi, crr, cri

    ends = lax.fori_loop(0, npairs, step,
                         (x0fr_ref[...], x0fi_ref[...], x0rr_ref[...], x0ri_ref[...]))
    for ref, val in zip((efr_ref, efi_ref, err_ref, eri_ref), ends):
        ref[...] = val


def _s5_state_scan(z4, lam, x0, *, tl):
    rows, w = z4[0].shape
    npairs = rows // 8
    z4 = [z.reshape(npairs, 8, w) for z in z4]
    zb = pl.BlockSpec((npairs, 8, tl), lambda j: (0, 0, j))
    cb = pl.BlockSpec((8, tl), lambda j: (0, j))
    outs = pl.pallas_call(
        functools.partial(_s5_scan_kernel, npairs=npairs),
        grid=(w // tl,),
        in_specs=[zb] * 4 + [pl.BlockSpec((2, N_DIR, tl), lambda j: (0, 0, j))] + [cb] * 4,
        out_specs=[zb] * 4 + [cb] * 4,
        out_shape=[jax.ShapeDtypeStruct((npairs, 8, w), F32)] * 4
        + [jax.ShapeDtypeStruct((8, w), F32)] * 4,
        compiler_params=_params(("parallel",)),
        name="s5_state_scan",
    )(*z4, lam, *x0)
    return [x.reshape(rows, w) for x in outs[:4]], list(outs[4:])


def _s5_out_kernel(a_ref, xfr_ref, xfi_ref, xrr_ref, xri_ref, w_ref, y_ref):
    lhs = jnp.concatenate([a_ref[0], a_ref[1]] + [r[...].astype(BF16) for r in
                                                  (xfr_ref, xfi_ref, xrr_ref, xri_ref)], axis=1)
    y = _dot(lhs, w_ref[0])
    lp = y_ref.shape[-1]
    y_ref[0] = y[:, :lp].astype(BF16)
    y_ref[1] = y[:, lp:].astype(BF16)


def _s5_chunk_out(a, x4, w_out, *, tr):
    g, rows, lp = a.shape
    xb = pl.BlockSpec((tr, LANES), lambda i, j: (j, i))
    ab = pl.BlockSpec((2, tr, lp), lambda i, j: (i, j, 0))
    return pl.pallas_call(
        _s5_out_kernel,
        grid=(g // 2, rows // tr),
        in_specs=[ab] + [xb] * 4 + [pl.BlockSpec((1,) + w_out.shape[1:], lambda i, j: (i, 0, 0))],
        out_specs=ab,
        out_shape=jax.ShapeDtypeStruct(a.shape, BF16),
        compiler_params=_params(("parallel", "parallel")),
        name="s5_chunk_out",
    )(a, *x4, w_out)


def _to_chunks(u):
    b, t, _ = u.shape
    a = u.reshape(b, t // S5_CHUNK, S5_CHUNK, S5_GROUPS, S5_GROUP)
    a = jnp.transpose(a, (3, 1, 0, 2, 4))
    return a.reshape(S5_GROUPS, (t // S5_CHUNK) * b, S5_CHUNK * S5_GROUP)


def _from_chunks(y, b):
    g, rows, _ = y.shape
    nc = rows // b
    a = y.reshape(g, nc, b, S5_CHUNK, S5_GROUP)
    a = jnp.transpose(a, (2, 1, 3, 0, 4))
    return a.reshape(b, nc * S5_CHUNK, g * S5_GROUP)


def _gelu_tanh(x):
    return 0.5 * x * (1.0 + jnp.tanh(0.7978845608028654 * (x + 0.044715 * x * x * x)))


def _merge_kernel(of_ref, or_ref, z_ref, ys_ref, u_ref, g_ref, x_ref, mod_ref,
                  dnw_ref, wa_ref, dsk_ref, wglu_ref, bglu_ref, wb_ref, wo_ref, n2w_ref,
                  xl_ref, h2_ref, *, d):
    o = of_ref[0].astype(F32) + or_ref[0].astype(F32)
    z = z_ref[0].astype(F32)
    heads = []
    for h in range(DN_HEADS):
        sl = slice(h * DN_HEAD_DIM, (h + 1) * DN_HEAD_DIM)
        oh = o[:, sl]
        on = oh * lax.rsqrt(jnp.mean(oh * oh, axis=-1, keepdims=True) + RMS_EPS) * dnw_ref[...]
        heads.append((on * _silu(z[:, sl])).astype(BF16))
    ya = _dot(jnp.concatenate(heads, axis=1), wa_ref[...])
    ys = ys_ref[0].astype(F32) + dsk_ref[...] * u_ref[0].astype(F32)
    zz = _dot(_gelu_tanh(ys).astype(BF16), wglu_ref[...]) + bglu_ref[...]
    yb = _dot((zz[:, :S5_WIDTH] * jax.nn.sigmoid(zz[:, S5_WIDTH:])).astype(BF16), wb_ref[...])
    gates = g_ref[0].astype(F32)
    mix = jax.nn.sigmoid(gates[:, :d]) * ya + jax.nn.sigmoid(gates[:, d:]) * yb
    xl = x_ref[0] + mod_ref[0, :, 2 * d:3 * d] * _dot(mix.astype(BF16), wo_ref[...])
    xl_ref[0] = xl
    hn = xl * lax.rsqrt(jnp.mean(xl * xl, axis=-1, keepdims=True) + RMS_EPS) * n2w_ref[...]
    h2_ref[0] = (hn * (1.0 + mod_ref[0, :, 4 * d:5 * d]) + mod_ref[0, :, 3 * d:4 * d]).astype(BF16)


def _mix_merge(o_f, o_r, z, ys, u, gates, x, mods, dn_norm_w, w_a_out, s5_d, w_glu, b_glu,
               w_b_out, w_o, norm2_w, *, tm):
    b, t, d = x.shape
    tok = lambda n: pl.BlockSpec((1, tm, n), lambda i, j: (i, j, 0))
    consts = [dn_norm_w.reshape(1, -1), w_a_out, s5_d.reshape(1, -1), w_glu, b_glu.reshape(1, -1),
              w_b_out, w_o, norm2_w.reshape(1, -1)]
    return pl.pallas_call(
        functools.partial(_merge_kernel, d=d),
        grid=(b, t // tm),
        in_specs=[tok(o_f.shape[-1]), tok(o_r.shape[-1]), tok(z.shape[-1]), tok(ys.shape[-1]),
                  tok(u.shape[-1]), tok(gates.shape[-1]), tok(d),
                  pl.BlockSpec((1, 1, mods.shape[-1]), lambda i, j: (i, 0, 0))]
        + [_resident(c.shape) for c in consts],
        out_specs=[tok(d), tok(d)],
        out_shape=[jax.ShapeDtypeStruct((b, t, d), F32), jax.ShapeDtypeStruct((b, t, d), BF16)],
        compiler_params=_params(("parallel", "parallel")),
        name="mix_merge",
    )(o_f, o_r, z, ys, u, gates, x, mods, *consts)


FFN_ROWS = 8
FFN_CB = 256


def _dwconv3x3(e, cw, n_out):
    n = e.shape[0]
    col = lax.broadcasted_iota(jnp.int32, e.shape, 0) % GRID_W
    left = jnp.where(col == 0, 0.0, pltpu.roll(e, 1, 0))
    right = jnp.where(col == GRID_W - 1, 0.0, pltpu.roll(e, n - 1, 0))
    out = None
    for i in range(3):
        rs = slice(i * GRID_W, i * GRID_W + n_out)
        term = (left[rs] * cw[3 * i:3 * i + 1, :] + e[rs] * cw[3 * i + 1:3 * i + 2, :]
                + right[rs] * cw[3 * i + 2:3 * i + 3, :])
        out = term if out is None else out + term
    return out


def _ffn_kernel(h_ref, hp_ref, hn_ref, xl_ref, mod_ref, wg_ref, wv_ref, cg_ref, cv_ref, wd_ref, nfw_ref,
                o_ref, acc_ref, *, d, nt, ncb):
    t = pl.program_id(1)
    top = jnp.where(t == 0, jnp.zeros_like(hp_ref[0]), hp_ref[0])
    bot = jnp.where(t == nt - 1, jnp.zeros_like(hn_ref[0]), hn_ref[0])
    hext = jnp.concatenate([top, h_ref[0], bot], axis=0)
    n_out = h_ref.shape[1]
    acc_ref[...] = jnp.zeros_like(acc_ref)

    def cblock(k, carry):
        gate = _dwconv3x3(_dot(hext, wg_ref[k]), cg_ref[k], n_out)
        val = _dwconv3x3(_dot(hext, wv_ref[k]), cv_ref[k], n_out)
        acc_ref[...] += _dot((_silu(gate) * val).astype(BF16), wd_ref[k])
        return carry

    lax.fori_loop(0, ncb, cblock, 0)
    xo = xl_ref[0] + mod_ref[0, :, 5 * d:6 * d] * acc_ref[...]
    o_ref[0] = xo * lax.rsqrt(jnp.mean(xo * xo, axis=-1, keepdims=True) + RMS_EPS) * nfw_ref[...]


def _conv_ffn(h2, xl, mods, w_up, conv_w, w_down, norm_f_w):
    b, t, d = xl.shape
    dff = w_down.shape[0]
    ncb = dff // FFN_CB
    tm = FFN_ROWS * GRID_W
    nt = t // tm
    nrow = t // GRID_W
    split = lambda w: jnp.transpose(w.reshape(w.shape[0], ncb, FFN_CB), (1, 0, 2))
    wg, wv = split(w_up[:, :dff].astype(BF16)), split(w_up[:, dff:].astype(BF16))
    cw = conv_w.reshape(9, 2 * dff)
    cg, cv = split(cw[:, :dff]), split(cw[:, dff:])
    wd = w_down.astype(BF16).reshape(ncb, FFN_CB, d)
    tok = lambda: pl.BlockSpec((1, tm, d), lambda i, j: (i, j, 0))
    return pl.pallas_call(
        functools.partial(_ffn_kernel, d=d, nt=nt, ncb=ncb),
        grid=(b, nt),
        in_specs=[tok(),
                  pl.BlockSpec((1, GRID_W, d), lambda i, j: (i, jnp.maximum(j * FFN_ROWS - 1, 0), 0)),
                  pl.BlockSpec((1, GRID_W, d), lambda i, j: (i, jnp.minimum((j + 1) * FFN_ROWS, nrow - 1), 0)),
                  tok(),
                  pl.BlockSpec((1, 1, mods.shape[-1]), lambda i, j: (i, 0, 0)),
                  _resident(wg.shape), _resident(wv.shape), _resident(cg.shape), _resident(cv.shape),
                  _resident(wd.shape), _resident((1, d))],
        out_specs=tok(),
        out_shape=jax.ShapeDtypeStruct((b, t, d), F32),
        scratch_shapes=[pltpu.VMEM((tm, d), F32)],
        compiler_params=_params(("parallel", "parallel")),
        name="conv_ffn",
    )(h2, h2, h2, xl, mods, wg, wv, cg, cv, wd, norm_f_w.reshape(1, d))


def _lane_row(a):
    a = a.reshape(-1).astype(F32)
    return jnp.pad(a, (0, LANES - a.shape[0])).reshape(1, LANES)


def _pad_cols(w):
    return jnp.pad(w, ((0, 0), (0, LANES - w.shape[1])))


def kernel(x, c, ctx, c_ctx, w_ada, b_ada, norm1_w, w_in, dn_conv_w, dn_a_log, dn_dt_bias, dn_norm_w,
           w_a_out, s5_a_re, s5_a_im, s5_log_step, s5_b_re, s5_b_im, s5_c_re, s5_c_im, s5_d, w_glu,
           b_glu, w_b_out, w_o, norm2_w, w_up, ffn_conv_w, w_down, norm_f_w):
    assert w_ada.shape[0] == 1, "single-layer block"
    b, t, d = x.shape
    tc = ctx.shape[1]
    nh = N_DIR * DN_HEADS

    c_rows = jnp.zeros((8, d), F32).at[:b].set(c).at[b].set(c_ctx)
    mods = _modulation(c_rows, w_ada[0], b_ada[0]).reshape(8, 1, N_MOD * d)

    w = w_in[0]
    o_z, o_b, o_a = 3 * DN_WIDTH, 4 * DN_WIDTH, 4 * DN_WIDTH + nh
    o_u = o_a + nh
    o_g = o_u + S5_WIDTH
    wqkv = w[:, :o_z].astype(BF16)
    wz = w[:, o_z:o_b].astype(BF16)
    wba = jnp.concatenate([_pad_cols(w[:, o_b:o_a]), _pad_cols(w[:, o_a:o_u])], axis=1).astype(BF16)
    wu = w[:, o_u:o_g].astype(BF16)
    wg = w[:, o_g:].astype(BF16)

    qkv_l, ba_l, u_l, z_l, gates_l = _in_proj(x, mods, lambda i: i, norm1_w[0], wqkv, wba, wu, wz, wg, tm=512)
    qkv_c, ba_c, u_c = _in_proj(ctx, mods, lambda i: b, norm1_w[0], wqkv, wba, wu, tm=tc)

    alog_row, dtb_row = _lane_row(dn_a_log[0]), _lane_row(dn_dt_bias[0])
    prep_c = _delta_prep(qkv_c, ba_c, dn_conv_w[0], alog_row, dtb_row, tm=tc)
    prep_l = _delta_prep(qkv_l, ba_l, dn_conv_w[0], alog_row, dtb_row, tm=512)
    s0 = jnp.zeros((b, nh, DN_HEAD_DIM, DN_HEAD_DIM), F32)
    _, _, s_ctx = _delta_scan(*_delta_wy(*prep_c, ts=tc), s0, ts=128)
    o_f, o_r, _ = _delta_scan(*_delta_wy(*prep_l, ts=256), s_ctx, ts=128)

    w_s5in, w_s5out, lam = _s5_operators(s5_a_re[0], s5_a_im[0], s5_log_step[0], s5_b_re[0], s5_b_im[0],
                                         s5_c_re[0], s5_c_im[0])
    a_c, a_l = _to_chunks(u_c), _to_chunks(u_l)
    zero = jnp.zeros((8, S5_GROUPS * S5_STATE), F32)
    zs_c = _s5_chunk_in(a_c, w_s5in, tr=a_c.shape[1])
    _, x_ctx = _s5_state_scan(zs_c, lam, [zero] * 4, tl=512)
    zs_l = _s5_chunk_in(a_l, w_s5in, tr=512)
    x_l, _ = _s5_state_scan(zs_l, lam, x_ctx, tl=512)
    y_s5 = _from_chunks(_s5_chunk_out(a_l, x_l, w_s5out, tr=512), b)

    xl, h2 = _mix_merge(o_f, o_r, z_l, y_s5, u_l, gates_l, x, mods, dn_norm_w[0], w_a_out[0].astype(BF16),
                        s5_d[0], w_glu[0].astype(BF16), b_glu[0], w_b_out[0].astype(BF16),
                        w_o[0].astype(BF16), norm2_w[0], tm=512)
    return _conv_ffn(h2, xl, mods, w_up[0], ffn_conv_w[0], w_down[0], norm_f_w)
```

```python
import functools

import jax
import jax.numpy as jnp
from jax import lax
from jax.experimental import pallas as pl
from jax.experimental.pallas import tpu as pltpu

F32 = jnp.float32
BF16 = jnp.bfloat16

GRID_W = 64
N_DIR = 2
DN_HEADS = 4
DN_HEAD_DIM = 128
DN_WIDTH = DN_HEADS * DN_HEAD_DIM
DN_CHUNK = 64
S5_WIDTH = 512
S5_GROUP = 16
S5_GROUPS = S5_WIDTH // S5_GROUP
S5_STATE = 64
S5_CHUNK = 16
N_MOD = 6
RMS_EPS = 1e-6
L2_EPS = 1e-6
LANES = 128
S5_BLK = LANES // S5_GROUP
S5_NBLK = S5_GROUPS // S5_BLK
VMEM_LIMIT = 56 * 1024 * 1024


def _dot(a, b):
    return jnp.dot(a, b, preferred_element_type=F32)


def _dot_f32(a, b):
    return jnp.dot(a, b, preferred_element_type=F32, precision=lax.Precision.HIGHEST)


def _dot_nt_f32(a, b):
    return lax.dot_general(a, b, (((1,), (1,)), ((), ())), preferred_element_type=F32,
                           precision=lax.Precision.HIGHEST)


def _silu(x):
    return x * jax.nn.sigmoid(x)


def _softplus(x):
    return jnp.maximum(x, 0.0) + jnp.log(1.0 + jnp.exp(-jnp.abs(x)))


def _params(sem, vmem=VMEM_LIMIT):
    return pltpu.CompilerParams(dimension_semantics=sem, vmem_limit_bytes=vmem)


def _resident(shape):
    nd = len(shape)
    return pl.BlockSpec(shape, lambda *_: (0,) * nd, pipeline_mode=pl.Buffered(1))


def _mod_kernel(c_ref, w_ref, b_ref, o_ref):
    sc = _silu(c_ref[...])
    o_ref[...] = _dot(sc.astype(BF16), w_ref[...].astype(BF16)) + b_ref[...]


def _modulation(c_rows, w_ada, b_ada):
    d, n = w_ada.shape
    tn = n // 4
    return pl.pallas_call(
        _mod_kernel,
        grid=(n // tn,),
        in_specs=[pl.BlockSpec(c_rows.shape, lambda j: (0, 0)),
                  pl.BlockSpec((d, tn), lambda j: (0, j)),
                  pl.BlockSpec((1, tn), lambda j: (0, j))],
        out_specs=pl.BlockSpec((c_rows.shape[0], tn), lambda j: (0, j)),
        out_shape=jax.ShapeDtypeStruct((c_rows.shape[0], n), F32),
        compiler_params=_params(("arbitrary",)),
        name="adaln_mod",
    )(c_rows, w_ada, b_ada.reshape(1, n))


def _inproj_kernel(x_ref, mod_ref, nw_ref, wqkv_ref, wba_ref, wu_ref, wz_ref, wg_ref,
                   qkv_ref, ba_ref, u_ref, z_ref, g_ref, *, d):
    x = x_ref[0]
    ms = jnp.mean(x * x, axis=-1, keepdims=True)
    h = x * lax.rsqrt(ms + RMS_EPS) * nw_ref[...]
    shift = mod_ref[0, :, 0:d]
    scale = mod_ref[0, :, d:2 * d]
    hb = (h * (1.0 + scale) + shift).astype(BF16)
    qkv_ref[0] = _dot(hb, wqkv_ref[...]).astype(BF16)
    ba_ref[0] = _dot(hb, wba_ref[...])
    u_ref[0] = _dot(hb, wu_ref[...])
    if z_ref is not None:
        z_ref[0] = _dot(hb, wz_ref[...]).astype(BF16)
        g_ref[0] = _dot(hb, wg_ref[...]).astype(BF16)


def _inproj_ctx_kernel(x_ref, mod_ref, nw_ref, wqkv_ref, wba_ref, wu_ref,
                       qkv_ref, ba_ref, u_ref, *, d):
    _inproj_kernel(x_ref, mod_ref, nw_ref, wqkv_ref, wba_ref, wu_ref, None, None,
                   qkv_ref, ba_ref, u_ref, None, None, d=d)


def _in_proj(x, mods, mod_row0, norm_w, wqkv, wba, wu, wz=None, wg=None, *, tm):
    b, t, d = x.shape
    full = wz is not None
    tok = lambda n: pl.BlockSpec((1, tm, n), lambda i, j: (i, j, 0))
    in_specs = [tok(d),
                pl.BlockSpec((1, 1, mods.shape[-1]), lambda i, j: (mod_row0(i), 0, 0)),
                _resident((1, d)), _resident(wqkv.shape), _resident(wba.shape),
                _resident(wu.shape)]
    args = [x, mods, norm_w.reshape(1, d), wqkv, wba, wu]
    out_specs = [tok(wqkv.shape[1]), tok(wba.shape[1]), tok(wu.shape[1])]
    out_shape = [jax.ShapeDtypeStruct((b, t, wqkv.shape[1]), BF16),
                 jax.ShapeDtypeStruct((b, t, wba.shape[1]), F32),
                 jax.ShapeDtypeStruct((b, t, wu.shape[1]), F32)]
    if full:
        in_specs += [_resident(wz.shape), _resident(wg.shape)]
        args += [wz, wg]
        out_specs += [tok(wz.shape[1]), tok(wg.shape[1])]
        out_shape += [jax.ShapeDtypeStruct((b, t, wz.shape[1]), BF16),
                      jax.ShapeDtypeStruct((b, t, wg.shape[1]), BF16)]
    body = functools.partial(_inproj_kernel if full else _inproj_ctx_kernel, d=d)
    return pl.pallas_call(
        body, grid=(b, t // tm), in_specs=in_specs, out_specs=out_specs, out_shape=out_shape,
        compiler_params=_params(("parallel", "parallel")),
        name="in_proj" if full else "in_proj_ctx",
    )(*args)


HALO = 16


def _dprep_kernel(x_ref, xp_ref, xn_ref, ba_ref, cw_ref, alog_ref, dtb_ref,
                  q_ref, k_ref, v_ref, beta_ref, g_ref, *, tm, nt):
    t = pl.program_id(1)
    x = x_ref[0].astype(F32)
    prow = jnp.where(t == 0, 0.0, xp_ref[0, HALO - 1:HALO, :].astype(F32))
    nrow = jnp.where(t == nt - 1, 0.0, xn_ref[0, 0:1, :].astype(F32))
    rows = lax.broadcasted_iota(jnp.int32, x.shape, 0)
    xprev = jnp.where(rows == 0, prow, pltpu.roll(x, 1, 0))
    xnext = jnp.where(rows == tm - 1, nrow, pltpu.roll(x, tm - 1, 0))
    y = _silu(xprev * cw_ref[0:1, :] + x * cw_ref[1:2, :] + xnext * cw_ref[2:3, :])
    for h in range(DN_HEADS):
        sl = slice(h * DN_HEAD_DIM, (h + 1) * DN_HEAD_DIM)
        qh = y[:, h * DN_HEAD_DIM:(h + 1) * DN_HEAD_DIM]
        kh = y[:, DN_WIDTH + h * DN_HEAD_DIM:DN_WIDTH + (h + 1) * DN_HEAD_DIM]
        qn = qh * lax.rsqrt(jnp.sum(qh * qh, axis=-1, keepdims=True) + L2_EPS)
        kn = kh * lax.rsqrt(jnp.sum(kh * kh, axis=-1, keepdims=True) + L2_EPS)
        q_ref[0, :, sl] = (qn * (DN_HEAD_DIM ** -0.5)).astype(BF16)
        k_ref[0, :, sl] = kn.astype(BF16)
    v_ref[0] = y[:, 2 * DN_WIDTH:3 * DN_WIDTH].astype(BF16)
    ba = ba_ref[0]
    beta_ref[0] = jax.nn.sigmoid(ba[:, 0:LANES])
    g_ref[0] = -jnp.exp(alog_ref[...]) * _softplus(ba[:, LANES:2 * LANES] + dtb_ref[...])


def _delta_prep(qkv, ba, conv_w, alog_row, dtb_row, *, tm):
    b, t, c = qkv.shape
    nt = t // tm
    r = tm // HALO
    tok = lambda n: pl.BlockSpec((1, tm, n), lambda i, j: (i, j, 0))
    return pl.pallas_call(
        functools.partial(_dprep_kernel, tm=tm, nt=nt),
        grid=(b, nt),
        in_specs=[tok(c),
                  pl.BlockSpec((1, HALO, c), lambda i, j: (i, jnp.maximum(j * r - 1, 0), 0)),
                  pl.BlockSpec((1, HALO, c), lambda i, j: (i, jnp.minimum((j + 1) * r, t // HALO - 1), 0)),
                  tok(ba.shape[-1]),
                  _resident(conv_w.shape), _resident(alog_row.shape), _resident(dtb_row.shape)],
        out_specs=[tok(DN_WIDTH), tok(DN_WIDTH), tok(DN_WIDTH), tok(LANES), tok(LANES)],
        out_shape=[jax.ShapeDtypeStruct((b, t, DN_WIDTH), BF16)] * 3
        + [jax.ShapeDtypeStruct((b, t, LANES), F32)] * 2,
        compiler_params=_params(("parallel", "parallel")),
        name="delta_prep",
    )(qkv, qkv, qkv, ba, conv_w, alog_row, dtb_row)


def _bmm(a, b):
    return jnp.einsum('nik,nkj->nij', a, b, preferred_element_type=F32)


def _bmm_nt(a, b):
    return jnp.einsum('nik,njk->nij', a, b, preferred_element_type=F32)


def _dwy_kernel(q_ref, k_ref, v_ref, b_ref, g_ref, u_ref, w_ref, qg_ref, a_ref, kdt_ref, eg_ref, *, nc):
    c, dk = DN_CHUNK, DN_HEAD_DIM
    ts = nc * c
    ii = lax.broadcasted_iota(jnp.int32, (c, c), 0)
    jj = lax.broadcasted_iota(jnp.int32, (c, c), 1)
    eye = (ii == jj).astype(F32)
    ti = lax.broadcasted_iota(jnp.int32, (ts, ts), 0)
    tj = lax.broadcasted_iota(jnp.int32, (ts, ts), 1)
    same_chunk = (ti // c) == (tj // c)
    blocks = [(ch, h) for ch in range(nc) for h in range(DN_HEADS)]
    tile = lambda ref, ch, h: ref[0, ch * c:(ch + 1) * c, h * dk:(h + 1) * dk]
    k_l = [tile(k_ref, ch, h) for ch, h in blocks]
    q_l = [tile(q_ref, ch, h) for ch, h in blocks]
    v_l = [tile(v_ref, ch, h) for ch, h in blocks]
    kkqk = _bmm_nt(jnp.stack([jnp.concatenate([k_, q_], axis=0) for k_, q_ in zip(k_l, q_l)]),
                   jnp.stack(k_l))
    beta = b_ref[0]
    g = g_ref[0]
    a_pad = jnp.zeros((c, dk - c), BF16)
    for r in range(N_DIR):
        incl = ii <= jj if r else ii >= jj
        strict = ii < jj if r else ii > jj
        tri = (same_chunk & (ti <= tj if r else ti >= tj)).astype(F32)
        g_cum = _dot_f32(tri, g)
        g_cum_t = g_cum.T
        neg_l, rhs_l = [], []
        for n_, (ch, h) in enumerate(blocks):
            s = r * DN_HEADS + h
            rs = slice(ch * c, (ch + 1) * c)
            ls = slice(s * dk, (s + 1) * dk)
            g_c = g_cum[rs, s:s + 1]
            g_r = g_cum_t[s:s + 1, rs]
            g_end = g_c[0:1] if r else g_c[c - 1:c]
            b_c = beta[rs, s:s + 1]
            decay = jnp.where(incl, jnp.exp(jnp.where(incl, g_c - g_r, 0.0)), 0.0)
            neg_l.append(jnp.where(strict, kkqk[n_, :c] * (-b_c) * decay, 0.0))
            eg = jnp.exp(g_c)
            kf = k_l[n_].astype(F32)
            rhs_l.append(jnp.concatenate([(v_l[n_].astype(F32) * b_c).astype(BF16),
                                          (kf * (b_c * eg)).astype(BF16)], axis=1))
            qg_ref[0, rs, ls] = (q_l[n_].astype(F32) * eg).astype(BF16)
            a_ref[0, rs, ls] = jnp.concatenate([(kkqk[n_, c:] * decay).astype(BF16), a_pad], axis=1)
            kdt_ref[0, ch, s] = (kf * jnp.exp(g_end - g_c)).T.astype(BF16)
            eg_ref[0, ch, s:s + 1, :] = jnp.broadcast_to(jnp.exp(g_end), (1, LANES))
        p = jnp.stack(neg_l)
        pb = p.astype(BF16)
        tinv = eye + p
        m = 2
        while m < c:
            pb = _bmm(pb, pb).astype(BF16)
            tinv = tinv + _bmm(tinv.astype(BF16), pb)
            m *= 2
        sol = _bmm(tinv.astype(BF16), jnp.stack(rhs_l))
        for n_, (ch, h) in enumerate(blocks):
            s = r * DN_HEADS + h
            rs = slice(ch * c, (ch + 1) * c)
            ls = slice(s * dk, (s + 1) * dk)
            u_ref[0, rs, ls] = sol[n_, :, :dk].astype(BF16)
            w_ref[0, rs, ls] = sol[n_, :, dk:].astype(BF16)


def _delta_wy(q, k, v, beta, g, *, ts):
    b, t, w = q.shape
    nc = ts // DN_CHUNK
    ns = N_DIR * DN_HEADS
    tok = lambda n: pl.BlockSpec((1, ts, n), lambda i, j: (i, j, 0))
    return pl.pallas_call(
        functools.partial(_dwy_kernel, nc=nc),
        grid=(b, t // ts),
        in_specs=[tok(w), tok(w), tok(w), tok(LANES), tok(LANES)],
        out_specs=[tok(N_DIR * w)] * 4
        + [pl.BlockSpec((1, nc, ns, DN_HEAD_DIM, DN_CHUNK), lambda i, j: (i, j, 0, 0, 0)),
           pl.BlockSpec((1, nc, ns, LANES), lambda i, j: (i, j, 0, 0))],
        out_shape=[jax.ShapeDtypeStruct((b, t, N_DIR * w), BF16)] * 4
        + [jax.ShapeDtypeStruct((b, t // DN_CHUNK, ns, DN_HEAD_DIM, DN_CHUNK), BF16),
           jax.ShapeDtypeStruct((b, t // DN_CHUNK, ns, LANES), F32)],
        compiler_params=_params(("parallel", "parallel")),
        name="delta_wy",
    )(q, k, v, beta, g)


def _dscan_kernel(uf_ref, wf_ref, qgf_ref, af_ref, kdf_ref, egf_ref,
                  ur_ref, wr_ref, qgr_ref, ar_ref, kdr_ref, egr_ref, s0_ref,
                  of_ref, or_ref, sfin_ref, s_ref, *, nc, nb):
    i = pl.program_id(0)

    @pl.when(i == 0)
    def _():
        s_ref[...] = s0_ref[...]

    c, dk = DN_CHUNK, DN_HEAD_DIM
    dirs = ((uf_ref, wf_ref, qgf_ref, af_ref, kdf_ref, egf_ref, of_ref),
            (ur_ref, wr_ref, qgr_ref, ar_ref, kdr_ref, egr_ref, or_ref))

    def chunk_step(j, carry):
        wq_l, u_l, a_l, kd_l, eg_l, s_l, dst = [], [], [], [], [], [], []
        for r, (u_ref, w_ref, qg_ref, a_ref, kd_ref, eg_ref, o_ref) in enumerate(dirs):
            ch = nc - 1 - j if r else j
            rs = pl.ds(pl.multiple_of(ch * c, c), c)
            for b in range(nb):
                for h in range(DN_HEADS):
                    hs = slice(h * dk, (h + 1) * dk)
                    s = r * DN_HEADS + h
                    wq_l.append(jnp.concatenate([w_ref[b, rs, hs], qg_ref[b, rs, hs]], axis=0))
                    u_l.append(u_ref[b, rs, hs])
                    a_l.append(a_ref[b, rs, hs][:, :c])
                    kd_l.append(kd_ref[b, ch, h])
                    eg_l.append(eg_ref[b, ch, s:s + 1, :])
                    s_l.append(s_ref[b, s])
                    dst.append((o_ref, b, rs, hs, s))
        st = jnp.stack(s_l)
        ws_qs = _bmm(jnp.stack(wq_l), st.astype(BF16))
        v_new = (jnp.stack(u_l).astype(F32) - ws_qs[:, :c]).astype(BF16)
        o = ws_qs[:, c:] + _bmm(jnp.stack(a_l), v_new)
        s_new = st * jnp.stack(eg_l) + _bmm(jnp.stack(kd_l), v_new)
        for n_, (o_ref, b, rs, hs, s) in enumerate(dst):
            o_ref[b, rs, hs] = o[n_].astype(BF16)
            s_ref[b, s] = s_new[n_]
        return carry

    lax.fori_loop(0, nc, chunk_step, 0)

    @pl.when(i == pl.num_programs(0) - 1)
    def _():
        sfin_ref[...] = s_ref[...]


def _delta_scan(u, w, qg, a, kdt, eg, s0, *, ts):
    b, t, w2 = u.shape
    wd = w2 // N_DIR
    n = t // ts
    nc = ts // DN_CHUNK
    fwd = pl.BlockSpec((b, ts, wd), lambda i: (0, i, 0))
    rev = pl.BlockSpec((b, ts, wd), lambda i: (0, n - 1 - i, 1))
    kd_f = pl.BlockSpec((b, nc, DN_HEADS) + kdt.shape[3:], lambda i: (0, i, 0, 0, 0))
    kd_r = pl.BlockSpec((b, nc, DN_HEADS) + kdt.shape[3:], lambda i: (0, n - 1 - i, 1, 0, 0))
    eg_f = pl.BlockSpec((b, nc) + eg.shape[2:], lambda i: (0, i, 0, 0))
    eg_r = pl.BlockSpec((b, nc) + eg.shape[2:], lambda i: (0, n - 1 - i, 0, 0))
    out_f = pl.BlockSpec((b, ts, wd), lambda i: (0, i, 0))
    out_r = pl.BlockSpec((b, ts, wd), lambda i: (0, n - 1 - i, 0))
    return pl.pallas_call(
        functools.partial(_dscan_kernel, nc=nc, nb=b),
        grid=(n,),
        in_specs=[fwd, fwd, fwd, fwd, kd_f, eg_f, rev, rev, rev, rev, kd_r, eg_r, _resident(s0.shape)],
        out_specs=[out_f, out_r, pl.BlockSpec(s0.shape, lambda i: (0, 0, 0, 0))],
        out_shape=[jax.ShapeDtypeStruct((b, t, wd), BF16)] * 2 + [jax.ShapeDtypeStruct(s0.shape, F32)],
        scratch_shapes=[pltpu.VMEM(s0.shape, F32)],
        compiler_params=_params(("arbitrary",)),
        name="delta_scan",
    )(u, w, qg, a, kdt, eg, u, w, qg, a, kdt, eg, s0)


S5_POW_ROWS = 24


def _cexp(re, im):
    m = jnp.exp(re)
    return m * jnp.cos(im), m * jnp.sin(im)


def _s5_param_kernel(are_ref, aim_ref, ls_ref, bre_ref, bim_ref, cre_ref, cim_ref,
                     wkr_ref, wki_ref, kall_ref, k0_ref, mr_ref, mi_ref, llr_ref, lli_ref):
    l, p, n = S5_CHUNK, S5_GROUP, S5_STATE

    def rows_of_power(tab, first):
        return jnp.concatenate([jnp.broadcast_to(tab[first + k:first + k + 1], (p, n)) for k in range(l)], axis=0)

    def rows_of_q(tab):
        return jnp.concatenate([tab] * l, axis=0)

    k0 = None
    for r in range(N_DIR):
        are, aim = are_ref[r, 0], aim_ref[r, 0]
        dt = jnp.exp(ls_ref[r, 0])
        kk = lax.broadcasted_iota(jnp.int32, (S5_POW_ROWS, n), 0).astype(F32)
        pr, pi = _cexp(kk * (are * dt), kk * (aim * dt))
        lbr, lbi = pr[1:2], pi[1:2]
        den = are * are + aim * aim
        nr, ni = lbr - 1.0, lbi
        cr = (nr * are + ni * aim) / den
        ci = (ni * are - nr * aim) / den
        bre, bim = bre_ref[r, 0], bim_ref[r, 0]
        bbr = cr * bre - ci * bim
        bbi = cr * bim + ci * bre
        pkr, pki = rows_of_power(pr, 0), rows_of_power(pi, 0)
        bqr, bqi = rows_of_q(bbr), rows_of_q(bbi)
        wr = bqr * pkr - bqi * pki
        wi = bqr * pki + bqi * pkr
        wkr_ref[r, 0] = wr
        wki_ref[r, 0] = wi
        cre, cim = cre_ref[r, 0], cim_ref[r, 0]
        kall = _dot_nt_f32(wr, cre) - _dot_nt_f32(wi, cim)
        kall_ref[r, 0] = kall
        k0 = kall[0:p] if k0 is None else k0 + kall[0:p]
        pkr, pki = rows_of_power(pr, 1), rows_of_power(pi, 1)
        cqr, cqi = rows_of_q(cre), rows_of_q(cim)
        mr_ref[r, 0] = cqr * pkr - cqi * pki
        mi_ref[r, 0] = -(cqr * pki + cqi * pkr)
        llr_ref[r, 0] = pr[l:l + 1]
        lli_ref[r, 0] = pi[l:l + 1]
    k0_ref[0] = k0


def _s5_operators(a_re, a_im, log_step, b_re, b_im, c_re, c_im):
    g, n, p, l = S5_GROUPS, S5_STATE, S5_GROUP, S5_CHUNK
    lp = l * p
    row = lambda a: a[:, :, None, :]
    ls = jnp.broadcast_to(log_step[:, :, None, None], (N_DIR, g, 1, n))
    bt_re, bt_im = jnp.swapaxes(b_re, 2, 3), jnp.swapaxes(b_im, 2, 3)
    blk = lambda r, c: pl.BlockSpec((N_DIR, 1, r, c), lambda i: (0, i, 0, 0))
    sds = lambda r, c: jax.ShapeDtypeStruct((N_DIR, g, r, c), F32)
    wkr, wki, kall, k0, mr, mi, llr, lli = pl.pallas_call(
        _s5_param_kernel,
        grid=(g,),
        in_specs=[blk(1, n)] * 3 + [blk(p, n)] * 4,
        out_specs=[blk(lp, n), blk(lp, n), blk(lp, p), pl.BlockSpec((1, p, p), lambda i: (i, 0, 0)),
                   blk(lp, n), blk(lp, n), blk(1, n), blk(1, n)],
        out_shape=[sds(lp, n), sds(lp, n), sds(lp, p), jax.ShapeDtypeStruct((g, p, p), F32),
                   sds(lp, n), sds(lp, n), sds(1, n), sds(1, n)],
        compiler_params=_params(("parallel",)),
        name="s5_params",
    )(row(a_re), row(a_im), ls, bt_re, bt_im, c_re, c_im)

    nb, g8 = S5_NBLK, S5_BLK
    same = jnp.eye(g8, dtype=bool)

    def chunked(a):
        return a.reshape(nb, g8, l, p, a.shape[-1])

    wz = jnp.stack([jnp.stack([jnp.flip(chunked(w[0]), axis=2) for w in (wkr, wki)]),
                    jnp.stack([chunked(w[1]) for w in (wkr, wki)])])
    wz = jnp.transpose(wz, (2, 4, 3, 5, 0, 1, 6))
    w_in = jnp.where(same[None, None, :, None, None, None, :, None], wz[..., None, :], 0.0)
    w_in = w_in.reshape(nb, l * g8 * p, N_DIR * 2 * g8 * n).astype(BF16)

    kf = kall[0].reshape(g, l, p, p)
    kr = kall[1].reshape(g, l, p, p)
    d = jnp.arange(l)[None, :] - jnp.arange(l)[:, None]
    dd = d[None, :, :, None, None]
    toep = jnp.where(dd > 0, kf[:, jnp.clip(d, 0, l - 1)],
                     jnp.where(dd < 0, kr[:, jnp.clip(-d, 0, l - 1)], k0[:, None, None]))
    toep = jnp.transpose(toep.reshape(nb, g8, l, l, p, p), (0, 2, 1, 4, 3, 5))
    toep = jnp.where(same[None, None, :, None, None, :, None], toep[..., None, :], 0.0)
    toep = toep.reshape(nb, l * g8 * p, l * g8 * p)

    def readout(m, rev):
        m = chunked(m)
        m = jnp.flip(m, axis=2) if rev else m
        m = jnp.transpose(m, (0, 1, 4, 2, 3))
        m = jnp.where(same[None, :, None, None, :, None], m[..., None, :], 0.0)
        return m.reshape(nb, g8 * n, l * g8 * p)

    w_out = jnp.concatenate([toep, readout(mr[0], False), readout(mi[0], False),
                             readout(mr[1], True), readout(mi[1], True)], axis=1).astype(BF16)
    lam = jnp.stack([llr.reshape(N_DIR, g * n), lli.reshape(N_DIR, g * n)])
    return w_in, w_out, lam


def _chunk_rows(u_ref, nct):
    return jnp.concatenate([u_ref[0, pl.ds(s, nct, stride=S5_CHUNK), :] for s in range(S5_CHUNK)],
                           axis=1).astype(BF16)


def _s5_in_kernel(u_ref, w_ref, zfr_ref, zfi_ref, zrr_ref, zri_ref, *, nct):
    z = _dot(_chunk_rows(u_ref, nct), w_ref[0])
    wd = zfr_ref.shape[-1]
    for k, ref in enumerate((zfr_ref, zfi_ref, zrr_ref, zri_ref)):
        ref[...] = z[:, k * wd:(k + 1) * wd]


def _s5_chunk_in(u, w_in):
    b, t, _ = u.shape
    nct = t // S5_CHUNK
    wd = S5_BLK * S5_STATE
    out = pl.BlockSpec((nct, wd), lambda j, i: (i, j))
    return pl.pallas_call(
        functools.partial(_s5_in_kernel, nct=nct),
        grid=(S5_NBLK, b),
        in_specs=[pl.BlockSpec((1, t, LANES), lambda j, i: (i, 0, j)),
                  pl.BlockSpec((1,) + w_in.shape[1:], lambda j, i: (j, 0, 0))],
        out_specs=[out] * 4,
        out_shape=[jax.ShapeDtypeStruct((b * nct, S5_NBLK * wd), F32)] * 4,
        compiler_params=_params(("parallel", "parallel")),
        name="s5_chunk_in",
    )(u, w_in)


SUBLANES = 8


def _cmul(ar, ai, br, bi):
    return ar * br - ai * bi, ar * bi + ai * br


def _s5_scan_tables(lr, li, row, rev):
    pw = [(lr, li)]
    for _ in range(SUBLANES - 1):
        pw.append(_cmul(pw[-1][0], pw[-1][1], lr, li))

    def by_row(power_of):
        tr, ti = jnp.zeros(row.shape, F32), jnp.zeros(row.shape, F32)
        for i in range(SUBLANES):
            k = power_of(i)
            if k:
                tr, ti = jnp.where(row == i, pw[k - 1][0], tr), jnp.where(row == i, pw[k - 1][1], ti)
        return tr, ti

    carry_w = by_row(lambda i: SUBLANES - i if rev else i + 1)
    steps = [by_row(lambda i, d=d: d if ((i <= SUBLANES - 1 - d) if rev else (i >= d)) else 0)
             for d in (1, 2, 4)]
    return carry_w, steps


def _s5_scan_tile(zr, zi, cr, ci, carry_w, steps, row, rev):
    yr, yi = zr, zi
    for d, (tr, ti) in zip((1, 2, 4), steps):
        sh = SUBLANES - d if rev else d
        ar, ai = _cmul(tr, ti, pltpu.roll(yr, sh, 0), pltpu.roll(yi, sh, 0))
        yr, yi = yr + ar, yi + ai
    ar, ai = _cmul(carry_w[0], carry_w[1], cr, ci)
    xr, xi = yr + ar, yi + ai
    sh, edge, last = (SUBLANES - 1, SUBLANES - 1, 0) if rev else (1, 0, SUBLANES - 1)
    er = jnp.where(row == edge, cr, pltpu.roll(xr, sh, 0))
    ei = jnp.where(row == edge, ci, pltpu.roll(xi, sh, 0))
    return er, ei, xr[last:last + 1], xi[last:last + 1]


def _s5_scan_kernel(zfr_ref, zfi_ref, zrr_ref, zri_ref, lam_ref, x0fr_ref, x0fi_ref, x0rr_ref, x0ri_ref,
                    xfr_ref, xfi_ref, xrr_ref, xri_ref, efr_ref, efi_ref, err_ref, eri_ref, *, nc, nb):
    tl = zfr_ref.shape[-1]
    nt = nc // SUBLANES
    row = lax.broadcasted_iota(jnp.int32, (SUBLANES, tl), 0)
    cw_f, st_f = _s5_scan_tables(lam_ref[0, 0:1, :], lam_ref[1, 0:1, :], row, False)
    cw_r, st_r = _s5_scan_tables(lam_ref[0, 1:2, :], lam_ref[1, 1:2, :], row, True)

    def step(t, carry):
        out = []
        for b in range(nb):
            cfr, cfi, crr, cri = carry[b]
            rows = pl.ds(pl.multiple_of(b * nc + t * SUBLANES, SUBLANES), SUBLANES)
            er, ei, cfr, cfi = _s5_scan_tile(zfr_ref[rows, :], zfi_ref[rows, :], cfr, cfi, cw_f, st_f, row, False)
            xfr_ref[rows, :], xfi_ref[rows, :] = er, ei
            rows = pl.ds(pl.multiple_of(b * nc + (nt - 1 - t) * SUBLANES, SUBLANES), SUBLANES)
            er, ei, crr, cri = _s5_scan_tile(zrr_ref[rows, :], zri_ref[rows, :], crr, cri, cw_r, st_r, row, True)
            xrr_ref[rows, :], xri_ref[rows, :] = er, ei
            out.append((cfr, cfi, crr, cri))
        return tuple(out)

    init = tuple(tuple(r[b:b + 1, :] for r in (x0fr_ref, x0fi_ref, x0rr_ref, x0ri_ref)) for b in range(nb))
    ends = lax.fori_loop(0, nt, step, init)
    pad = jnp.zeros((efr_ref.shape[0] - nb, tl), F32)
    for k, ref in enumerate((efr_ref, efi_ref, err_ref, eri_ref)):
        ref[...] = jnp.concatenate([ends[b][k] for b in range(nb)] + [pad], axis=0)


def _s5_state_scan(z4, lam, x0, *, nb, tl):
    rows, w = z4[0].shape
    zb = pl.BlockSpec((rows, tl), lambda j: (0, j))
    cb = pl.BlockSpec((8, tl), lambda j: (0, j))
    outs = pl.pallas_call(
        functools.partial(_s5_scan_kernel, nc=rows // nb, nb=nb),
        grid=(w // tl,),
        in_specs=[zb] * 4 + [pl.BlockSpec((2, N_DIR, tl), lambda j: (0, 0, j))] + [cb] * 4,
        out_specs=[zb] * 4 + [cb] * 4,
        out_shape=[jax.ShapeDtypeStruct((rows, w), F32)] * 4 + [jax.ShapeDtypeStruct((8, w), F32)] * 4,
        compiler_params=_params(("parallel",)),
        name="s5_state_scan",
    )(*z4, lam, *x0)
    return list(outs[:4]), list(outs[4:])


S5_OUT_SPLIT = 2


def _s5_out_kernel(u_ref, xfr_ref, xfi_ref, xrr_ref, xri_ref, w_ref, y_ref, *, nct):
    hf = pl.program_id(2)
    lhs = jnp.concatenate([_chunk_rows(u_ref, nct)] + [r[...].astype(BF16) for r in
                                                       (xfr_ref, xfi_ref, xrr_ref, xri_ref)], axis=1)
    y = _dot(lhs, w_ref[0])
    per = S5_CHUNK // S5_OUT_SPLIT
    for sl in range(per):
        y_ref[0, pl.ds(hf * per + sl, nct, stride=S5_CHUNK), :] = y[:, sl * LANES:(sl + 1) * LANES]


def _s5_chunk_out(u, x4, w_out):
    b, t, _ = u.shape
    nct = t // S5_CHUNK
    wd = S5_BLK * S5_STATE
    xb = pl.BlockSpec((nct, wd), lambda j, i, h: (i, j))
    ub = pl.BlockSpec((1, t, LANES), lambda j, i, h: (i, 0, j))
    wn = w_out.shape[2] // S5_OUT_SPLIT
    return pl.pallas_call(
        functools.partial(_s5_out_kernel, nct=nct),
        grid=(S5_NBLK, b, S5_OUT_SPLIT),
        in_specs=[ub] + [xb] * 4 + [pl.BlockSpec((1, w_out.shape[1], wn), lambda j, i, h: (j, 0, h))],
        out_specs=ub,
        out_shape=jax.ShapeDtypeStruct(u.shape, F32),
        compiler_params=_params(("parallel", "parallel", "arbitrary")),
        name="s5_chunk_out",
    )(u, *x4, w_out)


def _gelu_tanh(x):
    return 0.5 * x * (1.0 + jnp.tanh(0.7978845608028654 * (x + 0.044715 * x * x * x)))


def _merge_kernel(of_ref, or_ref, z_ref, ys_ref, u_ref, g_ref, x_ref, mod_ref,
                  dnw_ref, wa_ref, dsk_ref, wglu_ref, bglu_ref, wb_ref, wo_ref, n2w_ref,
                  xl_ref, h2_ref, *, d):
    o = of_ref[0].astype(F32) + or_ref[0].astype(F32)
    z = z_ref[0].astype(F32)
    heads = []
    for h in range(DN_HEADS):
        sl = slice(h * DN_HEAD_DIM, (h + 1) * DN_HEAD_DIM)
        oh = o[:, sl]
        on = oh * lax.rsqrt(jnp.mean(oh * oh, axis=-1, keepdims=True) + RMS_EPS) * dnw_ref[...]
        heads.append((on * _silu(z[:, sl])).astype(BF16))
    ya = _dot(jnp.concatenate(heads, axis=1), wa_ref[...])
    ys = ys_ref[0] + dsk_ref[...] * u_ref[0]
    zz = _dot(_gelu_tanh(ys).astype(BF16), wglu_ref[...]) + bglu_ref[...]
    yb = _dot((zz[:, :S5_WIDTH] * jax.nn.sigmoid(zz[:, S5_WIDTH:])).astype(BF16), wb_ref[...])
    gates = g_ref[0].astype(F32)
    mix = jax.nn.sigmoid(gates[:, :d]) * ya + jax.nn.sigmoid(gates[:, d:]) * yb
    xl = x_ref[0] + mod_ref[0, :, 2 * d:3 * d] * _dot(mix.astype(BF16), wo_ref[...])
    xl_ref[0] = xl
    hn = xl * lax.rsqrt(jnp.mean(xl * xl, axis=-1, keepdims=True) + RMS_EPS) * n2w_ref[...]
    h2_ref[0] = (hn * (1.0 + mod_ref[0, :, 4 * d:5 * d]) + mod_ref[0, :, 3 * d:4 * d]).astype(BF16)


def _mix_merge(o_f, o_r, z, ys, u, gates, x, mods, dn_norm_w, w_a_out, s5_d, w_glu, b_glu,
               w_b_out, w_o, norm2_w, *, tm):
    b, t, d = x.shape
    tok = lambda n: pl.BlockSpec((1, tm, n), lambda i, j: (i, j, 0))
    consts = [dn_norm_w.reshape(1, -1), w_a_out, s5_d.reshape(1, -1), w_glu, b_glu.reshape(1, -1),
              w_b_out, w_o, norm2_w.reshape(1, -1)]
    return pl.pallas_call(
        functools.partial(_merge_kernel, d=d),
        grid=(b, t // tm),
        in_specs=[tok(o_f.shape[-1]), tok(o_r.shape[-1]), tok(z.shape[-1]), tok(ys.shape[-1]),
                  tok(u.shape[-1]), tok(gates.shape[-1]), tok(d),
                  pl.BlockSpec((1, 1, mods.shape[-1]), lambda i, j: (i, 0, 0))]
        + [_resident(c.shape) for c in consts],
        out_specs=[tok(d), tok(d)],
        out_shape=[jax.ShapeDtypeStruct((b, t, d), F32), jax.ShapeDtypeStruct((b, t, d), BF16)],
        compiler_params=_params(("parallel", "parallel")),
        name="mix_merge",
    )(o_f, o_r, z, ys, u, gates, x, mods, *consts)


FFN_ROWS = 8
FFN_CB = 256


FFN_DOWN_GROUP = 4
FFN_AHEAD = 2


def _conv_row(e_ref, slot, part, r, cw):
    rows = [e_ref[slot, part, (r + i) * GRID_W:(r + i + 1) * GRID_W, :] for i in range(3)]
    taps = [rows[0] * cw[j:j + 1, :] + rows[1] * cw[3 + j:4 + j, :] + rows[2] * cw[6 + j:7 + j, :]
            for j in range(3)]
    left = pltpu.roll(taps[0], 1, 0)
    right = pltpu.roll(taps[2], GRID_W - 1, 0)
    sub = lax.broadcasted_iota(jnp.int32, (SUBLANES, left.shape[1]), 0)
    left = jnp.concatenate([jnp.where(sub == 0, 0.0, left[:SUBLANES]), left[SUBLANES:]], axis=0)
    right = jnp.concatenate([right[:-SUBLANES], jnp.where(sub == SUBLANES - 1, 0.0, right[-SUBLANES:])], axis=0)
    return left + taps[1] + right


def _ffn_kernel(h_ref, hp_ref, hn_ref, xl_ref, mod_ref, wup_ref, cw_ref, wd_ref, nfw_ref,
                o_ref, hext_ref, e_ref, act_ref, acc_ref, *, d, dff, nt):
    t = pl.program_id(1)
    n_out = h_ref.shape[1]
    hext_ref[0:GRID_W] = jnp.where(t == 0, jnp.zeros_like(hp_ref[0]), hp_ref[0])
    hext_ref[GRID_W:GRID_W + n_out] = h_ref[0]
    hext_ref[GRID_W + n_out:] = jnp.where(t == nt - 1, jnp.zeros_like(hn_ref[0]), hn_ref[0])
    ncb = dff // FFN_CB

    def up(k, slot):
        e_ref[slot, 0] = _dot(hext_ref[...], wup_ref[:, k * FFN_CB:(k + 1) * FFN_CB])
        e_ref[slot, 1] = _dot(hext_ref[...], wup_ref[:, dff + k * FFN_CB:dff + (k + 1) * FFN_CB])

    for k in range(FFN_AHEAD):
        up(k, k)
    done = 0
    for k in range(ncb):
        slot = k % (FFN_AHEAD + 1)
        if k + FFN_AHEAD < ncb:
            up(k + FFN_AHEAD, (k + FFN_AHEAD) % (FFN_AHEAD + 1))
        gs = slice(k * FFN_CB, (k + 1) * FFN_CB)
        cg = cw_ref[:, gs]
        cv = cw_ref[:, dff + k * FFN_CB:dff + (k + 1) * FFN_CB]
        for r in range(n_out // GRID_W):
            gate = _conv_row(e_ref, slot, 0, r, cg)
            val = _conv_row(e_ref, slot, 1, r, cv)
            act_ref[r * GRID_W:(r + 1) * GRID_W, gs] = (_silu(gate) * val).astype(BF16)
        if (k + 1) % FFN_DOWN_GROUP == 0 or k == ncb - 1:
            ks = slice(done * FFN_CB, (k + 1) * FFN_CB)
            part = _dot(act_ref[:, ks], wd_ref[ks, :])
            if done == 0:
                acc_ref[...] = part
            else:
                acc_ref[...] += part
            done = k + 1
    xo = xl_ref[0] + mod_ref[0, :, 5 * d:6 * d] * acc_ref[...]
    o_ref[0] = xo * lax.rsqrt(jnp.mean(xo * xo, axis=-1, keepdims=True) + RMS_EPS) * nfw_ref[...]


def _conv_ffn(h2, xl, mods, w_up, conv_w, w_down, norm_f_w):
    b, t, d = xl.shape
    dff = w_down.shape[0]
    tm = FFN_ROWS * GRID_W
    nt = t // tm
    nrow = t // GRID_W
    cw = conv_w.reshape(9, 2 * dff)
    tok = lambda: pl.BlockSpec((1, tm, d), lambda i, j: (i, j, 0))
    return pl.pallas_call(
        functools.partial(_ffn_kernel, d=d, dff=dff, nt=nt),
        grid=(b, nt),
        in_specs=[tok(),
                  pl.BlockSpec((1, GRID_W, d), lambda i, j: (i, jnp.maximum(j * FFN_ROWS - 1, 0), 0)),
                  pl.BlockSpec((1, GRID_W, d), lambda i, j: (i, jnp.minimum((j + 1) * FFN_ROWS, nrow - 1), 0)),
                  tok(),
                  pl.BlockSpec((1, 1, mods.shape[-1]), lambda i, j: (i, 0, 0)),
                  _resident(w_up.shape), _resident(cw.shape), _resident(w_down.shape), _resident((1, d))],
        out_specs=tok(),
        out_shape=jax.ShapeDtypeStruct((b, t, d), F32),
        scratch_shapes=[pltpu.VMEM((tm + 2 * GRID_W, d), BF16),
                        pltpu.VMEM((FFN_AHEAD + 1, 2, tm + 2 * GRID_W, FFN_CB), F32),
                        pltpu.VMEM((tm, dff), BF16),
                        pltpu.VMEM((tm, d), F32)],
        compiler_params=_params(("parallel", "parallel")),
        name="conv_ffn",
    )(h2, h2, h2, xl, mods, w_up.astype(BF16), cw, w_down.astype(BF16), norm_f_w.reshape(1, d))


def _lane_row(a):
    a = a.reshape(-1).astype(F32)
    return jnp.pad(a, (0, LANES - a.shape[0])).reshape(1, LANES)


def _pad_cols(w):
    return jnp.pad(w, ((0, 0), (0, LANES - w.shape[1])))


def kernel(x, c, ctx, c_ctx, w_ada, b_ada, norm1_w, w_in, dn_conv_w, dn_a_log, dn_dt_bias, dn_norm_w,
           w_a_out, s5_a_re, s5_a_im, s5_log_step, s5_b_re, s5_b_im, s5_c_re, s5_c_im, s5_d, w_glu,
           b_glu, w_b_out, w_o, norm2_w, w_up, ffn_conv_w, w_down, norm_f_w):
    assert w_ada.shape[0] == 1, "single-layer block"
    b, t, d = x.shape
    tc = ctx.shape[1]
    nh = N_DIR * DN_HEADS
    assert b < 8, "batch rows and the context row share one 8-row modulation block"

    c_rows = jnp.zeros((8, d), F32).at[:b].set(c).at[b].set(c_ctx)
    mods = _modulation(c_rows, w_ada[0], b_ada[0]).reshape(8, 1, N_MOD * d)

    w = w_in[0]
    o_z, o_b, o_a = 3 * DN_WIDTH, 4 * DN_WIDTH, 4 * DN_WIDTH + nh
    o_u = o_a + nh
    o_g = o_u + S5_WIDTH
    wqkv = w[:, :o_z].astype(BF16)
    wz = w[:, o_z:o_b].astype(BF16)
    wba = jnp.concatenate([_pad_cols(w[:, o_b:o_a]), _pad_cols(w[:, o_a:o_u])], axis=1).astype(BF16)
    wu = w[:, o_u:o_g].astype(BF16)
    wg = w[:, o_g:].astype(BF16)

    qkv_l, ba_l, u_l, z_l, gates_l = _in_proj(x, mods, lambda i: i, norm1_w[0], wqkv, wba, wu, wz, wg, tm=512)
    qkv_c, ba_c, u_c = _in_proj(ctx, mods, lambda i: b, norm1_w[0], wqkv, wba, wu, tm=tc)

    alog_row, dtb_row = _lane_row(dn_a_log[0]), _lane_row(dn_dt_bias[0])
    prep_c = _delta_prep(qkv_c, ba_c, dn_conv_w[0], alog_row, dtb_row, tm=tc)
    prep_l = _delta_prep(qkv_l, ba_l, dn_conv_w[0], alog_row, dtb_row, tm=512)
    s0 = jnp.zeros((b, nh, DN_HEAD_DIM, DN_HEAD_DIM), F32)
    _, _, s_ctx = _delta_scan(*_delta_wy(*prep_c, ts=tc), s0, ts=128)
    o_f, o_r, _ = _delta_scan(*_delta_wy(*prep_l, ts=256), s_ctx, ts=128)

    w_s5in, w_s5out, lam = _s5_operators(s5_a_re[0], s5_a_im[0], s5_log_step[0], s5_b_re[0], s5_b_im[0],
                                         s5_c_re[0], s5_c_im[0])
    zero = jnp.zeros((8, S5_GROUPS * S5_STATE), F32)
    _, x_ctx = _s5_state_scan(_s5_chunk_in(u_c, w_s5in), lam, [zero] * 4, nb=b, tl=256)
    x_l, _ = _s5_state_scan(_s5_chunk_in(u_l, w_s5in), lam, x_ctx, nb=b, tl=256)
    y_s5 = _s5_chunk_out(u_l, x_l, w_s5out)

    xl, h2 = _mix_merge(o_f, o_r, z_l, y_s5, u_l, gates_l, x, mods, dn_norm_w[0], w_a_out[0].astype(BF16),
                        s5_d[0], w_glu[0].astype(BF16), b_glu[0], w_b_out[0].astype(BF16),
                        w_o[0].astype(BF16), norm2_w[0], tm=512)
    return _conv_ffn(h2, xl, mods, w_up[0], ffn_conv_w[0], w_down[0], norm_f_w)
```

```python
import functools

import jax
import jax.numpy as jnp
from jax import lax
from jax.experimental import pallas as pl
from jax.experimental.pallas import tpu as pltpu

F32 = jnp.float32
BF16 = jnp.bfloat16

GRID_W = 64
N_DIR = 2
DN_HEADS = 4
DN_HEAD_DIM = 128
DN_WIDTH = DN_HEADS * DN_HEAD_DIM
DN_CHUNK = 64
S5_WIDTH = 512
S5_GROUP = 16
S5_GROUPS = S5_WIDTH // S5_GROUP
S5_STATE = 64
S5_CHUNK = 16
N_MOD = 6
RMS_EPS = 1e-6
L2_EPS = 1e-6
LANES = 128
S5_BLK = LANES // S5_GROUP
S5_NBLK = S5_GROUPS // S5_BLK
VMEM_LIMIT = 56 * 1024 * 1024


def _dot(a, b):
    return jnp.dot(a, b, preferred_element_type=F32)


def _dot_f32(a, b):
    return jnp.dot(a, b, preferred_element_type=F32, precision=lax.Precision.HIGHEST)


def _dot_nt_f32(a, b):
    return lax.dot_general(a, b, (((1,), (1,)), ((), ())), preferred_element_type=F32,
                           precision=lax.Precision.HIGHEST)


def _silu(x):
    return x * jax.nn.sigmoid(x)


def _softplus(x):
    return jnp.maximum(x, 0.0) + jnp.log(1.0 + jnp.exp(-jnp.abs(x)))


def _params(sem, vmem=VMEM_LIMIT):
    return pltpu.CompilerParams(dimension_semantics=sem, vmem_limit_bytes=vmem)


def _resident(shape):
    nd = len(shape)
    return pl.BlockSpec(shape, lambda *_: (0,) * nd, pipeline_mode=pl.Buffered(1))


def _mod_kernel(c_ref, w_ref, b_ref, o_ref):
    sc = _silu(c_ref[...])
    o_ref[...] = _dot(sc.astype(BF16), w_ref[...].astype(BF16)) + b_ref[...]


def _modulation(c_rows, w_ada, b_ada):
    d, n = w_ada.shape
    tn = n // 4
    return pl.pallas_call(
        _mod_kernel,
        grid=(n // tn,),
        in_specs=[pl.BlockSpec(c_rows.shape, lambda j: (0, 0)),
                  pl.BlockSpec((d, tn), lambda j: (0, j)),
                  pl.BlockSpec((1, tn), lambda j: (0, j))],
        out_specs=pl.BlockSpec((c_rows.shape[0], tn), lambda j: (0, j)),
        out_shape=jax.ShapeDtypeStruct((c_rows.shape[0], n), F32),
        compiler_params=_params(("arbitrary",)),
        name="adaln_mod",
    )(c_rows, w_ada, b_ada.reshape(1, n))


def _inproj_kernel(x_ref, mod_ref, nw_ref, wqkv_ref, wba_ref, wu_ref, wz_ref, wg_ref,
                   qkv_ref, ba_ref, u_ref, z_ref, g_ref, *, d):
    x = x_ref[0]
    ms = jnp.mean(x * x, axis=-1, keepdims=True)
    h = x * lax.rsqrt(ms + RMS_EPS) * nw_ref[...]
    shift = mod_ref[0, :, 0:d]
    scale = mod_ref[0, :, d:2 * d]
    hb = (h * (1.0 + scale) + shift).astype(BF16)
    qkv_ref[0] = _dot(hb, wqkv_ref[...]).astype(BF16)
    ba_ref[0] = _dot(hb, wba_ref[...])
    u_ref[0] = _dot(hb, wu_ref[...])
    if z_ref is not None:
        z_ref[0] = _dot(hb, wz_ref[...]).astype(BF16)
        g_ref[0] = _dot(hb, wg_ref[...]).astype(BF16)


def _inproj_ctx_kernel(x_ref, mod_ref, nw_ref, wqkv_ref, wba_ref, wu_ref,
                       qkv_ref, ba_ref, u_ref, *, d):
    _inproj_kernel(x_ref, mod_ref, nw_ref, wqkv_ref, wba_ref, wu_ref, None, None,
                   qkv_ref, ba_ref, u_ref, None, None, d=d)


def _in_proj(x, mods, mod_row0, norm_w, wqkv, wba, wu, wz=None, wg=None, *, tm):
    b, t, d = x.shape
    full = wz is not None
    tok = lambda n: pl.BlockSpec((1, tm, n), lambda i, j: (i, j, 0))
    in_specs = [tok(d),
                pl.BlockSpec((1, 1, mods.shape[-1]), lambda i, j: (mod_row0(i), 0, 0)),
                _resident((1, d)), _resident(wqkv.shape), _resident(wba.shape),
                _resident(wu.shape)]
    args = [x, mods, norm_w.reshape(1, d), wqkv, wba, wu]
    out_specs = [tok(wqkv.shape[1]), tok(wba.shape[1]), tok(wu.shape[1])]
    out_shape = [jax.ShapeDtypeStruct((b, t, wqkv.shape[1]), BF16),
                 jax.ShapeDtypeStruct((b, t, wba.shape[1]), F32),
                 jax.ShapeDtypeStruct((b, t, wu.shape[1]), F32)]
    if full:
        in_specs += [_resident(wz.shape), _resident(wg.shape)]
        args += [wz, wg]
        out_specs += [tok(wz.shape[1]), tok(wg.shape[1])]
        out_shape += [jax.ShapeDtypeStruct((b, t, wz.shape[1]), BF16),
                      jax.ShapeDtypeStruct((b, t, wg.shape[1]), BF16)]
    body = functools.partial(_inproj_kernel if full else _inproj_ctx_kernel, d=d)
    return pl.pallas_call(
        body, grid=(b, t // tm), in_specs=in_specs, out_specs=out_specs, out_shape=out_shape,
        compiler_params=_params(("parallel", "parallel")),
        name="in_proj" if full else "in_proj_ctx",
    )(*args)


HALO = 16


def _dprep_kernel(x_ref, xp_ref, xn_ref, ba_ref, cw_ref, alog_ref, dtb_ref,
                  q_ref, k_ref, v_ref, beta_ref, g_ref, *, tm, nt):
    t = pl.program_id(1)
    x = x_ref[0].astype(F32)
    prow = jnp.where(t == 0, 0.0, xp_ref[0, HALO - 1:HALO, :].astype(F32))
    nrow = jnp.where(t == nt - 1, 0.0, xn_ref[0, 0:1, :].astype(F32))
    rows = lax.broadcasted_iota(jnp.int32, x.shape, 0)
    xprev = jnp.where(rows == 0, prow, pltpu.roll(x, 1, 0))
    xnext = jnp.where(rows == tm - 1, nrow, pltpu.roll(x, tm - 1, 0))
    y = _silu(xprev * cw_ref[0:1, :] + x * cw_ref[1:2, :] + xnext * cw_ref[2:3, :])
    for h in range(DN_HEADS):
        sl = slice(h * DN_HEAD_DIM, (h + 1) * DN_HEAD_DIM)
        qh = y[:, h * DN_HEAD_DIM:(h + 1) * DN_HEAD_DIM]
        kh = y[:, DN_WIDTH + h * DN_HEAD_DIM:DN_WIDTH + (h + 1) * DN_HEAD_DIM]
        qn = qh * lax.rsqrt(jnp.sum(qh * qh, axis=-1, keepdims=True) + L2_EPS)
        kn = kh * lax.rsqrt(jnp.sum(kh * kh, axis=-1, keepdims=True) + L2_EPS)
        q_ref[0, :, sl] = (qn * (DN_HEAD_DIM ** -0.5)).astype(BF16)
        k_ref[0, :, sl] = kn.astype(BF16)
    v_ref[0] = y[:, 2 * DN_WIDTH:3 * DN_WIDTH].astype(BF16)
    ba = ba_ref[0]
    beta_ref[0] = jax.nn.sigmoid(ba[:, 0:LANES])
    g_ref[0] = -jnp.exp(alog_ref[...]) * _softplus(ba[:, LANES:2 * LANES] + dtb_ref[...])


def _delta_prep(qkv, ba, conv_w, alog_row, dtb_row, *, tm):
    b, t, c = qkv.shape
    nt = t // tm
    r = tm // HALO
    tok = lambda n: pl.BlockSpec((1, tm, n), lambda i, j: (i, j, 0))
    return pl.pallas_call(
        functools.partial(_dprep_kernel, tm=tm, nt=nt),
        grid=(b, nt),
        in_specs=[tok(c),
                  pl.BlockSpec((1, HALO, c), lambda i, j: (i, jnp.maximum(j * r - 1, 0), 0)),
                  pl.BlockSpec((1, HALO, c), lambda i, j: (i, jnp.minimum((j + 1) * r, t // HALO - 1), 0)),
                  tok(ba.shape[-1]),
                  _resident(conv_w.shape), _resident(alog_row.shape), _resident(dtb_row.shape)],
        out_specs=[tok(DN_WIDTH), tok(DN_WIDTH), tok(DN_WIDTH), tok(LANES), tok(LANES)],
        out_shape=[jax.ShapeDtypeStruct((b, t, DN_WIDTH), BF16)] * 3
        + [jax.ShapeDtypeStruct((b, t, LANES), F32)] * 2,
        compiler_params=_params(("parallel", "parallel")),
        name="delta_prep",
    )(qkv, qkv, qkv, ba, conv_w, alog_row, dtb_row)


def _bmm(a, b):
    return jnp.einsum('nik,nkj->nij', a, b, preferred_element_type=F32)


def _bmm_nt(a, b):
    return jnp.einsum('nik,njk->nij', a, b, preferred_element_type=F32)


def _dwy_kernel(q_ref, k_ref, v_ref, b_ref, g_ref, u_ref, w_ref, qg_ref, a_ref, kdt_ref, eg_ref, *, nc):
    c, dk = DN_CHUNK, DN_HEAD_DIM
    ts = nc * c
    ii = lax.broadcasted_iota(jnp.int32, (c, c), 0)
    jj = lax.broadcasted_iota(jnp.int32, (c, c), 1)
    eye = (ii == jj).astype(F32)
    ti = lax.broadcasted_iota(jnp.int32, (ts, ts), 0)
    tj = lax.broadcasted_iota(jnp.int32, (ts, ts), 1)
    same_chunk = (ti // c) == (tj // c)
    blocks = [(ch, h) for ch in range(nc) for h in range(DN_HEADS)]
    tile = lambda ref, ch, h: ref[0, ch * c:(ch + 1) * c, h * dk:(h + 1) * dk]
    k_l = [tile(k_ref, ch, h) for ch, h in blocks]
    q_l = [tile(q_ref, ch, h) for ch, h in blocks]
    v_l = [tile(v_ref, ch, h) for ch, h in blocks]
    kkqk = _bmm_nt(jnp.stack([jnp.concatenate([k_, q_], axis=0) for k_, q_ in zip(k_l, q_l)]),
                   jnp.stack(k_l))
    beta = b_ref[0]
    g = g_ref[0]
    a_pad = jnp.zeros((c, dk - c), BF16)
    for r in range(N_DIR):
        incl = ii <= jj if r else ii >= jj
        strict = ii < jj if r else ii > jj
        tri = (same_chunk & (ti <= tj if r else ti >= tj)).astype(F32)
        g_cum = _dot_f32(tri, g)
        g_cum_t = g_cum.T
        neg_l, rhs_l = [], []
        for n_, (ch, h) in enumerate(blocks):
            s = r * DN_HEADS + h
            rs = slice(ch * c, (ch + 1) * c)
            ls = slice(s * dk, (s + 1) * dk)
            g_c = g_cum[rs, s:s + 1]
            g_r = g_cum_t[s:s + 1, rs]
            g_end = g_c[0:1] if r else g_c[c - 1:c]
            b_c = beta[rs, s:s + 1]
            decay = jnp.where(incl, jnp.exp(jnp.where(incl, g_c - g_r, 0.0)), 0.0)
            neg_l.append(jnp.where(strict, kkqk[n_, :c] * (-b_c) * decay, 0.0))
            eg = jnp.exp(g_c)
            kf = k_l[n_].astype(F32)
            rhs_l.append(jnp.concatenate([(v_l[n_].astype(F32) * b_c).astype(BF16),
                                          (kf * (b_c * eg)).astype(BF16)], axis=1))
            qg_ref[0, rs, ls] = (q_l[n_].astype(F32) * eg).astype(BF16)
            a_ref[0, rs, ls] = jnp.concatenate([(kkqk[n_, c:] * decay).astype(BF16), a_pad], axis=1)
            kdt_ref[0, ch, s] = (kf * jnp.exp(g_end - g_c)).T.astype(BF16)
            eg_ref[0, ch, s:s + 1, :] = jnp.broadcast_to(jnp.exp(g_end), (1, LANES))
        p = jnp.stack(neg_l)
        pb = p.astype(BF16)
        tinv = eye + p
        m = 2
        while m < c:
            pb = _bmm(pb, pb).astype(BF16)
            tinv = tinv + _bmm(tinv.astype(BF16), pb)
            m *= 2
        sol = _bmm(tinv.astype(BF16), jnp.stack(rhs_l))
        for n_, (ch, h) in enumerate(blocks):
            s = r * DN_HEADS + h
            rs = slice(ch * c, (ch + 1) * c)
            ls = slice(s * dk, (s + 1) * dk)
            u_ref[0, rs, ls] = sol[n_, :, :dk].astype(BF16)
            w_ref[0, rs, ls] = sol[n_, :, dk:].astype(BF16)


def _delta_wy(q, k, v, beta, g, *, ts):
    b, t, w = q.shape
    nc = ts // DN_CHUNK
    ns = N_DIR * DN_HEADS
    tok = lambda n: pl.BlockSpec((1, ts, n), lambda i, j: (i, j, 0))
    return pl.pallas_call(
        functools.partial(_dwy_kernel, nc=nc),
        grid=(b, t // ts),
        in_specs=[tok(w), tok(w), tok(w), tok(LANES), tok(LANES)],
        out_specs=[tok(N_DIR * w)] * 4
        + [pl.BlockSpec((1, nc, ns, DN_HEAD_DIM, DN_CHUNK), lambda i, j: (i, j, 0, 0, 0)),
           pl.BlockSpec((1, nc, ns, LANES), lambda i, j: (i, j, 0, 0))],
        out_shape=[jax.ShapeDtypeStruct((b, t, N_DIR * w), BF16)] * 4
        + [jax.ShapeDtypeStruct((b, t // DN_CHUNK, ns, DN_HEAD_DIM, DN_CHUNK), BF16),
           jax.ShapeDtypeStruct((b, t // DN_CHUNK, ns, LANES), F32)],
        compiler_params=_params(("parallel", "parallel")),
        name="delta_wy",
    )(q, k, v, beta, g)


def _dscan_kernel(uf_ref, wf_ref, qgf_ref, af_ref, kdf_ref, egf_ref,
                  ur_ref, wr_ref, qgr_ref, ar_ref, kdr_ref, egr_ref, s0_ref,
                  of_ref, or_ref, sfin_ref, s_ref, *, nc, nb):
    i = pl.program_id(0)

    @pl.when(i == 0)
    def _():
        s_ref[...] = s0_ref[...]

    c, dk = DN_CHUNK, DN_HEAD_DIM
    dirs = ((uf_ref, wf_ref, qgf_ref, af_ref, kdf_ref, egf_ref, of_ref),
            (ur_ref, wr_ref, qgr_ref, ar_ref, kdr_ref, egr_ref, or_ref))

    def chunk_step(j, carry):
        wq_l, u_l, a_l, kd_l, eg_l, s_l, dst = [], [], [], [], [], [], []
        for r, (u_ref, w_ref, qg_ref, a_ref, kd_ref, eg_ref, o_ref) in enumerate(dirs):
            ch = nc - 1 - j if r else j
            rs = pl.ds(pl.multiple_of(ch * c, c), c)
            for b in range(nb):
                for h in range(DN_HEADS):
                    hs = slice(h * dk, (h + 1) * dk)
                    s = r * DN_HEADS + h
                    wq_l.append(jnp.concatenate([w_ref[b, rs, hs], qg_ref[b, rs, hs]], axis=0))
                    u_l.append(u_ref[b, rs, hs])
                    a_l.append(a_ref[b, rs, hs][:, :c])
                    kd_l.append(kd_ref[b, ch, h])
                    eg_l.append(eg_ref[b, ch, s:s + 1, :])
                    s_l.append(s_ref[b, s])
                    dst.append((o_ref, b, rs, hs, s))
        st = jnp.stack(s_l)
        ws_qs = _bmm(jnp.stack(wq_l), st.astype(BF16))
        v_new = (jnp.stack(u_l).astype(F32) - ws_qs[:, :c]).astype(BF16)
        o = ws_qs[:, c:] + _bmm(jnp.stack(a_l), v_new)
        s_new = st * jnp.stack(eg_l) + _bmm(jnp.stack(kd_l), v_new)
        for n_, (o_ref, b, rs, hs, s) in enumerate(dst):
            o_ref[b, rs, hs] = o[n_].astype(BF16)
            s_ref[b, s] = s_new[n_]
        return carry

    lax.fori_loop(0, nc, chunk_step, 0)

    @pl.when(i == pl.num_programs(0) - 1)
    def _():
        sfin_ref[...] = s_ref[...]


def _delta_scan(u, w, qg, a, kdt, eg, s0, *, ts):
    b, t, w2 = u.shape
    wd = w2 // N_DIR
    n = t // ts
    nc = ts // DN_CHUNK
    fwd = pl.BlockSpec((b, ts, wd), lambda i: (0, i, 0))
    rev = pl.BlockSpec((b, ts, wd), lambda i: (0, n - 1 - i, 1))
    kd_f = pl.BlockSpec((b, nc, DN_HEADS) + kdt.shape[3:], lambda i: (0, i, 0, 0, 0))
    kd_r = pl.BlockSpec((b, nc, DN_HEADS) + kdt.shape[3:], lambda i: (0, n - 1 - i, 1, 0, 0))
    eg_f = pl.BlockSpec((b, nc) + eg.shape[2:], lambda i: (0, i, 0, 0))
    eg_r = pl.BlockSpec((b, nc) + eg.shape[2:], lambda i: (0, n - 1 - i, 0, 0))
    out_f = pl.BlockSpec((b, ts, wd), lambda i: (0, i, 0))
    out_r = pl.BlockSpec((b, ts, wd), lambda i: (0, n - 1 - i, 0))
    return pl.pallas_call(
        functools.partial(_dscan_kernel, nc=nc, nb=b),
        grid=(n,),
        in_specs=[fwd, fwd, fwd, fwd, kd_f, eg_f, rev, rev, rev, rev, kd_r, eg_r, _resident(s0.shape)],
        out_specs=[out_f, out_r, pl.BlockSpec(s0.shape, lambda i: (0, 0, 0, 0))],
        out_shape=[jax.ShapeDtypeStruct((b, t, wd), BF16)] * 2 + [jax.ShapeDtypeStruct(s0.shape, F32)],
        scratch_shapes=[pltpu.VMEM(s0.shape, F32)],
        compiler_params=_params(("arbitrary",)),
        name="delta_scan",
    )(u, w, qg, a, kdt, eg, u, w, qg, a, kdt, eg, s0)


S5_POW_ROWS = 24


def _cexp(re, im):
    m = jnp.exp(re)
    return m * jnp.cos(im), m * jnp.sin(im)


def _s5_op_kernel(are_ref, aim_ref, ls_ref, btr_ref, bti_ref, ctr_ref, cti_ref,
                  win_ref, wm_ref, kbd_ref, lam_ref):
    l, p, n = S5_CHUNK, S5_GROUP, S5_STATE
    w = S5_BLK * n
    mask_w = (lax.broadcasted_iota(jnp.int32, (LANES, w), 0) // p
              == lax.broadcasted_iota(jnp.int32, (LANES, w), 1) // n)
    mask_c = (lax.broadcasted_iota(jnp.int32, (w, LANES), 0) // n
              == lax.broadcasted_iota(jnp.int32, (w, LANES), 1) // p)
    for r in range(N_DIR):
        are, aim = are_ref[r, 0], aim_ref[r, 0]
        dt = jnp.exp(ls_ref[r, 0])
        kk = lax.broadcasted_iota(jnp.int32, (S5_POW_ROWS, w), 0).astype(F32)
        pr, pi = _cexp(kk * (are * dt), kk * (aim * dt))
        lbr, lbi = pr[1:2], pi[1:2]
        den = are * are + aim * aim
        nr, ni = lbr - 1.0, lbi
        cr = (nr * are + ni * aim) / den
        ci = (ni * are - nr * aim) / den
        bre, bim = btr_ref[r, 0], bti_ref[r, 0]
        bmr = jnp.where(mask_w, jnp.concatenate([cr * bre - ci * bim] * S5_BLK, axis=0), 0.0)
        bmi = jnp.where(mask_w, jnp.concatenate([cr * bim + ci * bre] * S5_BLK, axis=0), 0.0)
        wr_l, wi_l = [], []
        for k in range(l):
            wr = bmr * pr[k:k + 1] - bmi * pi[k:k + 1]
            wi = bmr * pi[k:k + 1] + bmi * pr[k:k + 1]
            s = k if r else l - 1 - k
            win_ref[0, s * LANES:(s + 1) * LANES, (2 * r) * w:(2 * r + 1) * w] = wr.astype(BF16)
            win_ref[0, s * LANES:(s + 1) * LANES, (2 * r + 1) * w:(2 * r + 2) * w] = wi.astype(BF16)
            wr_l.append(wr)
            wi_l.append(wi)
        ctr = jnp.where(mask_c, ctr_ref[r, 0], 0.0)
        cti = jnp.where(mask_c, cti_ref[r, 0], 0.0)
        kbd_ref[0, r] = (_dot_f32(jnp.concatenate(wr_l, axis=0), ctr)
                         - _dot_f32(jnp.concatenate(wi_l, axis=0), cti))
        zpad = jnp.zeros((LANES - S5_POW_ROWS, w), F32)
        ptr = jnp.concatenate([pr, zpad], axis=0).T
        pti = jnp.concatenate([pi, zpad], axis=0).T
        for s in range(l):
            e = l - s if r else s + 1
            er, ei = ptr[:, e:e + 1], pti[:, e:e + 1]
            wm_ref[0, (2 * r) * w:(2 * r + 1) * w, s * LANES:(s + 1) * LANES] = (ctr * er - cti * ei).astype(BF16)
            wm_ref[0, (2 * r + 1) * w:(2 * r + 2) * w, s * LANES:(s + 1) * LANES] = (
                -(ctr * ei + cti * er)).astype(BF16)
        lam_ref[0, r:r + 1, :] = pr[l:l + 1]
        lam_ref[1, r:r + 1, :] = pi[l:l + 1]


S5_TOEP_ROWS = 4


def _s5_toep_kernel(kbd_ref, wt_ref):
    l = S5_CHUNK
    k0 = kbd_ref[0, 0, 0:LANES, :] + kbd_ref[0, 1, 0:LANES, :]
    for i in range(S5_TOEP_ROWS):
        sp = pl.program_id(1) * S5_TOEP_ROWS + i
        for s in range(l):
            d = s - sp
            kf = kbd_ref[0, 0, pl.ds(pl.multiple_of(jnp.maximum(d, 0) * LANES, LANES), LANES), :]
            kr = kbd_ref[0, 1, pl.ds(pl.multiple_of(jnp.maximum(-d, 0) * LANES, LANES), LANES), :]
            blk = jnp.where(d > 0, kf, jnp.where(d < 0, kr, k0))
            wt_ref[0, i * LANES:(i + 1) * LANES, s * LANES:(s + 1) * LANES] = blk.astype(BF16)


def _s5_operators(a_re, a_im, log_step, b_re, b_im, c_re, c_im):
    g, n, p, l = S5_GROUPS, S5_STATE, S5_GROUP, S5_CHUNK
    nb, g8 = S5_NBLK, S5_BLK
    w = g8 * n
    lane_row = lambda a: a.reshape(N_DIR, nb, 1, w)
    ls = lane_row(jnp.repeat(log_step, n, axis=1))
    bt = lambda b: jnp.transpose(b.reshape(N_DIR, nb, g8, n, p), (0, 1, 4, 2, 3)).reshape(N_DIR, nb, p, w)
    ct = lambda c: jnp.tile(jnp.swapaxes(c, 2, 3).reshape(N_DIR, nb, w, p), (1, 1, 1, g8))
    blk = lambda r, c: pl.BlockSpec((N_DIR, 1, r, c), lambda j: (0, j, 0, 0))
    ops = lambda r, c: pl.BlockSpec((1, r, c), lambda j: (j, 0, 0))
    lw = l * LANES
    w_in, w_m, kbd, lam = pl.pallas_call(
        _s5_op_kernel,
        grid=(nb,),
        in_specs=[blk(1, w)] * 3 + [blk(p, w)] * 2 + [blk(w, LANES)] * 2,
        out_specs=[ops(lw, 2 * N_DIR * w), ops(2 * N_DIR * w, lw),
                   pl.BlockSpec((1, N_DIR, lw, LANES), lambda j: (j, 0, 0, 0)),
                   pl.BlockSpec((2, N_DIR, w), lambda j: (0, 0, j))],
        out_shape=[jax.ShapeDtypeStruct((nb, lw, 2 * N_DIR * w), BF16),
                   jax.ShapeDtypeStruct((nb, 2 * N_DIR * w, lw), BF16),
                   jax.ShapeDtypeStruct((nb, N_DIR, lw, LANES), F32),
                   jax.ShapeDtypeStruct((2, N_DIR, g * n), F32)],
        compiler_params=_params(("parallel",)),
        name="s5_params",
    )(lane_row(a_re), lane_row(a_im), ls, bt(b_re), bt(b_im), ct(c_re), ct(c_im))
    w_t = pl.pallas_call(
        _s5_toep_kernel,
        grid=(nb, l // S5_TOEP_ROWS),
        in_specs=[pl.BlockSpec((1, N_DIR, lw, LANES), lambda j, i: (j, 0, 0, 0))],
        out_specs=pl.BlockSpec((1, S5_TOEP_ROWS * LANES, lw), lambda j, i: (j, i, 0)),
        out_shape=jax.ShapeDtypeStruct((nb, lw, lw), BF16),
        compiler_params=_params(("parallel", "parallel")),
        name="s5_toeplitz",
    )(kbd)
    return w_in, w_t, w_m, lam


def _chunk_rows(u_ref, nct):
    return jnp.concatenate([u_ref[0, pl.ds(s, nct, stride=S5_CHUNK), :] for s in range(S5_CHUNK)],
                           axis=1).astype(BF16)


def _s5_in_kernel(uc_ref, ul_ref, w_ref, *z_refs, ncc, ncl):
    v = jnp.concatenate([_chunk_rows(uc_ref, ncc), _chunk_rows(ul_ref, ncl)], axis=0)
    z = _dot(v, w_ref[0])
    wd = z_refs[0].shape[-1]
    for k in range(4):
        z_refs[k][...] = z[:ncc, k * wd:(k + 1) * wd]
        z_refs[4 + k][...] = z[ncc:, k * wd:(k + 1) * wd]


def _s5_chunk_in(u_c, u_l, w_in):
    b, tc, _ = u_c.shape
    t = u_l.shape[1]
    ncc, ncl = tc // S5_CHUNK, t // S5_CHUNK
    wd = S5_BLK * S5_STATE
    out = lambda rows: pl.BlockSpec((rows, wd), lambda j, i: (i, j))
    sds = lambda rows: jax.ShapeDtypeStruct((b * rows, S5_NBLK * wd), F32)
    outs = pl.pallas_call(
        functools.partial(_s5_in_kernel, ncc=ncc, ncl=ncl),
        grid=(S5_NBLK, b),
        in_specs=[pl.BlockSpec((1, tc, LANES), lambda j, i: (i, 0, j)),
                  pl.BlockSpec((1, t, LANES), lambda j, i: (i, 0, j)),
                  pl.BlockSpec((1,) + w_in.shape[1:], lambda j, i: (j, 0, 0))],
        out_specs=[out(ncc)] * 4 + [out(ncl)] * 4,
        out_shape=[sds(ncc)] * 4 + [sds(ncl)] * 4,
        compiler_params=_params(("parallel", "parallel")),
        name="s5_chunk_in",
    )(u_c, u_l, w_in)
    return list(outs[:4]), list(outs[4:])


SUBLANES = 8


def _cmul(ar, ai, br, bi):
    return ar * br - ai * bi, ar * bi + ai * br


def _s5_scan_tables(lr, li, row, rev):
    pw = [(lr, li)]
    for _ in range(SUBLANES - 1):
        pw.append(_cmul(pw[-1][0], pw[-1][1], lr, li))

    def by_row(power_of):
        tr, ti = jnp.zeros(row.shape, F32), jnp.zeros(row.shape, F32)
        for i in range(SUBLANES):
            k = power_of(i)
            if k:
                tr, ti = jnp.where(row == i, pw[k - 1][0], tr), jnp.where(row == i, pw[k - 1][1], ti)
        return tr, ti

    carry_w = by_row(lambda i: SUBLANES - i if rev else i + 1)
    steps = [by_row(lambda i, d=d: d if ((i <= SUBLANES - 1 - d) if rev else (i >= d)) else 0)
             for d in (1, 2, 4)]
    return carry_w, steps


def _s5_scan_tile(zr, zi, cr, ci, carry_w, steps, row, rev):
    yr, yi = zr, zi
    for d, (tr, ti) in zip((1, 2, 4), steps):
        sh = SUBLANES - d if rev else d
        ar, ai = _cmul(tr, ti, pltpu.roll(yr, sh, 0), pltpu.roll(yi, sh, 0))
        yr, yi = yr + ar, yi + ai
    ar, ai = _cmul(carry_w[0], carry_w[1], cr, ci)
    xr, xi = yr + ar, yi + ai
    sh, edge, last = (SUBLANES - 1, SUBLANES - 1, 0) if rev else (1, 0, SUBLANES - 1)
    er = jnp.where(row == edge, cr, pltpu.roll(xr, sh, 0))
    ei = jnp.where(row == edge, ci, pltpu.roll(xi, sh, 0))
    return er, ei, xr[last:last + 1], xi[last:last + 1]


def _s5_scan_kernel(cfr_ref, cfi_ref, crr_ref, cri_ref, zfr_ref, zfi_ref, zrr_ref, zri_ref, lam_ref,
                    xfr_ref, xfi_ref, xrr_ref, xri_ref, *, ncc, ncl, nb):
    tl = zfr_ref.shape[-1]
    row = lax.broadcasted_iota(jnp.int32, (SUBLANES, tl), 0)
    cw_f, st_f = _s5_scan_tables(lam_ref[0, 0:1, :], lam_ref[1, 0:1, :], row, False)
    cw_r, st_r = _s5_scan_tables(lam_ref[0, 1:2, :], lam_ref[1, 1:2, :], row, True)

    def make_step(zf, zr, nc, outs):
        nt = nc // SUBLANES

        def step(t, carry):
            new = []
            for b in range(nb):
                cfr, cfi, crr, cri = carry[b]
                rows = pl.ds(pl.multiple_of(b * nc + t * SUBLANES, SUBLANES), SUBLANES)
                er, ei, cfr, cfi = _s5_scan_tile(zf[0][rows, :], zf[1][rows, :], cfr, cfi, cw_f, st_f, row, False)
                if outs is not None:
                    outs[0][rows, :], outs[1][rows, :] = er, ei
                rows = pl.ds(pl.multiple_of(b * nc + (nt - 1 - t) * SUBLANES, SUBLANES), SUBLANES)
                er, ei, crr, cri = _s5_scan_tile(zr[0][rows, :], zr[1][rows, :], crr, cri, cw_r, st_r, row, True)
                if outs is not None:
                    outs[2][rows, :], outs[3][rows, :] = er, ei
                new.append((cfr, cfi, crr, cri))
            return tuple(new)

        return nt, step

    zero = jnp.zeros((1, tl), F32)
    carry = tuple((zero, zero, zero, zero) for _ in range(nb))
    nt, step = make_step((cfr_ref, cfi_ref), (crr_ref, cri_ref), ncc, None)
    carry = lax.fori_loop(0, nt, step, carry)
    nt, step = make_step((zfr_ref, zfi_ref), (zrr_ref, zri_ref), ncl, (xfr_ref, xfi_ref, xrr_ref, xri_ref))
    lax.fori_loop(0, nt, step, carry)


def _s5_state_scan(z_ctx, z_lat, lam, *, nb, tl):
    rc, w = z_ctx[0].shape
    rl = z_lat[0].shape[0]
    cb = pl.BlockSpec((rc, tl), lambda j: (0, j))
    zb = pl.BlockSpec((rl, tl), lambda j: (0, j))
    return pl.pallas_call(
        functools.partial(_s5_scan_kernel, ncc=rc // nb, ncl=rl // nb, nb=nb),
        grid=(w // tl,),
        in_specs=[cb] * 4 + [zb] * 4 + [pl.BlockSpec((2, N_DIR, tl), lambda j: (0, 0, j))],
        out_specs=[zb] * 4,
        out_shape=[jax.ShapeDtypeStruct((rl, w), F32)] * 4,
        compiler_params=_params(("parallel",)),
        name="s5_state_scan",
    )(*z_ctx, *z_lat, lam)


def _s5_out_kernel(u_ref, xfr_ref, xfi_ref, xrr_ref, xri_ref, wt_ref, wm_ref, y_ref, *, nct):
    x = jnp.concatenate([r[...].astype(BF16) for r in (xfr_ref, xfi_ref, xrr_ref, xri_ref)], axis=1)
    y = _dot(_chunk_rows(u_ref, nct), wt_ref[0]) + _dot(x, wm_ref[0])
    for s in range(S5_CHUNK):
        y_ref[0, pl.ds(s, nct, stride=S5_CHUNK), :] = y[:, s * LANES:(s + 1) * LANES]


def _s5_chunk_out(u, x4, w_t, w_m):
    b, t, _ = u.shape
    nct = t // S5_CHUNK
    wd = S5_BLK * S5_STATE
    xb = pl.BlockSpec((nct, wd), lambda j, i: (i, j))
    ub = pl.BlockSpec((1, t, LANES), lambda j, i: (i, 0, j))
    op = lambda a: pl.BlockSpec((1,) + a.shape[1:], lambda j, i: (j, 0, 0))
    return pl.pallas_call(
        functools.partial(_s5_out_kernel, nct=nct),
        grid=(S5_NBLK, b),
        in_specs=[ub] + [xb] * 4 + [op(w_t), op(w_m)],
        out_specs=ub,
        out_shape=jax.ShapeDtypeStruct(u.shape, F32),
        compiler_params=_params(("parallel", "parallel")),
        name="s5_chunk_out",
    )(u, *x4, w_t, w_m)


def _gelu_tanh(x):
    return 0.5 * x * (1.0 + jnp.tanh(0.7978845608028654 * (x + 0.044715 * x * x * x)))


def _merge_kernel(of_ref, or_ref, z_ref, ys_ref, u_ref, g_ref, x_ref, mod_ref,
                  dnw_ref, wa_ref, dsk_ref, wglu_ref, bglu_ref, wb_ref, wo_ref, n2w_ref,
                  xl_ref, h2_ref, *, d):
    o = of_ref[0].astype(F32) + or_ref[0].astype(F32)
    z = z_ref[0].astype(F32)
    heads = []
    for h in range(DN_HEADS):
        sl = slice(h * DN_HEAD_DIM, (h + 1) * DN_HEAD_DIM)
        oh = o[:, sl]
        on = oh * lax.rsqrt(jnp.mean(oh * oh, axis=-1, keepdims=True) + RMS_EPS) * dnw_ref[...]
        heads.append((on * _silu(z[:, sl])).astype(BF16))
    ya = _dot(jnp.concatenate(heads, axis=1), wa_ref[...])
    ys = ys_ref[0] + dsk_ref[...] * u_ref[0]
    zz = _dot(_gelu_tanh(ys).astype(BF16), wglu_ref[...]) + bglu_ref[...]
    yb = _dot((zz[:, :S5_WIDTH] * jax.nn.sigmoid(zz[:, S5_WIDTH:])).astype(BF16), wb_ref[...])
    gates = g_ref[0].astype(F32)
    mix = jax.nn.sigmoid(gates[:, :d]) * ya + jax.nn.sigmoid(gates[:, d:]) * yb
    xl = x_ref[0] + mod_ref[0, :, 2 * d:3 * d] * _dot(mix.astype(BF16), wo_ref[...])
    xl_ref[0] = xl
    hn = xl * lax.rsqrt(jnp.mean(xl * xl, axis=-1, keepdims=True) + RMS_EPS) * n2w_ref[...]
    h2_ref[0] = (hn * (1.0 + mod_ref[0, :, 4 * d:5 * d]) + mod_ref[0, :, 3 * d:4 * d]).astype(BF16)


def _mix_merge(o_f, o_r, z, ys, u, gates, x, mods, dn_norm_w, w_a_out, s5_d, w_glu, b_glu,
               w_b_out, w_o, norm2_w, *, tm):
    b, t, d = x.shape
    tok = lambda n: pl.BlockSpec((1, tm, n), lambda i, j: (i, j, 0))
    consts = [dn_norm_w.reshape(1, -1), w_a_out, s5_d.reshape(1, -1), w_glu, b_glu.reshape(1, -1),
              w_b_out, w_o, norm2_w.reshape(1, -1)]
    return pl.pallas_call(
        functools.partial(_merge_kernel, d=d),
        grid=(b, t // tm),
        in_specs=[tok(o_f.shape[-1]), tok(o_r.shape[-1]), tok(z.shape[-1]), tok(ys.shape[-1]),
                  tok(u.shape[-1]), tok(gates.shape[-1]), tok(d),
                  pl.BlockSpec((1, 1, mods.shape[-1]), lambda i, j: (i, 0, 0))]
        + [_resident(c.shape) for c in consts],
        out_specs=[tok(d), tok(d)],
        out_shape=[jax.ShapeDtypeStruct((b, t, d), F32), jax.ShapeDtypeStruct((b, t, d), BF16)],
        compiler_params=_params(("parallel", "parallel")),
        name="mix_merge",
    )(o_f, o_r, z, ys, u, gates, x, mods, *consts)


FFN_ROWS = 8
FFN_CB = 256


FFN_DOWN_GROUP = 4
FFN_AHEAD = 2


def _conv_row(e_ref, slot, part, r, cw):
    rows = [e_ref[slot, part, (r + i) * GRID_W:(r + i + 1) * GRID_W, :] for i in range(3)]
    taps = [rows[0] * cw[j:j + 1, :] + rows[1] * cw[3 + j:4 + j, :] + rows[2] * cw[6 + j:7 + j, :]
            for j in range(3)]
    left = pltpu.roll(taps[0], 1, 0)
    right = pltpu.roll(taps[2], GRID_W - 1, 0)
    sub = lax.broadcasted_iota(jnp.int32, (SUBLANES, left.shape[1]), 0)
    left = jnp.concatenate([jnp.where(sub == 0, 0.0, left[:SUBLANES]), left[SUBLANES:]], axis=0)
    right = jnp.concatenate([right[:-SUBLANES], jnp.where(sub == SUBLANES - 1, 0.0, right[-SUBLANES:])], axis=0)
    return left + taps[1] + right


def _ffn_kernel(h_ref, hp_ref, hn_ref, xl_ref, mod_ref, wup_ref, cw_ref, wd_ref, nfw_ref,
                o_ref, hext_ref, e_ref, act_ref, acc_ref, *, d, dff, nt):
    t = pl.program_id(1)
    n_out = h_ref.shape[1]
    hext_ref[0:GRID_W] = jnp.where(t == 0, jnp.zeros_like(hp_ref[0]), hp_ref[0])
    hext_ref[GRID_W:GRID_W + n_out] = h_ref[0]
    hext_ref[GRID_W + n_out:] = jnp.where(t == nt - 1, jnp.zeros_like(hn_ref[0]), hn_ref[0])
    ncb = dff // FFN_CB

    def up(k, slot):
        e_ref[slot, 0] = _dot(hext_ref[...], wup_ref[:, k * FFN_CB:(k + 1) * FFN_CB])
        e_ref[slot, 1] = _dot(hext_ref[...], wup_ref[:, dff + k * FFN_CB:dff + (k + 1) * FFN_CB])

    for k in range(FFN_AHEAD):
        up(k, k)
    done = 0
    for k in range(ncb):
        slot = k % (FFN_AHEAD + 1)
        if k + FFN_AHEAD < ncb:
            up(k + FFN_AHEAD, (k + FFN_AHEAD) % (FFN_AHEAD + 1))
        gs = slice(k * FFN_CB, (k + 1) * FFN_CB)
        cg = cw_ref[:, gs]
        cv = cw_ref[:, dff + k * FFN_CB:dff + (k + 1) * FFN_CB]
        for r in range(n_out // GRID_W):
            gate = _conv_row(e_ref, slot, 0, r, cg)
            val = _conv_row(e_ref, slot, 1, r, cv)
            act_ref[r * GRID_W:(r + 1) * GRID_W, gs] = (_silu(gate) * val).astype(BF16)
        if (k + 1) % FFN_DOWN_GROUP == 0 or k == ncb - 1:
            ks = slice(done * FFN_CB, (k + 1) * FFN_CB)
            part = _dot(act_ref[:, ks], wd_ref[ks, :])
            if done == 0:
                acc_ref[...] = part
            else:
                acc_ref[...] += part
            done = k + 1
    xo = xl_ref[0] + mod_ref[0, :, 5 * d:6 * d] * acc_ref[...]
    o_ref[0] = xo * lax.rsqrt(jnp.mean(xo * xo, axis=-1, keepdims=True) + RMS_EPS) * nfw_ref[...]


def _conv_ffn(h2, xl, mods, w_up, conv_w, w_down, norm_f_w):
    b, t, d = xl.shape
    dff = w_down.shape[0]
    tm = FFN_ROWS * GRID_W
    nt = t // tm
    nrow = t // GRID_W
    cw = conv_w.reshape(9, 2 * dff)
    tok = lambda: pl.BlockSpec((1, tm, d), lambda i, j: (i, j, 0))
    return pl.pallas_call(
        functools.partial(_ffn_kernel, d=d, dff=dff, nt=nt),
        grid=(b, nt),
        in_specs=[tok(),
                  pl.BlockSpec((1, GRID_W, d), lambda i, j: (i, jnp.maximum(j * FFN_ROWS - 1, 0), 0)),
                  pl.BlockSpec((1, GRID_W, d), lambda i, j: (i, jnp.minimum((j + 1) * FFN_ROWS, nrow - 1), 0)),
                  tok(),
                  pl.BlockSpec((1, 1, mods.shape[-1]), lambda i, j: (i, 0, 0)),
                  _resident(w_up.shape), _resident(cw.shape), _resident(w_down.shape), _resident((1, d))],
        out_specs=tok(),
        out_shape=jax.ShapeDtypeStruct((b, t, d), F32),
        scratch_shapes=[pltpu.VMEM((tm + 2 * GRID_W, d), BF16),
                        pltpu.VMEM((FFN_AHEAD + 1, 2, tm + 2 * GRID_W, FFN_CB), F32),
                        pltpu.VMEM((tm, dff), BF16),
                        pltpu.VMEM((tm, d), F32)],
        compiler_params=_params(("parallel", "parallel")),
        name="conv_ffn",
    )(h2, h2, h2, xl, mods, w_up.astype(BF16), cw, w_down.astype(BF16), norm_f_w.reshape(1, d))


def _lane_row(a):
    a = a.reshape(-1).astype(F32)
    return jnp.pad(a, (0, LANES - a.shape[0])).reshape(1, LANES)


def _pad_cols(w):
    return jnp.pad(w, ((0, 0), (0, LANES - w.shape[1])))


def kernel(x, c, ctx, c_ctx, w_ada, b_ada, norm1_w, w_in, dn_conv_w, dn_a_log, dn_dt_bias, dn_norm_w,
           w_a_out, s5_a_re, s5_a_im, s5_log_step, s5_b_re, s5_b_im, s5_c_re, s5_c_im, s5_d, w_glu,
           b_glu, w_b_out, w_o, norm2_w, w_up, ffn_conv_w, w_down, norm_f_w):
    assert w_ada.shape[0] == 1, "single-layer block"
    b, t, d = x.shape
    tc = ctx.shape[1]
    nh = N_DIR * DN_HEADS
    assert b < 8, "batch rows and the context row share one 8-row modulation block"

    c_rows = jnp.zeros((8, d), F32).at[:b].set(c).at[b].set(c_ctx)
    mods = _modulation(c_rows, w_ada[0], b_ada[0]).reshape(8, 1, N_MOD * d)

    w = w_in[0]
    o_z, o_b, o_a = 3 * DN_WIDTH, 4 * DN_WIDTH, 4 * DN_WIDTH + nh
    o_u = o_a + nh
    o_g = o_u + S5_WIDTH
    wqkv = w[:, :o_z].astype(BF16)
    wz = w[:, o_z:o_b].astype(BF16)
    wba = jnp.concatenate([_pad_cols(w[:, o_b:o_a]), _pad_cols(w[:, o_a:o_u])], axis=1).astype(BF16)
    wu = w[:, o_u:o_g].astype(BF16)
    wg = w[:, o_g:].astype(BF16)

    qkv_l, ba_l, u_l, z_l, gates_l = _in_proj(x, mods, lambda i: i, norm1_w[0], wqkv, wba, wu, wz, wg, tm=512)
    qkv_c, ba_c, u_c = _in_proj(ctx, mods, lambda i: b, norm1_w[0], wqkv, wba, wu, tm=tc)

    alog_row, dtb_row = _lane_row(dn_a_log[0]), _lane_row(dn_dt_bias[0])
    prep_c = _delta_prep(qkv_c, ba_c, dn_conv_w[0], alog_row, dtb_row, tm=tc)
    prep_l = _delta_prep(qkv_l, ba_l, dn_conv_w[0], alog_row, dtb_row, tm=512)
    s0 = jnp.zeros((b, nh, DN_HEAD_DIM, DN_HEAD_DIM), F32)
    _, _, s_ctx = _delta_scan(*_delta_wy(*prep_c, ts=tc), s0, ts=128)
    o_f, o_r, _ = _delta_scan(*_delta_wy(*prep_l, ts=256), s_ctx, ts=128)

    w_s5in, w_s5t, w_s5m, lam = _s5_operators(s5_a_re[0], s5_a_im[0], s5_log_step[0], s5_b_re[0],
                                              s5_b_im[0], s5_c_re[0], s5_c_im[0])
    zs_c, zs_l = _s5_chunk_in(u_c, u_l, w_s5in)
    x_l = _s5_state_scan(zs_c, zs_l, lam, nb=b, tl=256)
    y_s5 = _s5_chunk_out(u_l, x_l, w_s5t, w_s5m)

    xl, h2 = _mix_merge(o_f, o_r, z_l, y_s5, u_l, gates_l, x, mods, dn_norm_w[0], w_a_out[0].astype(BF16),
                        s5_d[0], w_glu[0].astype(BF16), b_glu[0], w_b_out[0].astype(BF16),
                        w_o[0].astype(BF16), norm2_w[0], tm=512)
    return _conv_ffn(h2, xl, mods, w_up[0], ffn_conv_w[0], w_down[0], norm_f_w)
```

```python
import functools

import jax
import jax.numpy as jnp
from jax import lax
from jax.experimental import pallas as pl
from jax.experimental.pallas import tpu as pltpu

F32 = jnp.float32
BF16 = jnp.bfloat16

GRID_W = 64
N_DIR = 2
DN_HEADS = 4
DN_HEAD_DIM = 128
DN_WIDTH = DN_HEADS * DN_HEAD_DIM
DN_CHUNK = 64
S5_WIDTH = 512
S5_GROUP = 16
S5_GROUPS = S5_WIDTH // S5_GROUP
S5_STATE = 64
S5_CHUNK = 16
N_MOD = 6
RMS_EPS = 1e-6
L2_EPS = 1e-6
LANES = 128
S5_BLK = LANES // S5_GROUP
S5_NBLK = S5_GROUPS // S5_BLK
VMEM_LIMIT = 56 * 1024 * 1024


def _dot(a, b):
    return jnp.dot(a, b, preferred_element_type=F32)


def _dot_f32(a, b):
    return jnp.dot(a, b, preferred_element_type=F32, precision=lax.Precision.HIGHEST)


def _dot_nt_f32(a, b):
    return lax.dot_general(a, b, (((1,), (1,)), ((), ())), preferred_element_type=F32,
                           precision=lax.Precision.HIGHEST)


def _silu(x):
    return x * jax.nn.sigmoid(x)


def _softplus(x):
    return jnp.maximum(x, 0.0) + jnp.log(1.0 + jnp.exp(-jnp.abs(x)))


def _params(sem, vmem=VMEM_LIMIT, flags=None):
    return pltpu.CompilerParams(dimension_semantics=sem, vmem_limit_bytes=vmem, flags=flags)


def _resident(shape):
    nd = len(shape)
    return pl.BlockSpec(shape, lambda *_: (0,) * nd, pipeline_mode=pl.Buffered(1))


def _mod_kernel(c_ref, w_ref, b_ref, o_ref):
    sc = _silu(c_ref[...])
    o_ref[...] = _dot(sc.astype(BF16), w_ref[...].astype(BF16)) + b_ref[...]


def _modulation(c_rows, w_ada, b_ada):
    d, n = w_ada.shape
    tn = n // 4
    return pl.pallas_call(
        _mod_kernel,
        grid=(n // tn,),
        in_specs=[pl.BlockSpec(c_rows.shape, lambda j: (0, 0)),
                  pl.BlockSpec((d, tn), lambda j: (0, j)),
                  pl.BlockSpec((1, tn), lambda j: (0, j))],
        out_specs=pl.BlockSpec((c_rows.shape[0], tn), lambda j: (0, j)),
        out_shape=jax.ShapeDtypeStruct((c_rows.shape[0], n), F32),
        compiler_params=_params(("arbitrary",)),
        name="adaln_mod",
    )(c_rows, w_ada, b_ada.reshape(1, n))


def _inproj_kernel(x_ref, mod_ref, nw_ref, wqkv_ref, wba_ref, wu_ref, wz_ref, wg_ref,
                   qkv_ref, ba_ref, u_ref, z_ref, g_ref, *, d):
    x = x_ref[0]
    ms = jnp.mean(x * x, axis=-1, keepdims=True)
    h = x * lax.rsqrt(ms + RMS_EPS) * nw_ref[...]
    shift = mod_ref[0, :, 0:d]
    scale = mod_ref[0, :, d:2 * d]
    hb = (h * (1.0 + scale) + shift).astype(BF16)
    qkv_ref[0] = _dot(hb, wqkv_ref[...]).astype(BF16)
    ba_ref[0] = _dot(hb, wba_ref[...])
    u_ref[0] = _dot(hb, wu_ref[...])
    if z_ref is not None:
        z_ref[0] = _dot(hb, wz_ref[...]).astype(BF16)
        g_ref[0] = _dot(hb, wg_ref[...]).astype(BF16)


def _inproj_ctx_kernel(x_ref, mod_ref, nw_ref, wqkv_ref, wba_ref, wu_ref,
                       qkv_ref, ba_ref, u_ref, *, d):
    _inproj_kernel(x_ref, mod_ref, nw_ref, wqkv_ref, wba_ref, wu_ref, None, None,
                   qkv_ref, ba_ref, u_ref, None, None, d=d)


def _in_proj(x, mods, mod_row0, norm_w, wqkv, wba, wu, wz=None, wg=None, *, tm):
    b, t, d = x.shape
    full = wz is not None
    tok = lambda n: pl.BlockSpec((1, tm, n), lambda i, j: (i, j, 0))
    in_specs = [tok(d),
                pl.BlockSpec((1, 1, mods.shape[-1]), lambda i, j: (mod_row0(i), 0, 0)),
                _resident((1, d)), _resident(wqkv.shape), _resident(wba.shape),
                _resident(wu.shape)]
    args = [x, mods, norm_w.reshape(1, d), wqkv, wba, wu]
    out_specs = [tok(wqkv.shape[1]), tok(wba.shape[1]), tok(wu.shape[1])]
    out_shape = [jax.ShapeDtypeStruct((b, t, wqkv.shape[1]), BF16),
                 jax.ShapeDtypeStruct((b, t, wba.shape[1]), F32),
                 jax.ShapeDtypeStruct((b, t, wu.shape[1]), F32)]
    if full:
        in_specs += [_resident(wz.shape), _resident(wg.shape)]
        args += [wz, wg]
        out_specs += [tok(wz.shape[1]), tok(wg.shape[1])]
        out_shape += [jax.ShapeDtypeStruct((b, t, wz.shape[1]), BF16),
                      jax.ShapeDtypeStruct((b, t, wg.shape[1]), BF16)]
    body = functools.partial(_inproj_kernel if full else _inproj_ctx_kernel, d=d)
    return pl.pallas_call(
        body, grid=(b, t // tm), in_specs=in_specs, out_specs=out_specs, out_shape=out_shape,
        compiler_params=_params(("parallel", "parallel")),
        name="in_proj" if full else "in_proj_ctx",
    )(*args)


HALO = 16


def _dprep_kernel(x_ref, xp_ref, xn_ref, ba_ref, cw_ref, alog_ref, dtb_ref,
                  q_ref, k_ref, v_ref, beta_ref, g_ref, *, tm, nt):
    t = pl.program_id(1)
    x = x_ref[0].astype(F32)
    prow = jnp.where(t == 0, 0.0, xp_ref[0, HALO - 1:HALO, :].astype(F32))
    nrow = jnp.where(t == nt - 1, 0.0, xn_ref[0, 0:1, :].astype(F32))
    rows = lax.broadcasted_iota(jnp.int32, x.shape, 0)
    xprev = jnp.where(rows == 0, prow, pltpu.roll(x, 1, 0))
    xnext = jnp.where(rows == tm - 1, nrow, pltpu.roll(x, tm - 1, 0))
    y = _silu(xprev * cw_ref[0:1, :] + x * cw_ref[1:2, :] + xnext * cw_ref[2:3, :])
    for h in range(DN_HEADS):
        sl = slice(h * DN_HEAD_DIM, (h + 1) * DN_HEAD_DIM)
        qh = y[:, h * DN_HEAD_DIM:(h + 1) * DN_HEAD_DIM]
        kh = y[:, DN_WIDTH + h * DN_HEAD_DIM:DN_WIDTH + (h + 1) * DN_HEAD_DIM]
        qn = qh * lax.rsqrt(jnp.sum(qh * qh, axis=-1, keepdims=True) + L2_EPS)
        kn = kh * lax.rsqrt(jnp.sum(kh * kh, axis=-1, keepdims=True) + L2_EPS)
        q_ref[0, :, sl] = (qn * (DN_HEAD_DIM ** -0.5)).astype(BF16)
        k_ref[0, :, sl] = kn.astype(BF16)
    v_ref[0] = y[:, 2 * DN_WIDTH:3 * DN_WIDTH].astype(BF16)
    ba = ba_ref[0]
    beta_ref[0] = jax.nn.sigmoid(ba[:, 0:LANES])
    g_ref[0] = -jnp.exp(alog_ref[...]) * _softplus(ba[:, LANES:2 * LANES] + dtb_ref[...])


def _delta_prep(qkv, ba, conv_w, alog_row, dtb_row, *, tm):
    b, t, c = qkv.shape
    nt = t // tm
    r = tm // HALO
    tok = lambda n: pl.BlockSpec((1, tm, n), lambda i, j: (i, j, 0))
    return pl.pallas_call(
        functools.partial(_dprep_kernel, tm=tm, nt=nt),
        grid=(b, nt),
        in_specs=[tok(c),
                  pl.BlockSpec((1, HALO, c), lambda i, j: (i, jnp.maximum(j * r - 1, 0), 0)),
                  pl.BlockSpec((1, HALO, c), lambda i, j: (i, jnp.minimum((j + 1) * r, t // HALO - 1), 0)),
                  tok(ba.shape[-1]),
                  _resident(conv_w.shape), _resident(alog_row.shape), _resident(dtb_row.shape)],
        out_specs=[tok(DN_WIDTH), tok(DN_WIDTH), tok(DN_WIDTH), tok(LANES), tok(LANES)],
        out_shape=[jax.ShapeDtypeStruct((b, t, DN_WIDTH), BF16)] * 3
        + [jax.ShapeDtypeStruct((b, t, LANES), F32)] * 2,
        compiler_params=_params(("parallel", "parallel")),
        name="delta_prep",
    )(qkv, qkv, qkv, ba, conv_w, alog_row, dtb_row)


def _bmm(a, b):
    return jnp.einsum('nik,nkj->nij', a, b, preferred_element_type=F32)


def _bmm_nt(a, b):
    return jnp.einsum('nik,njk->nij', a, b, preferred_element_type=F32)


def _dwy_kernel(q_ref, k_ref, v_ref, b_ref, g_ref, u_ref, w_ref, qg_ref, a_ref, kdt_ref, eg_ref, *, nc):
    c, dk = DN_CHUNK, DN_HEAD_DIM
    ts = nc * c
    ii = lax.broadcasted_iota(jnp.int32, (c, c), 0)
    jj = lax.broadcasted_iota(jnp.int32, (c, c), 1)
    ti = lax.broadcasted_iota(jnp.int32, (ts, ts), 0)
    tj = lax.broadcasted_iota(jnp.int32, (ts, ts), 1)
    same_chunk = (ti // c) == (tj // c)
    blocks = [(ch, h) for ch in range(nc) for h in range(DN_HEADS)]
    tile = lambda ref, ch, h: ref[0, ch * c:(ch + 1) * c, h * dk:(h + 1) * dk]
    k_l = [tile(k_ref, ch, h) for ch, h in blocks]
    q_l = [tile(q_ref, ch, h) for ch, h in blocks]
    v_l = [tile(v_ref, ch, h) for ch, h in blocks]
    kkqk = _bmm_nt(jnp.stack([jnp.concatenate([k_, q_], axis=0) for k_, q_ in zip(k_l, q_l)]),
                   jnp.stack(k_l))
    beta = b_ref[0]
    g = g_ref[0]
    a_pad = jnp.zeros((c, dk - c), BF16)
    neg_l, rhs_l, kd_l, dst = [], [], [], []
    for r in range(N_DIR):
        incl = ii <= jj if r else ii >= jj
        strict = ii < jj if r else ii > jj
        tri = (same_chunk & (ti <= tj if r else ti >= tj)).astype(F32)
        g_cum = _dot_f32(tri, g)
        g_cum_t = g_cum.T
        for n_, (ch, h) in enumerate(blocks):
            s = r * DN_HEADS + h
            rs = slice(ch * c, (ch + 1) * c)
            ls = slice(s * dk, (s + 1) * dk)
            g_c = g_cum[rs, s:s + 1]
            g_r = g_cum_t[s:s + 1, rs]
            g_end = g_c[0:1] if r else g_c[c - 1:c]
            b_c = beta[rs, s:s + 1]
            decay = jnp.where(incl, jnp.exp(jnp.where(incl, g_c - g_r, 0.0)), 0.0)
            neg_l.append(jnp.where(strict, kkqk[n_, :c] * (-b_c) * decay, 0.0))
            eg = jnp.exp(g_c)
            kf = k_l[n_].astype(F32)
            rhs_l.append(jnp.concatenate([(v_l[n_].astype(F32) * b_c).astype(BF16),
                                          (kf * (b_c * eg)).astype(BF16)], axis=1))
            qg_ref[0, rs, ls] = (q_l[n_].astype(F32) * eg).astype(BF16)
            a_ref[0, rs, ls] = jnp.concatenate([(kkqk[n_, c:] * decay).astype(BF16), a_pad], axis=1)
            kd_l.append((kf * jnp.exp(g_end - g_c)).astype(BF16))
            eg_ref[0, ch, s:s + 1, :] = jnp.broadcast_to(jnp.exp(g_end), (1, LANES))
            dst.append((rs, ls, ch, s))
    eye_k = (lax.broadcasted_iota(jnp.int32, (dk, dk), 0)
             == lax.broadcasted_iota(jnp.int32, (dk, dk), 1)).astype(BF16)
    kdt = _bmm_nt(jnp.broadcast_to(eye_k, (len(kd_l), dk, dk)), jnp.stack(kd_l))
    for n_, (_, _, ch, s) in enumerate(dst):
        kdt_ref[0, ch, s] = kdt[n_].astype(BF16)
    wi = lax.broadcasted_iota(jnp.int32, (c, 2 * c), 0)
    wj = lax.broadcasted_iota(jnp.int32, (c, 2 * c), 1)
    right = wj >= c
    eye_r = (wj - c == wi).astype(F32)
    a = jnp.stack(neg_l)
    a_wide = jnp.concatenate([a, jnp.zeros_like(a)], axis=2)
    ps = _bmm(a.astype(BF16), (a_wide + eye_r).astype(BF16)) + eye_r
    m = 2
    while m < c:
        ps = _bmm(ps[:, :, :c].astype(BF16), ps.astype(BF16)) + jnp.where(right, ps, 0.0)
        m *= 2
    rhs = jnp.stack(rhs_l)
    sol = _bmm(ps.astype(BF16), jnp.concatenate([jnp.zeros_like(rhs), rhs], axis=1))
    for n_, (rs, ls, _, _) in enumerate(dst):
        u_ref[0, rs, ls] = sol[n_, :, :dk].astype(BF16)
        w_ref[0, rs, ls] = sol[n_, :, dk:].astype(BF16)


def _delta_wy(q, k, v, beta, g, *, ts):
    b, t, w = q.shape
    nc = ts // DN_CHUNK
    ns = N_DIR * DN_HEADS
    tok = lambda n: pl.BlockSpec((1, ts, n), lambda i, j: (i, j, 0))
    return pl.pallas_call(
        functools.partial(_dwy_kernel, nc=nc),
        grid=(b, t // ts),
        in_specs=[tok(w), tok(w), tok(w), tok(LANES), tok(LANES)],
        out_specs=[tok(N_DIR * w)] * 4
        + [pl.BlockSpec((1, nc, ns, DN_HEAD_DIM, DN_CHUNK), lambda i, j: (i, j, 0, 0, 0)),
           pl.BlockSpec((1, nc, ns, LANES), lambda i, j: (i, j, 0, 0))],
        out_shape=[jax.ShapeDtypeStruct((b, t, N_DIR * w), BF16)] * 4
        + [jax.ShapeDtypeStruct((b, t // DN_CHUNK, ns, DN_HEAD_DIM, DN_CHUNK), BF16),
           jax.ShapeDtypeStruct((b, t // DN_CHUNK, ns, LANES), F32)],
        compiler_params=_params(("parallel", "parallel")),
        name="delta_wy",
    )(q, k, v, beta, g)


def _dscan_kernel(uf_ref, wf_ref, qgf_ref, af_ref, kdf_ref, egf_ref,
                  ur_ref, wr_ref, qgr_ref, ar_ref, kdr_ref, egr_ref, s0_ref,
                  of_ref, or_ref, sfin_ref, s_ref, *, nc, nb):
    i = pl.program_id(0)

    @pl.when(i == 0)
    def _():
        s_ref[...] = s0_ref[...]

    c, dk = DN_CHUNK, DN_HEAD_DIM
    dirs = ((uf_ref, wf_ref, qgf_ref, af_ref, kdf_ref, egf_ref, of_ref),
            (ur_ref, wr_ref, qgr_ref, ar_ref, kdr_ref, egr_ref, or_ref))

    def chunk_step(j, carry):
        wq_l, u_l, a_l, kd_l, eg_l, s_l, dst = [], [], [], [], [], [], []
        for r, (u_ref, w_ref, qg_ref, a_ref, kd_ref, eg_ref, o_ref) in enumerate(dirs):
            ch = nc - 1 - j if r else j
            rs = pl.ds(pl.multiple_of(ch * c, c), c)
            for b in range(nb):
                for h in range(DN_HEADS):
                    hs = slice(h * dk, (h + 1) * dk)
                    s = r * DN_HEADS + h
                    wq_l.append(jnp.concatenate([w_ref[b, rs, hs], qg_ref[b, rs, hs]], axis=0))
                    u_l.append(u_ref[b, rs, hs])
                    a_l.append(a_ref[b, rs, hs][:, :c])
                    kd_l.append(kd_ref[b, ch, h])
                    eg_l.append(eg_ref[b, ch, s:s + 1, :])
                    s_l.append(s_ref[b, s])
                    dst.append((o_ref, b, rs, hs, s))
        st = jnp.stack(s_l)
        ws_qs = _bmm(jnp.stack(wq_l), st.astype(BF16))
        v_new = (jnp.stack(u_l).astype(F32) - ws_qs[:, :c]).astype(BF16)
        o = ws_qs[:, c:] + _bmm(jnp.stack(a_l), v_new)
        s_new = st * jnp.stack(eg_l) + _bmm(jnp.stack(kd_l), v_new)
        for n_, (o_ref, b, rs, hs, s) in enumerate(dst):
            o_ref[b, rs, hs] = o[n_].astype(BF16)
            s_ref[b, s] = s_new[n_]
        return carry

    lax.fori_loop(0, nc, chunk_step, 0)

    @pl.when(i == pl.num_programs(0) - 1)
    def _():
        sfin_ref[...] = s_ref[...]


def _delta_scan(u, w, qg, a, kdt, eg, s0, *, ts):
    b, t, w2 = u.shape
    wd = w2 // N_DIR
    n = t // ts
    nc = ts // DN_CHUNK
    fwd = pl.BlockSpec((b, ts, wd), lambda i: (0, i, 0))
    rev = pl.BlockSpec((b, ts, wd), lambda i: (0, n - 1 - i, 1))
    kd_f = pl.BlockSpec((b, nc, DN_HEADS) + kdt.shape[3:], lambda i: (0, i, 0, 0, 0))
    kd_r = pl.BlockSpec((b, nc, DN_HEADS) + kdt.shape[3:], lambda i: (0, n - 1 - i, 1, 0, 0))
    eg_f = pl.BlockSpec((b, nc) + eg.shape[2:], lambda i: (0, i, 0, 0))
    eg_r = pl.BlockSpec((b, nc) + eg.shape[2:], lambda i: (0, n - 1 - i, 0, 0))
    out_f = pl.BlockSpec((b, ts, wd), lambda i: (0, i, 0))
    out_r = pl.BlockSpec((b, ts, wd), lambda i: (0, n - 1 - i, 0))
    return pl.pallas_call(
        functools.partial(_dscan_kernel, nc=nc, nb=b),
        grid=(n,),
        in_specs=[fwd, fwd, fwd, fwd, kd_f, eg_f, rev, rev, rev, rev, kd_r, eg_r, _resident(s0.shape)],
        out_specs=[out_f, out_r, pl.BlockSpec(s0.shape, lambda i: (0, 0, 0, 0))],
        out_shape=[jax.ShapeDtypeStruct((b, t, wd), BF16)] * 2 + [jax.ShapeDtypeStruct(s0.shape, F32)],
        scratch_shapes=[pltpu.VMEM(s0.shape, F32)],
        compiler_params=_params(("arbitrary",)),
        name="delta_scan",
    )(u, w, qg, a, kdt, eg, u, w, qg, a, kdt, eg, s0)


S5_POW_ROWS = 24


def _cexp(re, im):
    m = jnp.exp(re)
    return m * jnp.cos(im), m * jnp.sin(im)


def _s5_op_kernel(are_ref, aim_ref, ls_ref, btr_ref, bti_ref, ctr_ref, cti_ref,
                  win_ref, wm_ref, kbd_ref, lam_ref):
    l, p, n = S5_CHUNK, S5_GROUP, S5_STATE
    w = S5_BLK * n
    mask_w = (lax.broadcasted_iota(jnp.int32, (LANES, w), 0) // p
              == lax.broadcasted_iota(jnp.int32, (LANES, w), 1) // n)
    mask_c = (lax.broadcasted_iota(jnp.int32, (w, LANES), 0) // n
              == lax.broadcasted_iota(jnp.int32, (w, LANES), 1) // p)
    for r in range(N_DIR):
        are, aim = are_ref[r, 0], aim_ref[r, 0]
        dt = jnp.exp(ls_ref[r, 0])
        kk = lax.broadcasted_iota(jnp.int32, (S5_POW_ROWS, w), 0).astype(F32)
        pr, pi = _cexp(kk * (are * dt), kk * (aim * dt))
        lbr, lbi = pr[1:2], pi[1:2]
        den = are * are + aim * aim
        nr, ni = lbr - 1.0, lbi
        cr = (nr * are + ni * aim) / den
        ci = (ni * are - nr * aim) / den
        bre, bim = btr_ref[r, 0], bti_ref[r, 0]
        bmr = jnp.where(mask_w, jnp.concatenate([cr * bre - ci * bim] * S5_BLK, axis=0), 0.0)
        bmi = jnp.where(mask_w, jnp.concatenate([cr * bim + ci * bre] * S5_BLK, axis=0), 0.0)
        wr_l, wi_l = [], []
        for k in range(l):
            wr = bmr * pr[k:k + 1] - bmi * pi[k:k + 1]
            wi = bmr * pi[k:k + 1] + bmi * pr[k:k + 1]
            s = k if r else l - 1 - k
            win_ref[0, s * LANES:(s + 1) * LANES, (2 * r) * w:(2 * r + 1) * w] = wr.astype(BF16)
            win_ref[0, s * LANES:(s + 1) * LANES, (2 * r + 1) * w:(2 * r + 2) * w] = wi.astype(BF16)
            wr_l.append(wr)
            wi_l.append(wi)
        ctr = jnp.where(mask_c, ctr_ref[r, 0], 0.0)
        cti = jnp.where(mask_c, cti_ref[r, 0], 0.0)
        kbd_ref[0, r] = (_dot_f32(jnp.concatenate(wr_l, axis=0), ctr)
                         - _dot_f32(jnp.concatenate(wi_l, axis=0), cti))
        zpad = jnp.zeros((LANES - S5_POW_ROWS, w), F32)
        ptr = jnp.concatenate([pr, zpad], axis=0).T
        pti = jnp.concatenate([pi, zpad], axis=0).T
        for s in range(l):
            e = l - s if r else s + 1
            er, ei = ptr[:, e:e + 1], pti[:, e:e + 1]
            wm_ref[0, (2 * r) * w:(2 * r + 1) * w, s * LANES:(s + 1) * LANES] = (ctr * er - cti * ei).astype(BF16)
            wm_ref[0, (2 * r + 1) * w:(2 * r + 2) * w, s * LANES:(s + 1) * LANES] = (
                -(ctr * ei + cti * er)).astype(BF16)
        lam_ref[0, r:r + 1, :] = pr[l:l + 1]
        lam_ref[1, r:r + 1, :] = pi[l:l + 1]


S5_TOEP_ROWS = 4


def _s5_toep_kernel(kbd_ref, wt_ref):
    l = S5_CHUNK
    k0 = kbd_ref[0, 0, 0:LANES, :] + kbd_ref[0, 1, 0:LANES, :]
    for i in range(S5_TOEP_ROWS):
        sp = pl.program_id(1) * S5_TOEP_ROWS + i
        for s in range(l):
            d = s - sp
            kf = kbd_ref[0, 0, pl.ds(pl.multiple_of(jnp.maximum(d, 0) * LANES, LANES), LANES), :]
            kr = kbd_ref[0, 1, pl.ds(pl.multiple_of(jnp.maximum(-d, 0) * LANES, LANES), LANES), :]
            blk = jnp.where(d > 0, kf, jnp.where(d < 0, kr, k0))
            wt_ref[0, i * LANES:(i + 1) * LANES, s * LANES:(s + 1) * LANES] = blk.astype(BF16)


def _s5_operators(a_re, a_im, log_step, b_re, b_im, c_re, c_im):
    g, n, p, l = S5_GROUPS, S5_STATE, S5_GROUP, S5_CHUNK
    nb, g8 = S5_NBLK, S5_BLK
    w = g8 * n
    lane_row = lambda a: a.reshape(N_DIR, nb, 1, w)
    ls = lane_row(jnp.repeat(log_step, n, axis=1))
    bt = lambda b: jnp.transpose(b.reshape(N_DIR, nb, g8, n, p), (0, 1, 4, 2, 3)).reshape(N_DIR, nb, p, w)
    ct = lambda c: jnp.tile(jnp.swapaxes(c, 2, 3).reshape(N_DIR, nb, w, p), (1, 1, 1, g8))
    blk = lambda r, c: pl.BlockSpec((N_DIR, 1, r, c), lambda j: (0, j, 0, 0))
    ops = lambda r, c: pl.BlockSpec((1, r, c), lambda j: (j, 0, 0))
    lw = l * LANES
    w_in, w_m, kbd, lam = pl.pallas_call(
        _s5_op_kernel,
        grid=(nb,),
        in_specs=[blk(1, w)] * 3 + [blk(p, w)] * 2 + [blk(w, LANES)] * 2,
        out_specs=[ops(lw, 2 * N_DIR * w), ops(2 * N_DIR * w, lw),
                   pl.BlockSpec((1, N_DIR, lw, LANES), lambda j: (j, 0, 0, 0)),
                   pl.BlockSpec((2, N_DIR, w), lambda j: (0, 0, j))],
        out_shape=[jax.ShapeDtypeStruct((nb, lw, 2 * N_DIR * w), BF16),
                   jax.ShapeDtypeStruct((nb, 2 * N_DIR * w, lw), BF16),
                   jax.ShapeDtypeStruct((nb, N_DIR, lw, LANES), F32),
                   jax.ShapeDtypeStruct((2, N_DIR, g * n), F32)],
        compiler_params=_params(("parallel",)),
        name="s5_params",
    )(lane_row(a_re), lane_row(a_im), ls, bt(b_re), bt(b_im), ct(c_re), ct(c_im))
    w_t = pl.pallas_call(
        _s5_toep_kernel,
        grid=(nb, l // S5_TOEP_ROWS),
        in_specs=[pl.BlockSpec((1, N_DIR, lw, LANES), lambda j, i: (j, 0, 0, 0))],
        out_specs=pl.BlockSpec((1, S5_TOEP_ROWS * LANES, lw), lambda j, i: (j, i, 0)),
        out_shape=jax.ShapeDtypeStruct((nb, lw, lw), BF16),
        compiler_params=_params(("parallel", "parallel")),
        name="s5_toeplitz",
    )(kbd)
    return w_in, w_t, w_m, lam


def _chunk_rows(u_ref, nct):
    return jnp.concatenate([u_ref[0, pl.ds(s, nct, stride=S5_CHUNK), :] for s in range(S5_CHUNK)],
                           axis=1).astype(BF16)


def _s5_in_kernel(uc_ref, ul_ref, w_ref, *z_refs, ncc, ncl):
    v = jnp.concatenate([_chunk_rows(uc_ref, ncc), _chunk_rows(ul_ref, ncl)], axis=0)
    z = _dot(v, w_ref[0])
    wd = z_refs[0].shape[-1]
    for k in range(4):
        z_refs[k][...] = z[:ncc, k * wd:(k + 1) * wd]
        z_refs[4 + k][...] = z[ncc:, k * wd:(k + 1) * wd]


def _s5_chunk_in(u_c, u_l, w_in):
    b, tc, _ = u_c.shape
    t = u_l.shape[1]
    ncc, ncl = tc // S5_CHUNK, t // S5_CHUNK
    wd = S5_BLK * S5_STATE
    out = lambda rows: pl.BlockSpec((rows, wd), lambda j, i: (i, j))
    sds = lambda rows: jax.ShapeDtypeStruct((b * rows, S5_NBLK * wd), F32)
    outs = pl.pallas_call(
        functools.partial(_s5_in_kernel, ncc=ncc, ncl=ncl),
        grid=(S5_NBLK, b),
        in_specs=[pl.BlockSpec((1, tc, LANES), lambda j, i: (i, 0, j)),
                  pl.BlockSpec((1, t, LANES), lambda j, i: (i, 0, j)),
                  pl.BlockSpec((1,) + w_in.shape[1:], lambda j, i: (j, 0, 0))],
        out_specs=[out(ncc)] * 4 + [out(ncl)] * 4,
        out_shape=[sds(ncc)] * 4 + [sds(ncl)] * 4,
        compiler_params=_params(("parallel", "parallel")),
        name="s5_chunk_in",
    )(u_c, u_l, w_in)
    return list(outs[:4]), list(outs[4:])


SUBLANES = 8


def _cmul(ar, ai, br, bi):
    return ar * br - ai * bi, ar * bi + ai * br


def _s5_scan_tables(lr, li, row, rev):
    pw = [(lr, li)]
    for _ in range(SUBLANES - 1):
        pw.append(_cmul(pw[-1][0], pw[-1][1], lr, li))

    def by_row(power_of):
        tr, ti = jnp.zeros(row.shape, F32), jnp.zeros(row.shape, F32)
        for i in range(SUBLANES):
            k = power_of(i)
            if k:
                tr, ti = jnp.where(row == i, pw[k - 1][0], tr), jnp.where(row == i, pw[k - 1][1], ti)
        return tr, ti

    carry_w = by_row(lambda i: SUBLANES - i if rev else i + 1)
    steps = [by_row(lambda i, d=d: d if ((i <= SUBLANES - 1 - d) if rev else (i >= d)) else 0)
             for d in (1, 2, 4)]
    return carry_w, steps


def _s5_scan_tile(zr, zi, cr, ci, carry_w, steps, row, rev):
    yr, yi = zr, zi
    for d, (tr, ti) in zip((1, 2, 4), steps):
        sh = SUBLANES - d if rev else d
        ar, ai = _cmul(tr, ti, pltpu.roll(yr, sh, 0), pltpu.roll(yi, sh, 0))
        yr, yi = yr + ar, yi + ai
    ar, ai = _cmul(carry_w[0], carry_w[1], cr, ci)
    xr, xi = yr + ar, yi + ai
    sh, edge, last = (SUBLANES - 1, SUBLANES - 1, 0) if rev else (1, 0, SUBLANES - 1)
    er = jnp.where(row == edge, cr, pltpu.roll(xr, sh, 0))
    ei = jnp.where(row == edge, ci, pltpu.roll(xi, sh, 0))
    return er, ei, xr[last:last + 1], xi[last:last + 1]


def _s5_scan_kernel(cfr_ref, cfi_ref, crr_ref, cri_ref, zfr_ref, zfi_ref, zrr_ref, zri_ref, lam_ref,
                    xfr_ref, xfi_ref, xrr_ref, xri_ref, *, ncc, ncl, nb):
    tl = zfr_ref.shape[-1]
    row = lax.broadcasted_iota(jnp.int32, (SUBLANES, tl), 0)
    cw_f, st_f = _s5_scan_tables(lam_ref[0, 0:1, :], lam_ref[1, 0:1, :], row, False)
    cw_r, st_r = _s5_scan_tables(lam_ref[0, 1:2, :], lam_ref[1, 1:2, :], row, True)

    def make_step(zf, zr, nc, outs):
        nt = nc // SUBLANES

        def step(t, carry):
            new = []
            for b in range(nb):
                cfr, cfi, crr, cri = carry[b]
                rows = pl.ds(pl.multiple_of(b * nc + t * SUBLANES, SUBLANES), SUBLANES)
                er, ei, cfr, cfi = _s5_scan_tile(zf[0][rows, :], zf[1][rows, :], cfr, cfi, cw_f, st_f, row, False)
                if outs is not None:
                    outs[0][rows, :], outs[1][rows, :] = er, ei
                rows = pl.ds(pl.multiple_of(b * nc + (nt - 1 - t) * SUBLANES, SUBLANES), SUBLANES)
                er, ei, crr, cri = _s5_scan_tile(zr[0][rows, :], zr[1][rows, :], crr, cri, cw_r, st_r, row, True)
                if outs is not None:
                    outs[2][rows, :], outs[3][rows, :] = er, ei
                new.append((cfr, cfi, crr, cri))
            return tuple(new)

        return nt, step

    zero = jnp.zeros((1, tl), F32)
    carry = tuple((zero, zero, zero, zero) for _ in range(nb))
    nt, step = make_step((cfr_ref, cfi_ref), (crr_ref, cri_ref), ncc, None)
    carry = lax.fori_loop(0, nt, step, carry)
    nt, step = make_step((zfr_ref, zfi_ref), (zrr_ref, zri_ref), ncl, (xfr_ref, xfi_ref, xrr_ref, xri_ref))
    lax.fori_loop(0, nt, step, carry)


def _s5_state_scan(z_ctx, z_lat, lam, *, nb, tl):
    rc, w = z_ctx[0].shape
    rl = z_lat[0].shape[0]
    cb = pl.BlockSpec((rc, tl), lambda j: (0, j))
    zb = pl.BlockSpec((rl, tl), lambda j: (0, j))
    return pl.pallas_call(
        functools.partial(_s5_scan_kernel, ncc=rc // nb, ncl=rl // nb, nb=nb),
        grid=(w // tl,),
        in_specs=[cb] * 4 + [zb] * 4 + [pl.BlockSpec((2, N_DIR, tl), lambda j: (0, 0, j))],
        out_specs=[zb] * 4,
        out_shape=[jax.ShapeDtypeStruct((rl, w), F32)] * 4,
        compiler_params=_params(("parallel",)),
        name="s5_state_scan",
    )(*z_ctx, *z_lat, lam)


def _s5_out_kernel(u_ref, xfr_ref, xfi_ref, xrr_ref, xri_ref, wt_ref, wm_ref, y_ref, *, nct):
    x = jnp.concatenate([r[...].astype(BF16) for r in (xfr_ref, xfi_ref, xrr_ref, xri_ref)], axis=1)
    y = _dot(_chunk_rows(u_ref, nct), wt_ref[0]) + _dot(x, wm_ref[0])
    for s in range(S5_CHUNK):
        y_ref[0, pl.ds(s, nct, stride=S5_CHUNK), :] = y[:, s * LANES:(s + 1) * LANES]


def _s5_chunk_out(u, x4, w_t, w_m):
    b, t, _ = u.shape
    nct = t // S5_CHUNK
    wd = S5_BLK * S5_STATE
    xb = pl.BlockSpec((nct, wd), lambda j, i: (i, j))
    ub = pl.BlockSpec((1, t, LANES), lambda j, i: (i, 0, j))
    op = lambda a: pl.BlockSpec((1,) + a.shape[1:], lambda j, i: (j, 0, 0))
    return pl.pallas_call(
        functools.partial(_s5_out_kernel, nct=nct),
        grid=(S5_NBLK, b),
        in_specs=[ub] + [xb] * 4 + [op(w_t), op(w_m)],
        out_specs=ub,
        out_shape=jax.ShapeDtypeStruct(u.shape, F32),
        compiler_params=_params(("parallel", "parallel")),
        name="s5_chunk_out",
    )(u, *x4, w_t, w_m)


def _gelu_tanh(x):
    return 0.5 * x * (1.0 + jnp.tanh(0.7978845608028654 * (x + 0.044715 * x * x * x)))


def _merge_kernel(of_ref, or_ref, z_ref, ys_ref, u_ref, g_ref, x_ref, mod_ref,
                  dnw_ref, wa_ref, dsk_ref, wglu_ref, bglu_ref, wb_ref, wo_ref, n2w_ref,
                  xl_ref, h2_ref, *, d):
    o = of_ref[0].astype(F32) + or_ref[0].astype(F32)
    z = z_ref[0].astype(F32)
    heads = []
    for h in range(DN_HEADS):
        sl = slice(h * DN_HEAD_DIM, (h + 1) * DN_HEAD_DIM)
        oh = o[:, sl]
        on = oh * lax.rsqrt(jnp.mean(oh * oh, axis=-1, keepdims=True) + RMS_EPS) * dnw_ref[...]
        heads.append((on * _silu(z[:, sl])).astype(BF16))
    ya = _dot(jnp.concatenate(heads, axis=1), wa_ref[...])
    ys = ys_ref[0] + dsk_ref[...] * u_ref[0]
    zz = _dot(_gelu_tanh(ys).astype(BF16), wglu_ref[...]) + bglu_ref[...]
    yb = _dot((zz[:, :S5_WIDTH] * jax.nn.sigmoid(zz[:, S5_WIDTH:])).astype(BF16), wb_ref[...])
    gates = g_ref[0].astype(F32)
    mix = jax.nn.sigmoid(gates[:, :d]) * ya + jax.nn.sigmoid(gates[:, d:]) * yb
    xl = x_ref[0] + mod_ref[0, :, 2 * d:3 * d] * _dot(mix.astype(BF16), wo_ref[...])
    xl_ref[0] = xl
    hn = xl * lax.rsqrt(jnp.mean(xl * xl, axis=-1, keepdims=True) + RMS_EPS) * n2w_ref[...]
    h2_ref[0] = (hn * (1.0 + mod_ref[0, :, 4 * d:5 * d]) + mod_ref[0, :, 3 * d:4 * d]).astype(BF16)


def _mix_merge(o_f, o_r, z, ys, u, gates, x, mods, dn_norm_w, w_a_out, s5_d, w_glu, b_glu,
               w_b_out, w_o, norm2_w, *, tm):
    b, t, d = x.shape
    tok = lambda n: pl.BlockSpec((1, tm, n), lambda i, j: (i, j, 0))
    consts = [dn_norm_w.reshape(1, -1), w_a_out, s5_d.reshape(1, -1), w_glu, b_glu.reshape(1, -1),
              w_b_out, w_o, norm2_w.reshape(1, -1)]
    return pl.pallas_call(
        functools.partial(_merge_kernel, d=d),
        grid=(b, t // tm),
        in_specs=[tok(o_f.shape[-1]), tok(o_r.shape[-1]), tok(z.shape[-1]), tok(ys.shape[-1]),
                  tok(u.shape[-1]), tok(gates.shape[-1]), tok(d),
                  pl.BlockSpec((1, 1, mods.shape[-1]), lambda i, j: (i, 0, 0))]
        + [_resident(c.shape) for c in consts],
        out_specs=[tok(d), tok(d)],
        out_shape=[jax.ShapeDtypeStruct((b, t, d), F32), jax.ShapeDtypeStruct((b, t, d), BF16)],
        compiler_params=_params(("parallel", "parallel")),
        name="mix_merge",
    )(o_f, o_r, z, ys, u, gates, x, mods, *consts)


FFN_ROWS = 8
FFN_CB = 256


FFN_DOWN_GROUP = 4
FFN_AHEAD = 2


def _conv_row(e_ref, slot, part, r, cw):
    rows = [e_ref[slot, part, (r + i) * GRID_W:(r + i + 1) * GRID_W, :] for i in range(3)]
    taps = [rows[0] * cw[j:j + 1, :] + rows[1] * cw[3 + j:4 + j, :] + rows[2] * cw[6 + j:7 + j, :]
            for j in range(3)]
    left = pltpu.roll(taps[0], 1, 0)
    right = pltpu.roll(taps[2], GRID_W - 1, 0)
    sub = lax.broadcasted_iota(jnp.int32, (SUBLANES, left.shape[1]), 0)
    left = jnp.concatenate([jnp.where(sub == 0, 0.0, left[:SUBLANES]), left[SUBLANES:]], axis=0)
    right = jnp.concatenate([right[:-SUBLANES], jnp.where(sub == SUBLANES - 1, 0.0, right[-SUBLANES:])], axis=0)
    return left + taps[1] + right


def _ffn_kernel(h_ref, hp_ref, hn_ref, xl_ref, mod_ref, wup_ref, cw_ref, wd_ref, nfw_ref,
                o_ref, hext_ref, e_ref, act_ref, acc_ref, *, d, dff, nt):
    t = pl.program_id(1)
    n_out = h_ref.shape[1]
    hext_ref[0:GRID_W] = jnp.where(t == 0, jnp.zeros_like(hp_ref[0]), hp_ref[0])
    hext_ref[GRID_W:GRID_W + n_out] = h_ref[0]
    hext_ref[GRID_W + n_out:] = jnp.where(t == nt - 1, jnp.zeros_like(hn_ref[0]), hn_ref[0])
    ncb = dff // FFN_CB

    def up(k, slot):
        e_ref[slot, 0] = _dot(hext_ref[...], wup_ref[:, k * FFN_CB:(k + 1) * FFN_CB])
        e_ref[slot, 1] = _dot(hext_ref[...], wup_ref[:, dff + k * FFN_CB:dff + (k + 1) * FFN_CB])

    for k in range(FFN_AHEAD):
        up(k, k)
    done = 0
    for k in range(ncb):
        slot = k % (FFN_AHEAD + 1)
        if k + FFN_AHEAD < ncb:
            up(k + FFN_AHEAD, (k + FFN_AHEAD) % (FFN_AHEAD + 1))
        gs = slice(k * FFN_CB, (k + 1) * FFN_CB)
        cg = cw_ref[:, gs]
        cv = cw_ref[:, dff + k * FFN_CB:dff + (k + 1) * FFN_CB]
        for r in range(n_out // GRID_W):
            gate = _conv_row(e_ref, slot, 0, r, cg)
            val = _conv_row(e_ref, slot, 1, r, cv)
            act_ref[r * GRID_W:(r + 1) * GRID_W, gs] = (_silu(gate) * val).astype(BF16)
        if (k + 1) % FFN_DOWN_GROUP == 0 or k == ncb - 1:
            ks = slice(done * FFN_CB, (k + 1) * FFN_CB)
            part = _dot(act_ref[:, ks], wd_ref[ks, :])
            if done == 0:
                acc_ref[...] = part
            else:
                acc_ref[...] += part
            done = k + 1
    xo = xl_ref[0] + mod_ref[0, :, 5 * d:6 * d] * acc_ref[...]
    o_ref[0] = xo * lax.rsqrt(jnp.mean(xo * xo, axis=-1, keepdims=True) + RMS_EPS) * nfw_ref[...]


def _conv_ffn(h2, xl, mods, w_up, conv_w, w_down, norm_f_w):
    b, t, d = xl.shape
    dff = w_down.shape[0]
    tm = FFN_ROWS * GRID_W
    nt = t // tm
    nrow = t // GRID_W
    cw = conv_w.reshape(9, 2 * dff)
    tok = lambda: pl.BlockSpec((1, tm, d), lambda i, j: (i, j, 0))
    return pl.pallas_call(
        functools.partial(_ffn_kernel, d=d, dff=dff, nt=nt),
        grid=(b, nt),
        in_specs=[tok(),
                  pl.BlockSpec((1, GRID_W, d), lambda i, j: (i, jnp.maximum(j * FFN_ROWS - 1, 0), 0)),
                  pl.BlockSpec((1, GRID_W, d), lambda i, j: (i, jnp.minimum((j + 1) * FFN_ROWS, nrow - 1), 0)),
                  tok(),
                  pl.BlockSpec((1, 1, mods.shape[-1]), lambda i, j: (i, 0, 0)),
                  _resident(w_up.shape), _resident(cw.shape), _resident(w_down.shape), _resident((1, d))],
        out_specs=tok(),
        out_shape=jax.ShapeDtypeStruct((b, t, d), F32),
        scratch_shapes=[pltpu.VMEM((tm + 2 * GRID_W, d), BF16),
                        pltpu.VMEM((FFN_AHEAD + 1, 2, tm + 2 * GRID_W, FFN_CB), F32),
                        pltpu.VMEM((tm, dff), BF16),
                        pltpu.VMEM((tm, d), F32)],
        compiler_params=_params(("parallel", "parallel")),
        name="conv_ffn",
    )(h2, h2, h2, xl, mods, w_up.astype(BF16), cw, w_down.astype(BF16), norm_f_w.reshape(1, d))


def _lane_row(a):
    a = a.reshape(-1).astype(F32)
    return jnp.pad(a, (0, LANES - a.shape[0])).reshape(1, LANES)


def _pad_cols(w):
    return jnp.pad(w, ((0, 0), (0, LANES - w.shape[1])))


def kernel(x, c, ctx, c_ctx, w_ada, b_ada, norm1_w, w_in, dn_conv_w, dn_a_log, dn_dt_bias, dn_norm_w,
           w_a_out, s5_a_re, s5_a_im, s5_log_step, s5_b_re, s5_b_im, s5_c_re, s5_c_im, s5_d, w_glu,
           b_glu, w_b_out, w_o, norm2_w, w_up, ffn_conv_w, w_down, norm_f_w):
    assert w_ada.shape[0] == 1, "single-layer block"
    b, t, d = x.shape
    tc = ctx.shape[1]
    nh = N_DIR * DN_HEADS
    assert b < 8, "batch rows and the context row share one 8-row modulation block"

    c_rows = jnp.zeros((8, d), F32).at[:b].set(c).at[b].set(c_ctx)
    mods = _modulation(c_rows, w_ada[0], b_ada[0]).reshape(8, 1, N_MOD * d)

    w = w_in[0]
    o_z, o_b, o_a = 3 * DN_WIDTH, 4 * DN_WIDTH, 4 * DN_WIDTH + nh
    o_u = o_a + nh
    o_g = o_u + S5_WIDTH
    wqkv = w[:, :o_z].astype(BF16)
    wz = w[:, o_z:o_b].astype(BF16)
    wba = jnp.concatenate([_pad_cols(w[:, o_b:o_a]), _pad_cols(w[:, o_a:o_u])], axis=1).astype(BF16)
    wu = w[:, o_u:o_g].astype(BF16)
    wg = w[:, o_g:].astype(BF16)

    qkv_l, ba_l, u_l, z_l, gates_l = _in_proj(x, mods, lambda i: i, norm1_w[0], wqkv, wba, wu, wz, wg, tm=512)
    qkv_c, ba_c, u_c = _in_proj(ctx, mods, lambda i: b, norm1_w[0], wqkv, wba, wu, tm=tc)

    alog_row, dtb_row = _lane_row(dn_a_log[0]), _lane_row(dn_dt_bias[0])
    prep_c = _delta_prep(qkv_c, ba_c, dn_conv_w[0], alog_row, dtb_row, tm=tc)
    prep_l = _delta_prep(qkv_l, ba_l, dn_conv_w[0], alog_row, dtb_row, tm=512)
    s0 = jnp.zeros((b, nh, DN_HEAD_DIM, DN_HEAD_DIM), F32)
    _, _, s_ctx = _delta_scan(*_delta_wy(*prep_c, ts=tc), s0, ts=128)
    o_f, o_r, _ = _delta_scan(*_delta_wy(*prep_l, ts=256), s_ctx, ts=128)

    w_s5in, w_s5t, w_s5m, lam = _s5_operators(s5_a_re[0], s5_a_im[0], s5_log_step[0], s5_b_re[0],
                                              s5_b_im[0], s5_c_re[0], s5_c_im[0])
    zs_c, zs_l = _s5_chunk_in(u_c, u_l, w_s5in)
    x_l = _s5_state_scan(zs_c, zs_l, lam, nb=b, tl=256)
    y_s5 = _s5_chunk_out(u_l, x_l, w_s5t, w_s5m)

    xl, h2 = _mix_merge(o_f, o_r, z_l, y_s5, u_l, gates_l, x, mods, dn_norm_w[0], w_a_out[0].astype(BF16),
                        s5_d[0], w_glu[0].astype(BF16), b_glu[0], w_b_out[0].astype(BF16),
                        w_o[0].astype(BF16), norm2_w[0], tm=512)
    return _conv_ffn(h2, xl, mods, w_up[0], ffn_conv_w[0], w_down[0], norm_f_w)
```

```python
import functools

import jax
import jax.numpy as jnp
from jax import lax
from jax.experimental import pallas as pl
from jax.experimental.pallas import tpu as pltpu

F32 = jnp.float32
BF16 = jnp.bfloat16

GRID_W = 64
N_DIR = 2
DN_HEADS = 4
DN_HEAD_DIM = 128
DN_WIDTH = DN_HEADS * DN_HEAD_DIM
DN_CHUNK = 64
S5_WIDTH = 512
S5_GROUP = 16
S5_GROUPS = S5_WIDTH // S5_GROUP
S5_STATE = 64
S5_CHUNK = 16
N_MOD = 6
RMS_EPS = 1e-6
L2_EPS = 1e-6
LANES = 128
S5_BLK = LANES // S5_GROUP
S5_NBLK = S5_GROUPS // S5_BLK
VMEM_LIMIT = 56 * 1024 * 1024


def _dot(a, b):
    return jnp.dot(a, b, preferred_element_type=F32)


def _dot_f32(a, b):
    return jnp.dot(a, b, preferred_element_type=F32, precision=lax.Precision.HIGHEST)


def _dot_split(a, b):
    ah, bh = a.astype(BF16), b.astype(BF16)
    al, bl = (a - ah.astype(F32)).astype(BF16), (b - bh.astype(F32)).astype(BF16)
    return _dot(ah, bh) + _dot(ah, bl) + _dot(al, bh)


def _dot_nt_f32(a, b):
    return lax.dot_general(a, b, (((1,), (1,)), ((), ())), preferred_element_type=F32,
                           precision=lax.Precision.HIGHEST)


def _silu(x):
    return x * jax.nn.sigmoid(x)


def _softplus(x):
    return jnp.maximum(x, 0.0) + jnp.log(1.0 + jnp.exp(-jnp.abs(x)))


def _params(sem, vmem=VMEM_LIMIT, flags=None):
    return pltpu.CompilerParams(dimension_semantics=sem, vmem_limit_bytes=vmem, flags=flags)


def _resident(shape):
    nd = len(shape)
    return pl.BlockSpec(shape, lambda *_: (0,) * nd, pipeline_mode=pl.Buffered(1))


def _mod_kernel(c_ref, w_ref, b_ref, o_ref):
    sc = _silu(c_ref[...])
    o_ref[...] = _dot(sc.astype(BF16), w_ref[...].astype(BF16)) + b_ref[...]


def _modulation(c_rows, w_ada, b_ada):
    d, n = w_ada.shape
    tn = n // 4
    return pl.pallas_call(
        _mod_kernel,
        grid=(n // tn,),
        in_specs=[pl.BlockSpec(c_rows.shape, lambda j: (0, 0)),
                  pl.BlockSpec((d, tn), lambda j: (0, j)),
                  pl.BlockSpec((1, tn), lambda j: (0, j))],
        out_specs=pl.BlockSpec((c_rows.shape[0], tn), lambda j: (0, j)),
        out_shape=jax.ShapeDtypeStruct((c_rows.shape[0], n), F32),
        compiler_params=_params(("arbitrary",)),
        name="adaln_mod",
    )(c_rows, w_ada, b_ada.reshape(1, n))


def _inproj_kernel(x_ref, mod_ref, nw_ref, wqkv_ref, wba_ref, wu_ref, wz_ref, wg_ref,
                   qkv_ref, ba_ref, u_ref, z_ref, g_ref, *, d):
    x = x_ref[0]
    ms = jnp.mean(x * x, axis=-1, keepdims=True)
    h = x * lax.rsqrt(ms + RMS_EPS) * nw_ref[...]
    shift = mod_ref[0, :, 0:d]
    scale = mod_ref[0, :, d:2 * d]
    hb = (h * (1.0 + scale) + shift).astype(BF16)
    qkv_ref[0] = _dot(hb, wqkv_ref[...]).astype(BF16)
    ba_ref[0] = _dot(hb, wba_ref[...])
    u_ref[0] = _dot(hb, wu_ref[...])
    if z_ref is not None:
        z_ref[0] = _dot(hb, wz_ref[...]).astype(BF16)
        g_ref[0] = _dot(hb, wg_ref[...]).astype(BF16)


def _inproj_ctx_kernel(x_ref, mod_ref, nw_ref, wqkv_ref, wba_ref, wu_ref,
                       qkv_ref, ba_ref, u_ref, *, d):
    _inproj_kernel(x_ref, mod_ref, nw_ref, wqkv_ref, wba_ref, wu_ref, None, None,
                   qkv_ref, ba_ref, u_ref, None, None, d=d)


def _in_proj(x, mods, mod_row0, norm_w, wqkv, wba, wu, wz=None, wg=None, *, tm):
    b, t, d = x.shape
    full = wz is not None
    tok = lambda n: pl.BlockSpec((1, tm, n), lambda i, j: (i, j, 0))
    in_specs = [tok(d),
                pl.BlockSpec((1, 1, mods.shape[-1]), lambda i, j: (mod_row0(i), 0, 0)),
                _resident((1, d)), _resident(wqkv.shape), _resident(wba.shape),
                _resident(wu.shape)]
    args = [x, mods, norm_w.reshape(1, d), wqkv, wba, wu]
    out_specs = [tok(wqkv.shape[1]), tok(wba.shape[1]), tok(wu.shape[1])]
    out_shape = [jax.ShapeDtypeStruct((b, t, wqkv.shape[1]), BF16),
                 jax.ShapeDtypeStruct((b, t, wba.shape[1]), F32),
                 jax.ShapeDtypeStruct((b, t, wu.shape[1]), F32)]
    if full:
        in_specs += [_resident(wz.shape), _resident(wg.shape)]
        args += [wz, wg]
        out_specs += [tok(wz.shape[1]), tok(wg.shape[1])]
        out_shape += [jax.ShapeDtypeStruct((b, t, wz.shape[1]), BF16),
                      jax.ShapeDtypeStruct((b, t, wg.shape[1]), BF16)]
    body = functools.partial(_inproj_kernel if full else _inproj_ctx_kernel, d=d)
    return pl.pallas_call(
        body, grid=(b, t // tm), in_specs=in_specs, out_specs=out_specs, out_shape=out_shape,
        compiler_params=_params(("parallel", "parallel")),
        name="in_proj" if full else "in_proj_ctx",
    )(*args)


HALO = 16


def _dprep_kernel(x_ref, xp_ref, xn_ref, ba_ref, cw_ref, alog_ref, dtb_ref,
                  q_ref, k_ref, v_ref, beta_ref, g_ref, *, tm, nt):
    t = pl.program_id(1)
    x = x_ref[0].astype(F32)
    prow = jnp.where(t == 0, 0.0, xp_ref[0, HALO - 1:HALO, :].astype(F32))
    nrow = jnp.where(t == nt - 1, 0.0, xn_ref[0, 0:1, :].astype(F32))
    sub = lax.broadcasted_iota(jnp.int32, (SUBLANES, x.shape[1]), 0)
    xprev = pltpu.roll(x, 1, 0)
    xnext = pltpu.roll(x, tm - 1, 0)
    xprev = jnp.concatenate([jnp.where(sub == 0, prow, xprev[:SUBLANES]), xprev[SUBLANES:]], axis=0)
    xnext = jnp.concatenate([xnext[:-SUBLANES], jnp.where(sub == SUBLANES - 1, nrow, xnext[-SUBLANES:])], axis=0)
    y = _silu(xprev * cw_ref[0:1, :] + x * cw_ref[1:2, :] + xnext * cw_ref[2:3, :])
    for h in range(DN_HEADS):
        sl = slice(h * DN_HEAD_DIM, (h + 1) * DN_HEAD_DIM)
        qh = y[:, h * DN_HEAD_DIM:(h + 1) * DN_HEAD_DIM]
        kh = y[:, DN_WIDTH + h * DN_HEAD_DIM:DN_WIDTH + (h + 1) * DN_HEAD_DIM]
        qn = qh * lax.rsqrt(jnp.sum(qh * qh, axis=-1, keepdims=True) + L2_EPS)
        kn = kh * lax.rsqrt(jnp.sum(kh * kh, axis=-1, keepdims=True) + L2_EPS)
        q_ref[0, :, sl] = (qn * (DN_HEAD_DIM ** -0.5)).astype(BF16)
        k_ref[0, :, sl] = kn.astype(BF16)
    v_ref[0] = y[:, 2 * DN_WIDTH:3 * DN_WIDTH].astype(BF16)
    ba = ba_ref[0]
    beta_ref[0] = jax.nn.sigmoid(ba[:, 0:LANES])
    g_ref[0] = -jnp.exp(alog_ref[...]) * _softplus(ba[:, LANES:2 * LANES] + dtb_ref[...])


def _delta_prep(qkv, ba, conv_w, alog_row, dtb_row, *, tm):
    b, t, c = qkv.shape
    nt = t // tm
    r = tm // HALO
    tok = lambda n: pl.BlockSpec((1, tm, n), lambda i, j: (i, j, 0))
    return pl.pallas_call(
        functools.partial(_dprep_kernel, tm=tm, nt=nt),
        grid=(b, nt),
        in_specs=[tok(c),
                  pl.BlockSpec((1, HALO, c), lambda i, j: (i, jnp.maximum(j * r - 1, 0), 0)),
                  pl.BlockSpec((1, HALO, c), lambda i, j: (i, jnp.minimum((j + 1) * r, t // HALO - 1), 0)),
                  tok(ba.shape[-1]),
                  _resident(conv_w.shape), _resident(alog_row.shape), _resident(dtb_row.shape)],
        out_specs=[tok(DN_WIDTH), tok(DN_WIDTH), tok(DN_WIDTH), tok(LANES), tok(LANES)],
        out_shape=[jax.ShapeDtypeStruct((b, t, DN_WIDTH), BF16)] * 3
        + [jax.ShapeDtypeStruct((b, t, LANES), F32)] * 2,
        compiler_params=_params(("parallel", "parallel")),
        name="delta_prep",
    )(qkv, qkv, qkv, ba, conv_w, alog_row, dtb_row)


def _bmm(a, b):
    return jnp.einsum('nik,nkj->nij', a, b, preferred_element_type=F32)


def _bmm_nt(a, b):
    return jnp.einsum('nik,njk->nij', a, b, preferred_element_type=F32)


def _dwy_kernel(q_ref, k_ref, v_ref, b_ref, g_ref, u_ref, w_ref, qg_ref, a_ref, kdt_ref, eg_ref, *, nc):
    c, dk = DN_CHUNK, DN_HEAD_DIM
    ts = nc * c
    ii = lax.broadcasted_iota(jnp.int32, (c, c), 0)
    jj = lax.broadcasted_iota(jnp.int32, (c, c), 1)
    ti = lax.broadcasted_iota(jnp.int32, (ts, ts), 0)
    tj = lax.broadcasted_iota(jnp.int32, (ts, ts), 1)
    same_chunk = (ti // c) == (tj // c)
    blocks = [(ch, h) for ch in range(nc) for h in range(DN_HEADS)]
    tile = lambda ref, ch, h: ref[0, ch * c:(ch + 1) * c, h * dk:(h + 1) * dk]
    k_l = [tile(k_ref, ch, h) for ch, h in blocks]
    q_l = [tile(q_ref, ch, h) for ch, h in blocks]
    v_l = [tile(v_ref, ch, h) for ch, h in blocks]
    kkqk = _bmm_nt(jnp.stack([jnp.concatenate([k_, q_], axis=0) for k_, q_ in zip(k_l, q_l)]),
                   jnp.stack(k_l))
    beta = b_ref[0]
    g = g_ref[0]
    a_pad = jnp.zeros((c, dk - c), BF16)
    neg_l, rhs_l, kd_l, dst = [], [], [], []
    for r in range(N_DIR):
        incl = ii <= jj if r else ii >= jj
        strict = ii < jj if r else ii > jj
        tri = (same_chunk & (ti <= tj if r else ti >= tj)).astype(F32)
        g_cum = _dot_f32(tri, g)
        g_cum_t = g_cum.T
        for n_, (ch, h) in enumerate(blocks):
            s = r * DN_HEADS + h
            rs = slice(ch * c, (ch + 1) * c)
            ls = slice(s * dk, (s + 1) * dk)
            g_c = g_cum[rs, s:s + 1]
            g_r = g_cum_t[s:s + 1, rs]
            g_end = g_c[0:1] if r else g_c[c - 1:c]
            b_c = beta[rs, s:s + 1]
            decay = jnp.where(incl, jnp.exp(jnp.where(incl, g_c - g_r, 0.0)), 0.0)
            neg_l.append(jnp.where(strict, kkqk[n_, :c] * (-b_c) * decay, 0.0))
            eg = jnp.exp(g_c)
            kf = k_l[n_].astype(F32)
            rhs_l.append(jnp.concatenate([(v_l[n_].astype(F32) * b_c).astype(BF16),
                                          (kf * (b_c * eg)).astype(BF16)], axis=1))
            qg_ref[0, rs, ls] = (q_l[n_].astype(F32) * eg).astype(BF16)
            a_ref[0, rs, ls] = jnp.concatenate([(kkqk[n_, c:] * decay).astype(BF16), a_pad], axis=1)
            kd_l.append((kf * jnp.exp(g_end - g_c)).astype(BF16))
            eg_ref[0, ch, s:s + 1, :] = jnp.broadcast_to(jnp.exp(g_end), (1, LANES))
            dst.append((rs, ls, ch, s))
    eye_k = (lax.broadcasted_iota(jnp.int32, (dk, dk), 0)
             == lax.broadcasted_iota(jnp.int32, (dk, dk), 1)).astype(BF16)
    kdt = _bmm_nt(jnp.broadcast_to(eye_k, (len(kd_l), dk, dk)), jnp.stack(kd_l))
    for n_, (_, _, ch, s) in enumerate(dst):
        kdt_ref[0, ch, s] = kdt[n_].astype(BF16)
    wi = lax.broadcasted_iota(jnp.int32, (c, 2 * c), 0)
    wj = lax.broadcasted_iota(jnp.int32, (c, 2 * c), 1)
    right = wj >= c
    eye_r = (wj - c == wi).astype(F32)
    a = jnp.stack(neg_l)
    a_wide = jnp.concatenate([a, jnp.zeros_like(a)], axis=2)
    ps = _bmm(a.astype(BF16), (a_wide + eye_r).astype(BF16)) + eye_r
    m = 2
    while m < c:
        ps = _bmm(ps[:, :, :c].astype(BF16), ps.astype(BF16)) + jnp.where(right, ps, 0.0)
        m *= 2
    rhs = jnp.stack(rhs_l)
    sol = _bmm(ps.astype(BF16), jnp.concatenate([jnp.zeros_like(rhs), rhs], axis=1))
    for n_, (rs, ls, _, _) in enumerate(dst):
        u_ref[0, rs, ls] = sol[n_, :, :dk].astype(BF16)
        w_ref[0, rs, ls] = sol[n_, :, dk:].astype(BF16)


def _delta_wy(q, k, v, beta, g, *, ts):
    b, t, w = q.shape
    nc = ts // DN_CHUNK
    ns = N_DIR * DN_HEADS
    tok = lambda n: pl.BlockSpec((1, ts, n), lambda i, j: (i, j, 0))
    return pl.pallas_call(
        functools.partial(_dwy_kernel, nc=nc),
        grid=(b, t // ts),
        in_specs=[tok(w), tok(w), tok(w), tok(LANES), tok(LANES)],
        out_specs=[tok(N_DIR * w)] * 4
        + [pl.BlockSpec((1, nc, ns, DN_HEAD_DIM, DN_CHUNK), lambda i, j: (i, j, 0, 0, 0)),
           pl.BlockSpec((1, nc, ns, LANES), lambda i, j: (i, j, 0, 0))],
        out_shape=[jax.ShapeDtypeStruct((b, t, N_DIR * w), BF16)] * 4
        + [jax.ShapeDtypeStruct((b, t // DN_CHUNK, ns, DN_HEAD_DIM, DN_CHUNK), BF16),
           jax.ShapeDtypeStruct((b, t // DN_CHUNK, ns, LANES), F32)],
        compiler_params=_params(("parallel", "parallel")),
        name="delta_wy",
    )(q, k, v, beta, g)


def _dscan_kernel(uf_ref, wf_ref, qgf_ref, af_ref, kdf_ref, egf_ref,
                  ur_ref, wr_ref, qgr_ref, ar_ref, kdr_ref, egr_ref, s0_ref,
                  of_ref, or_ref, sfin_ref, s_ref, *, nc, nb):
    i = pl.program_id(0)

    @pl.when(i == 0)
    def _():
        s_ref[...] = s0_ref[...]

    c, dk = DN_CHUNK, DN_HEAD_DIM
    dirs = ((uf_ref, wf_ref, qgf_ref, af_ref, kdf_ref, egf_ref, of_ref),
            (ur_ref, wr_ref, qgr_ref, ar_ref, kdr_ref, egr_ref, or_ref))

    def chunk_step(j, carry):
        wq_l, u_l, a_l, kd_l, eg_l, s_l, dst = [], [], [], [], [], [], []
        for r, (u_ref, w_ref, qg_ref, a_ref, kd_ref, eg_ref, o_ref) in enumerate(dirs):
            ch = nc - 1 - j if r else j
            rs = pl.ds(pl.multiple_of(ch * c, c), c)
            for b in range(nb):
                for h in range(DN_HEADS):
                    hs = slice(h * dk, (h + 1) * dk)
                    s = r * DN_HEADS + h
                    wq_l.append(jnp.concatenate([w_ref[b, rs, hs], qg_ref[b, rs, hs]], axis=0))
                    u_l.append(u_ref[b, rs, hs])
                    a_l.append(a_ref[b, rs, hs][:, :c])
                    kd_l.append(kd_ref[b, ch, h])
                    eg_l.append(eg_ref[b, ch, s:s + 1, :])
                    s_l.append(s_ref[b, s])
                    dst.append((o_ref, b, rs, hs, s))
        st = jnp.stack(s_l)
        ws_qs = _bmm(jnp.stack(wq_l), st.astype(BF16))
        v_new = (jnp.stack(u_l).astype(F32) - ws_qs[:, :c]).astype(BF16)
        o = ws_qs[:, c:] + _bmm(jnp.stack(a_l), v_new)
        s_new = st * jnp.stack(eg_l) + _bmm(jnp.stack(kd_l), v_new)
        for n_, (o_ref, b, rs, hs, s) in enumerate(dst):
            o_ref[b, rs, hs] = o[n_].astype(BF16)
            s_ref[b, s] = s_new[n_]
        return carry

    lax.fori_loop(0, nc, chunk_step, 0)

    @pl.when(i == pl.num_programs(0) - 1)
    def _():
        sfin_ref[...] = s_ref[...]


def _delta_scan(u, w, qg, a, kdt, eg, s0, *, ts):
    b, t, w2 = u.shape
    wd = w2 // N_DIR
    n = t // ts
    nc = ts // DN_CHUNK
    fwd = pl.BlockSpec((b, ts, wd), lambda i: (0, i, 0))
    rev = pl.BlockSpec((b, ts, wd), lambda i: (0, n - 1 - i, 1))
    kd_f = pl.BlockSpec((b, nc, DN_HEADS) + kdt.shape[3:], lambda i: (0, i, 0, 0, 0))
    kd_r = pl.BlockSpec((b, nc, DN_HEADS) + kdt.shape[3:], lambda i: (0, n - 1 - i, 1, 0, 0))
    eg_f = pl.BlockSpec((b, nc) + eg.shape[2:], lambda i: (0, i, 0, 0))
    eg_r = pl.BlockSpec((b, nc) + eg.shape[2:], lambda i: (0, n - 1 - i, 0, 0))
    out_f = pl.BlockSpec((b, ts, wd), lambda i: (0, i, 0))
    out_r = pl.BlockSpec((b, ts, wd), lambda i: (0, n - 1 - i, 0))
    return pl.pallas_call(
        functools.partial(_dscan_kernel, nc=nc, nb=b),
        grid=(n,),
        in_specs=[fwd, fwd, fwd, fwd, kd_f, eg_f, rev, rev, rev, rev, kd_r, eg_r, _resident(s0.shape)],
        out_specs=[out_f, out_r, pl.BlockSpec(s0.shape, lambda i: (0, 0, 0, 0))],
        out_shape=[jax.ShapeDtypeStruct((b, t, wd), BF16)] * 2 + [jax.ShapeDtypeStruct(s0.shape, F32)],
        scratch_shapes=[pltpu.VMEM(s0.shape, F32)],
        compiler_params=_params(("arbitrary",)),
        name="delta_scan",
    )(u, w, qg, a, kdt, eg, u, w, qg, a, kdt, eg, s0)


S5_POW_ROWS = 24


def _cexp(re, im):
    m = jnp.exp(re)
    return m * jnp.cos(im), m * jnp.sin(im)


def _s5_op_kernel(are_ref, aim_ref, ls_ref, btr_ref, bti_ref, ctr_ref, cti_ref,
                  win_ref, wm_ref, kbd_ref, lam_ref):
    l, p, n = S5_CHUNK, S5_GROUP, S5_STATE
    w = S5_BLK * n
    mask_w = (lax.broadcasted_iota(jnp.int32, (LANES, w), 0) // p
              == lax.broadcasted_iota(jnp.int32, (LANES, w), 1) // n)
    mask_c = (lax.broadcasted_iota(jnp.int32, (w, LANES), 0) // n
              == lax.broadcasted_iota(jnp.int32, (w, LANES), 1) // p)
    for r in range(N_DIR):
        are, aim = are_ref[r, 0], aim_ref[r, 0]
        dt = jnp.exp(ls_ref[r, 0])
        kk = lax.broadcasted_iota(jnp.int32, (S5_POW_ROWS, w), 0).astype(F32)
        pr, pi = _cexp(kk * (are * dt), kk * (aim * dt))
        lbr, lbi = pr[1:2], pi[1:2]
        den = are * are + aim * aim
        nr, ni = lbr - 1.0, lbi
        cr = (nr * are + ni * aim) / den
        ci = (ni * are - nr * aim) / den
        bre, bim = btr_ref[r, 0], bti_ref[r, 0]
        bmr = jnp.where(mask_w, jnp.concatenate([cr * bre - ci * bim] * S5_BLK, axis=0), 0.0)
        bmi = jnp.where(mask_w, jnp.concatenate([cr * bim + ci * bre] * S5_BLK, axis=0), 0.0)
        wr_l, wi_l = [], []
        for k in range(l):
            wr = bmr * pr[k:k + 1] - bmi * pi[k:k + 1]
            wi = bmr * pi[k:k + 1] + bmi * pr[k:k + 1]
            s = k if r else l - 1 - k
            win_ref[0, s * LANES:(s + 1) * LANES, (2 * r) * w:(2 * r + 1) * w] = wr.astype(BF16)
            win_ref[0, s * LANES:(s + 1) * LANES, (2 * r + 1) * w:(2 * r + 2) * w] = wi.astype(BF16)
            wr_l.append(wr)
            wi_l.append(wi)
        ctr = jnp.where(mask_c, ctr_ref[r, 0], 0.0)
        cti = jnp.where(mask_c, cti_ref[r, 0], 0.0)
        kbd_ref[0, r] = (_dot_split(jnp.concatenate(wr_l, axis=0), ctr)
                         - _dot_split(jnp.concatenate(wi_l, axis=0), cti))
        zpad = jnp.zeros((LANES - S5_POW_ROWS, w), F32)
        ptr = jnp.concatenate([pr, zpad], axis=0).T
        pti = jnp.concatenate([pi, zpad], axis=0).T
        for s in range(l):
            e = l - s if r else s + 1
            er, ei = ptr[:, e:e + 1], pti[:, e:e + 1]
            wm_ref[0, (2 * r) * w:(2 * r + 1) * w, s * LANES:(s + 1) * LANES] = (ctr * er - cti * ei).astype(BF16)
            wm_ref[0, (2 * r + 1) * w:(2 * r + 2) * w, s * LANES:(s + 1) * LANES] = (
                -(ctr * ei + cti * er)).astype(BF16)
        lam_ref[0, r:r + 1, :] = pr[l:l + 1]
        lam_ref[1, r:r + 1, :] = pi[l:l + 1]


S5_TOEP_ROWS = 4


def _s5_toep_kernel(kbd_ref, wt_ref):
    l = S5_CHUNK
    k0 = kbd_ref[0, 0, 0:LANES, :] + kbd_ref[0, 1, 0:LANES, :]
    for i in range(S5_TOEP_ROWS):
        sp = pl.program_id(1) * S5_TOEP_ROWS + i
        for s in range(l):
            d = s - sp
            kf = kbd_ref[0, 0, pl.ds(pl.multiple_of(jnp.maximum(d, 0) * LANES, LANES), LANES), :]
            kr = kbd_ref[0, 1, pl.ds(pl.multiple_of(jnp.maximum(-d, 0) * LANES, LANES), LANES), :]
            blk = jnp.where(d > 0, kf, jnp.where(d < 0, kr, k0))
            wt_ref[0, i * LANES:(i + 1) * LANES, s * LANES:(s + 1) * LANES] = blk.astype(BF16)


def _s5_operators(a_re, a_im, log_step, b_re, b_im, c_re, c_im):
    g, n, p, l = S5_GROUPS, S5_STATE, S5_GROUP, S5_CHUNK
    nb, g8 = S5_NBLK, S5_BLK
    w = g8 * n
    lane_row = lambda a: a.reshape(N_DIR, nb, 1, w)
    ls = lane_row(jnp.repeat(log_step, n, axis=1))
    bt = lambda b: jnp.transpose(b.reshape(N_DIR, nb, g8, n, p), (0, 1, 4, 2, 3)).reshape(N_DIR, nb, p, w)
    ct = lambda c: jnp.tile(jnp.swapaxes(c, 2, 3).reshape(N_DIR, nb, w, p), (1, 1, 1, g8))
    blk = lambda r, c: pl.BlockSpec((N_DIR, 1, r, c), lambda j: (0, j, 0, 0))
    ops = lambda r, c: pl.BlockSpec((1, r, c), lambda j: (j, 0, 0))
    lw = l * LANES
    w_in, w_m, kbd, lam = pl.pallas_call(
        _s5_op_kernel,
        grid=(nb,),
        in_specs=[blk(1, w)] * 3 + [blk(p, w)] * 2 + [blk(w, LANES)] * 2,
        out_specs=[ops(lw, 2 * N_DIR * w), ops(2 * N_DIR * w, lw),
                   pl.BlockSpec((1, N_DIR, lw, LANES), lambda j: (j, 0, 0, 0)),
                   pl.BlockSpec((2, N_DIR, w), lambda j: (0, 0, j))],
        out_shape=[jax.ShapeDtypeStruct((nb, lw, 2 * N_DIR * w), BF16),
                   jax.ShapeDtypeStruct((nb, 2 * N_DIR * w, lw), BF16),
                   jax.ShapeDtypeStruct((nb, N_DIR, lw, LANES), F32),
                   jax.ShapeDtypeStruct((2, N_DIR, g * n), F32)],
        compiler_params=_params(("parallel",)),
        name="s5_params",
    )(lane_row(a_re), lane_row(a_im), ls, bt(b_re), bt(b_im), ct(c_re), ct(c_im))
    w_t = pl.pallas_call(
        _s5_toep_kernel,
        grid=(nb, l // S5_TOEP_ROWS),
        in_specs=[pl.BlockSpec((1, N_DIR, lw, LANES), lambda j, i: (j, 0, 0, 0))],
        out_specs=pl.BlockSpec((1, S5_TOEP_ROWS * LANES, lw), lambda j, i: (j, i, 0)),
        out_shape=jax.ShapeDtypeStruct((nb, lw, lw), BF16),
        compiler_params=_params(("parallel", "parallel")),
        name="s5_toeplitz",
    )(kbd)
    return w_in, w_t, w_m, lam


def _chunk_rows(u_ref, nct):
    return jnp.concatenate([u_ref[0, pl.ds(s, nct, stride=S5_CHUNK), :] for s in range(S5_CHUNK)],
                           axis=1).astype(BF16)


def _s5_in_kernel(uc_ref, ul_ref, w_ref, *z_refs, ncc, ncl):
    v = jnp.concatenate([_chunk_rows(uc_ref, ncc), _chunk_rows(ul_ref, ncl)], axis=0)
    z = _dot(v, w_ref[0])
    wd = z_refs[0].shape[-1]
    for k in range(4):
        z_refs[k][...] = z[:ncc, k * wd:(k + 1) * wd]
        z_refs[4 + k][...] = z[ncc:, k * wd:(k + 1) * wd]


def _s5_chunk_in(u_c, u_l, w_in):
    b, tc, _ = u_c.shape
    t = u_l.shape[1]
    ncc, ncl = tc // S5_CHUNK, t // S5_CHUNK
    wd = S5_BLK * S5_STATE
    out = lambda rows: pl.BlockSpec((rows, wd), lambda j, i: (i, j))
    sds = lambda rows: jax.ShapeDtypeStruct((b * rows, S5_NBLK * wd), F32)
    outs = pl.pallas_call(
        functools.partial(_s5_in_kernel, ncc=ncc, ncl=ncl),
        grid=(S5_NBLK, b),
        in_specs=[pl.BlockSpec((1, tc, LANES), lambda j, i: (i, 0, j)),
                  pl.BlockSpec((1, t, LANES), lambda j, i: (i, 0, j)),
                  pl.BlockSpec((1,) + w_in.shape[1:], lambda j, i: (j, 0, 0))],
        out_specs=[out(ncc)] * 4 + [out(ncl)] * 4,
        out_shape=[sds(ncc)] * 4 + [sds(ncl)] * 4,
        compiler_params=_params(("parallel", "parallel")),
        name="s5_chunk_in",
    )(u_c, u_l, w_in)
    return list(outs[:4]), list(outs[4:])


SUBLANES = 8


def _cmul(ar, ai, br, bi):
    return ar * br - ai * bi, ar * bi + ai * br


def _s5_scan_tables(lr, li, row, rev):
    pw = [(lr, li)]
    for _ in range(SUBLANES - 1):
        pw.append(_cmul(pw[-1][0], pw[-1][1], lr, li))

    def by_row(power_of):
        tr, ti = jnp.zeros(row.shape, F32), jnp.zeros(row.shape, F32)
        for i in range(SUBLANES):
            k = power_of(i)
            if k:
                tr, ti = jnp.where(row == i, pw[k - 1][0], tr), jnp.where(row == i, pw[k - 1][1], ti)
        return tr, ti

    carry_w = by_row(lambda i: SUBLANES - i if rev else i + 1)
    steps = [by_row(lambda i, d=d: d if ((i <= SUBLANES - 1 - d) if rev else (i >= d)) else 0)
             for d in (1, 2, 4)]
    return carry_w, steps


def _s5_scan_tile(zr, zi, cr, ci, carry_w, steps, row, rev):
    yr, yi = zr, zi
    for d, (tr, ti) in zip((1, 2, 4), steps):
        sh = SUBLANES - d if rev else d
        ar, ai = _cmul(tr, ti, pltpu.roll(yr, sh, 0), pltpu.roll(yi, sh, 0))
        yr, yi = yr + ar, yi + ai
    ar, ai = _cmul(carry_w[0], carry_w[1], cr, ci)
    xr, xi = yr + ar, yi + ai
    sh, edge, last = (SUBLANES - 1, SUBLANES - 1, 0) if rev else (1, 0, SUBLANES - 1)
    er = jnp.where(row == edge, cr, pltpu.roll(xr, sh, 0))
    ei = jnp.where(row == edge, ci, pltpu.roll(xi, sh, 0))
    return er, ei, xr[last:last + 1], xi[last:last + 1]


def _s5_scan_kernel(cfr_ref, cfi_ref, crr_ref, cri_ref, zfr_ref, zfi_ref, zrr_ref, zri_ref, lam_ref,
                    xfr_ref, xfi_ref, xrr_ref, xri_ref, *, ncc, ncl, nb):
    tl = zfr_ref.shape[-1]
    row = lax.broadcasted_iota(jnp.int32, (SUBLANES, tl), 0)
    cw_f, st_f = _s5_scan_tables(lam_ref[0, 0:1, :], lam_ref[1, 0:1, :], row, False)
    cw_r, st_r = _s5_scan_tables(lam_ref[0, 1:2, :], lam_ref[1, 1:2, :], row, True)

    def make_step(zf, zr, nc, outs):
        nt = nc // SUBLANES

        def step(t, carry):
            new = []
            for b in range(nb):
                cfr, cfi, crr, cri = carry[b]
                rows = pl.ds(pl.multiple_of(b * nc + t * SUBLANES, SUBLANES), SUBLANES)
                er, ei, cfr, cfi = _s5_scan_tile(zf[0][rows, :], zf[1][rows, :], cfr, cfi, cw_f, st_f, row, False)
                if outs is not None:
                    outs[0][rows, :], outs[1][rows, :] = er, ei
                rows = pl.ds(pl.multiple_of(b * nc + (nt - 1 - t) * SUBLANES, SUBLANES), SUBLANES)
                er, ei, crr, cri = _s5_scan_tile(zr[0][rows, :], zr[1][rows, :], crr, cri, cw_r, st_r, row, True)
                if outs is not None:
                    outs[2][rows, :], outs[3][rows, :] = er, ei
                new.append((cfr, cfi, crr, cri))
            return tuple(new)

        return nt, step

    zero = jnp.zeros((1, tl), F32)
    carry = tuple((zero, zero, zero, zero) for _ in range(nb))
    nt, step = make_step((cfr_ref, cfi_ref), (crr_ref, cri_ref), ncc, None)
    carry = lax.fori_loop(0, nt, step, carry)
    nt, step = make_step((zfr_ref, zfi_ref), (zrr_ref, zri_ref), ncl, (xfr_ref, xfi_ref, xrr_ref, xri_ref))
    lax.fori_loop(0, nt, step, carry)


def _s5_state_scan(z_ctx, z_lat, lam, *, nb, tl):
    rc, w = z_ctx[0].shape
    rl = z_lat[0].shape[0]
    cb = pl.BlockSpec((rc, tl), lambda j: (0, j))
    zb = pl.BlockSpec((rl, tl), lambda j: (0, j))
    return pl.pallas_call(
        functools.partial(_s5_scan_kernel, ncc=rc // nb, ncl=rl // nb, nb=nb),
        grid=(w // tl,),
        in_specs=[cb] * 4 + [zb] * 4 + [pl.BlockSpec((2, N_DIR, tl), lambda j: (0, 0, j))],
        out_specs=[zb] * 4,
        out_shape=[jax.ShapeDtypeStruct((rl, w), F32)] * 4,
        compiler_params=_params(("parallel",)),
        name="s5_state_scan",
    )(*z_ctx, *z_lat, lam)


def _s5_out_kernel(u_ref, xfr_ref, xfi_ref, xrr_ref, xri_ref, wt_ref, wm_ref, y_ref, *, nct):
    x = jnp.concatenate([r[...].astype(BF16) for r in (xfr_ref, xfi_ref, xrr_ref, xri_ref)], axis=1)
    y = _dot(_chunk_rows(u_ref, nct), wt_ref[0]) + _dot(x, wm_ref[0])
    for s in range(S5_CHUNK):
        y_ref[0, pl.ds(s, nct, stride=S5_CHUNK), :] = y[:, s * LANES:(s + 1) * LANES]


def _s5_chunk_out(u, x4, w_t, w_m):
    b, t, _ = u.shape
    nct = t // S5_CHUNK
    wd = S5_BLK * S5_STATE
    xb = pl.BlockSpec((nct, wd), lambda j, i: (i, j))
    ub = pl.BlockSpec((1, t, LANES), lambda j, i: (i, 0, j))
    op = lambda a: pl.BlockSpec((1,) + a.shape[1:], lambda j, i: (j, 0, 0))
    return pl.pallas_call(
        functools.partial(_s5_out_kernel, nct=nct),
        grid=(S5_NBLK, b),
        in_specs=[ub] + [xb] * 4 + [op(w_t), op(w_m)],
        out_specs=ub,
        out_shape=jax.ShapeDtypeStruct(u.shape, F32),
        compiler_params=_params(("parallel", "parallel")),
        name="s5_chunk_out",
    )(u, *x4, w_t, w_m)


def _gelu_tanh(x):
    return 0.5 * x * (1.0 + jnp.tanh(0.7978845608028654 * (x + 0.044715 * x * x * x)))


def _merge_kernel(of_ref, or_ref, z_ref, ys_ref, u_ref, g_ref, x_ref, mod_ref,
                  dnw_ref, wa_ref, dsk_ref, wglu_ref, bglu_ref, wb_ref, wo_ref, n2w_ref,
                  xl_ref, h2_ref, *, d):
    o = of_ref[0].astype(F32) + or_ref[0].astype(F32)
    z = z_ref[0].astype(F32)
    heads = []
    for h in range(DN_HEADS):
        sl = slice(h * DN_HEAD_DIM, (h + 1) * DN_HEAD_DIM)
        oh = o[:, sl]
        on = oh * lax.rsqrt(jnp.mean(oh * oh, axis=-1, keepdims=True) + RMS_EPS) * dnw_ref[...]
        heads.append((on * _silu(z[:, sl])).astype(BF16))
    ya = _dot(jnp.concatenate(heads, axis=1), wa_ref[...])
    ys = ys_ref[0] + dsk_ref[...] * u_ref[0]
    zz = _dot(_gelu_tanh(ys).astype(BF16), wglu_ref[...]) + bglu_ref[...]
    yb = _dot((zz[:, :S5_WIDTH] * jax.nn.sigmoid(zz[:, S5_WIDTH:])).astype(BF16), wb_ref[...])
    gates = g_ref[0].astype(F32)
    mix = jax.nn.sigmoid(gates[:, :d]) * ya + jax.nn.sigmoid(gates[:, d:]) * yb
    xl = x_ref[0] + mod_ref[0, :, 2 * d:3 * d] * _dot(mix.astype(BF16), wo_ref[...])
    xl_ref[0] = xl
    hn = xl * lax.rsqrt(jnp.mean(xl * xl, axis=-1, keepdims=True) + RMS_EPS) * n2w_ref[...]
    h2_ref[0] = (hn * (1.0 + mod_ref[0, :, 4 * d:5 * d]) + mod_ref[0, :, 3 * d:4 * d]).astype(BF16)


def _mix_merge(o_f, o_r, z, ys, u, gates, x, mods, dn_norm_w, w_a_out, s5_d, w_glu, b_glu,
               w_b_out, w_o, norm2_w, *, tm):
    b, t, d = x.shape
    tok = lambda n: pl.BlockSpec((1, tm, n), lambda i, j: (i, j, 0))
    consts = [dn_norm_w.reshape(1, -1), w_a_out, s5_d.reshape(1, -1), w_glu, b_glu.reshape(1, -1),
              w_b_out, w_o, norm2_w.reshape(1, -1)]
    return pl.pallas_call(
        functools.partial(_merge_kernel, d=d),
        grid=(b, t // tm),
        in_specs=[tok(o_f.shape[-1]), tok(o_r.shape[-1]), tok(z.shape[-1]), tok(ys.shape[-1]),
                  tok(u.shape[-1]), tok(gates.shape[-1]), tok(d),
                  pl.BlockSpec((1, 1, mods.shape[-1]), lambda i, j: (i, 0, 0))]
        + [_resident(c.shape) for c in consts],
        out_specs=[tok(d), tok(d)],
        out_shape=[jax.ShapeDtypeStruct((b, t, d), F32), jax.ShapeDtypeStruct((b, t, d), BF16)],
        compiler_params=_params(("parallel", "parallel")),
        name="mix_merge",
    )(o_f, o_r, z, ys, u, gates, x, mods, *consts)


FFN_ROWS = 8
FFN_CB = 256


FFN_DOWN_GROUP = 4
FFN_AHEAD = 2


def _conv_row(e_ref, slot, part, r, cw):
    rows = [e_ref[slot, part, (r + i) * GRID_W:(r + i + 1) * GRID_W, :] for i in range(3)]
    taps = [(rows[0] * cw[j:j + 1, :] + rows[1] * cw[3 + j:4 + j, :] + rows[2] * cw[6 + j:7 + j, :]).astype(F32)
            for j in range(3)]
    left = pltpu.roll(taps[0], 1, 0)
    right = pltpu.roll(taps[2], GRID_W - 1, 0)
    sub = lax.broadcasted_iota(jnp.int32, (SUBLANES, left.shape[1]), 0)
    left = jnp.concatenate([jnp.where(sub == 0, 0.0, left[:SUBLANES]), left[SUBLANES:]], axis=0)
    right = jnp.concatenate([right[:-SUBLANES], jnp.where(sub == SUBLANES - 1, 0.0, right[-SUBLANES:])], axis=0)
    return left + taps[1] + right


def _ffn_kernel(h_ref, hp_ref, hn_ref, xl_ref, mod_ref, wup_ref, cw_ref, wd_ref, nfw_ref,
                o_ref, hext_ref, e_ref, act_ref, acc_ref, *, d, dff, nt):
    t = pl.program_id(1)
    n_out = h_ref.shape[1]
    hext_ref[0:GRID_W] = jnp.where(t == 0, jnp.zeros_like(hp_ref[0]), hp_ref[0])
    hext_ref[GRID_W:GRID_W + n_out] = h_ref[0]
    hext_ref[GRID_W + n_out:] = jnp.where(t == nt - 1, jnp.zeros_like(hn_ref[0]), hn_ref[0])
    ncb = dff // FFN_CB

    def up(k, slot):
        e_ref[slot, 0] = _dot(hext_ref[...], wup_ref[:, k * FFN_CB:(k + 1) * FFN_CB]).astype(BF16)
        e_ref[slot, 1] = _dot(hext_ref[...], wup_ref[:, dff + k * FFN_CB:dff + (k + 1) * FFN_CB]).astype(BF16)

    for k in range(FFN_AHEAD):
        up(k, k)
    done = 0
    for k in range(ncb):
        slot = k % (FFN_AHEAD + 1)
        if k + FFN_AHEAD < ncb:
            up(k + FFN_AHEAD, (k + FFN_AHEAD) % (FFN_AHEAD + 1))
        gs = slice(k * FFN_CB, (k + 1) * FFN_CB)
        cg = cw_ref[:, gs]
        cv = cw_ref[:, dff + k * FFN_CB:dff + (k + 1) * FFN_CB]
        for r in range(n_out // GRID_W):
            gate = _conv_row(e_ref, slot, 0, r, cg)
            val = _conv_row(e_ref, slot, 1, r, cv)
            act_ref[r * GRID_W:(r + 1) * GRID_W, gs] = (_silu(gate) * val).astype(BF16)
        if (k + 1) % FFN_DOWN_GROUP == 0 or k == ncb - 1:
            ks = slice(done * FFN_CB, (k + 1) * FFN_CB)
            part = _dot(act_ref[:, ks], wd_ref[ks, :])
            if done == 0:
                acc_ref[...] = part
            else:
                acc_ref[...] += part
            done = k + 1
    xo = xl_ref[0] + mod_ref[0, :, 5 * d:6 * d] * acc_ref[...]
    o_ref[0] = xo * lax.rsqrt(jnp.mean(xo * xo, axis=-1, keepdims=True) + RMS_EPS) * nfw_ref[...]


def _conv_ffn(h2, xl, mods, w_up, conv_w, w_down, norm_f_w):
    b, t, d = xl.shape
    dff = w_down.shape[0]
    tm = FFN_ROWS * GRID_W
    nt = t // tm
    nrow = t // GRID_W
    cw = conv_w.reshape(9, 2 * dff).astype(BF16)
    tok = lambda: pl.BlockSpec((1, tm, d), lambda i, j: (i, j, 0))
    return pl.pallas_call(
        functools.partial(_ffn_kernel, d=d, dff=dff, nt=nt),
        grid=(b, nt),
        in_specs=[tok(),
                  pl.BlockSpec((1, GRID_W, d), lambda i, j: (i, jnp.maximum(j * FFN_ROWS - 1, 0), 0)),
                  pl.BlockSpec((1, GRID_W, d), lambda i, j: (i, jnp.minimum((j + 1) * FFN_ROWS, nrow - 1), 0)),
                  tok(),
                  pl.BlockSpec((1, 1, mods.shape[-1]), lambda i, j: (i, 0, 0)),
                  _resident(w_up.shape), _resident(cw.shape), _resident(w_down.shape), _resident((1, d))],
        out_specs=tok(),
        out_shape=jax.ShapeDtypeStruct((b, t, d), F32),
        scratch_shapes=[pltpu.VMEM((tm + 2 * GRID_W, d), BF16),
                        pltpu.VMEM((FFN_AHEAD + 1, 2, tm + 2 * GRID_W, FFN_CB), BF16),
                        pltpu.VMEM((tm, dff), BF16),
                        pltpu.VMEM((tm, d), F32)],
        compiler_params=_params(("parallel", "parallel")),
        name="conv_ffn",
    )(h2, h2, h2, xl, mods, w_up.astype(BF16), cw, w_down.astype(BF16), norm_f_w.reshape(1, d))


def _lane_row(a):
    a = a.reshape(-1).astype(F32)
    return jnp.pad(a, (0, LANES - a.shape[0])).reshape(1, LANES)


def _pad_cols(w):
    return jnp.pad(w, ((0, 0), (0, LANES - w.shape[1])))


def kernel(x, c, ctx, c_ctx, w_ada, b_ada, norm1_w, w_in, dn_conv_w, dn_a_log, dn_dt_bias, dn_norm_w,
           w_a_out, s5_a_re, s5_a_im, s5_log_step, s5_b_re, s5_b_im, s5_c_re, s5_c_im, s5_d, w_glu,
           b_glu, w_b_out, w_o, norm2_w, w_up, ffn_conv_w, w_down, norm_f_w):
    assert w_ada.shape[0] == 1, "single-layer block"
    b, t, d = x.shape
    tc = ctx.shape[1]
    nh = N_DIR * DN_HEADS
    assert b < 8, "batch rows and the context row share one 8-row modulation block"

    c_rows = jnp.zeros((8, d), F32).at[:b].set(c).at[b].set(c_ctx)
    mods = _modulation(c_rows, w_ada[0], b_ada[0]).reshape(8, 1, N_MOD * d)

    w = w_in[0]
    o_z, o_b, o_a = 3 * DN_WIDTH, 4 * DN_WIDTH, 4 * DN_WIDTH + nh
    o_u = o_a + nh
    o_g = o_u + S5_WIDTH
    wqkv = w[:, :o_z].astype(BF16)
    wz = w[:, o_z:o_b].astype(BF16)
    wba = jnp.concatenate([_pad_cols(w[:, o_b:o_a]), _pad_cols(w[:, o_a:o_u])], axis=1).astype(BF16)
    wu = w[:, o_u:o_g].astype(BF16)
    wg = w[:, o_g:].astype(BF16)

    qkv_l, ba_l, u_l, z_l, gates_l = _in_proj(x, mods, lambda i: i, norm1_w[0], wqkv, wba, wu, wz, wg, tm=512)
    qkv_c, ba_c, u_c = _in_proj(ctx, mods, lambda i: b, norm1_w[0], wqkv, wba, wu, tm=tc)

    alog_row, dtb_row = _lane_row(dn_a_log[0]), _lane_row(dn_dt_bias[0])
    prep_c = _delta_prep(qkv_c, ba_c, dn_conv_w[0], alog_row, dtb_row, tm=tc)
    prep_l = _delta_prep(qkv_l, ba_l, dn_conv_w[0], alog_row, dtb_row, tm=512)
    s0 = jnp.zeros((b, nh, DN_HEAD_DIM, DN_HEAD_DIM), F32)
    _, _, s_ctx = _delta_scan(*_delta_wy(*prep_c, ts=tc), s0, ts=128)
    o_f, o_r, _ = _delta_scan(*_delta_wy(*prep_l, ts=256), s_ctx, ts=128)

    w_s5in, w_s5t, w_s5m, lam = _s5_operators(s5_a_re[0], s5_a_im[0], s5_log_step[0], s5_b_re[0],
                                              s5_b_im[0], s5_c_re[0], s5_c_im[0])
    zs_c, zs_l = _s5_chunk_in(u_c, u_l, w_s5in)
    x_l = _s5_state_scan(zs_c, zs_l, lam, nb=b, tl=256)
    y_s5 = _s5_chunk_out(u_l, x_l, w_s5t, w_s5m)

    xl, h2 = _mix_merge(o_f, o_r, z_l, y_s5, u_l, gates_l, x, mods, dn_norm_w[0], w_a_out[0].astype(BF16),
                        s5_d[0], w_glu[0].astype(BF16), b_glu[0], w_b_out[0].astype(BF16),
                        w_o[0].astype(BF16), norm2_w[0], tm=512)
    return _conv_ffn(h2, xl, mods, w_up[0], ffn_conv_w[0], w_down[0], norm_f_w)
```

```python
import functools

import jax
import jax.numpy as jnp
from jax import lax
from jax.experimental import pallas as pl
from jax.experimental.pallas import tpu as pltpu

F32 = jnp.float32
BF16 = jnp.bfloat16

GRID_W = 64
N_DIR = 2
DN_HEADS = 4
DN_HEAD_DIM = 128
DN_WIDTH = DN_HEADS * DN_HEAD_DIM
DN_CHUNK = 64
S5_WIDTH = 512
S5_GROUP = 16
S5_GROUPS = S5_WIDTH // S5_GROUP
S5_STATE = 64
S5_CHUNK = 16
N_MOD = 6
RMS_EPS = 1e-6
L2_EPS = 1e-6
LANES = 128
S5_BLK = LANES // S5_GROUP
S5_NBLK = S5_GROUPS // S5_BLK
VMEM_LIMIT = 56 * 1024 * 1024


def _dot(a, b):
    return jnp.dot(a, b, preferred_element_type=F32)


def _dot_f32(a, b):
    return jnp.dot(a, b, preferred_element_type=F32, precision=lax.Precision.HIGHEST)


def _silu(x):
    return x * jax.nn.sigmoid(x)


def _softplus(x):
    return jnp.maximum(x, 0.0) + jnp.log(1.0 + jnp.exp(-jnp.abs(x)))


def _params(sem, vmem=VMEM_LIMIT, flags=None):
    return pltpu.CompilerParams(dimension_semantics=sem, vmem_limit_bytes=vmem, flags=flags)


def _resident(shape):
    nd = len(shape)
    return pl.BlockSpec(shape, lambda *_: (0,) * nd, pipeline_mode=pl.Buffered(1))


def _mod_kernel(c_ref, w_ref, b_ref, o_ref):
    sc = _silu(c_ref[...])
    o_ref[...] = _dot(sc.astype(BF16), w_ref[...].astype(BF16)) + b_ref[...]


def _modulation(c_rows, w_ada, b_ada):
    d, n = w_ada.shape
    tn = n // 4
    return pl.pallas_call(
        _mod_kernel,
        grid=(n // tn,),
        in_specs=[pl.BlockSpec(c_rows.shape, lambda j: (0, 0)),
                  pl.BlockSpec((d, tn), lambda j: (0, j)),
                  pl.BlockSpec((1, tn), lambda j: (0, j))],
        out_specs=pl.BlockSpec((c_rows.shape[0], tn), lambda j: (0, j)),
        out_shape=jax.ShapeDtypeStruct((c_rows.shape[0], n), F32),
        compiler_params=_params(("arbitrary",)),
        name="adaln_mod",
    )(c_rows, w_ada, b_ada.reshape(1, n))


def _inproj_kernel(x_ref, mod_ref, nw_ref, wqkv_ref, wba_ref, wu_ref, wz_ref, wg_ref,
                   qkv_ref, ba_ref, u_ref, z_ref, g_ref, *, d):
    x = x_ref[0]
    ms = jnp.mean(x * x, axis=-1, keepdims=True)
    h = x * lax.rsqrt(ms + RMS_EPS) * nw_ref[...]
    shift = mod_ref[0, :, 0:d]
    scale = mod_ref[0, :, d:2 * d]
    hb = (h * (1.0 + scale) + shift).astype(BF16)
    qkv_ref[0] = _dot(hb, wqkv_ref[...]).astype(BF16)
    ba_ref[0] = _dot(hb, wba_ref[...])
    u_ref[0] = _dot(hb, wu_ref[...])
    if z_ref is not None:
        z_ref[0] = _dot(hb, wz_ref[...]).astype(BF16)
        g_ref[0] = _dot(hb, wg_ref[...]).astype(BF16)


def _inproj_ctx_kernel(x_ref, mod_ref, nw_ref, wqkv_ref, wba_ref, wu_ref,
                       qkv_ref, ba_ref, u_ref, *, d):
    _inproj_kernel(x_ref, mod_ref, nw_ref, wqkv_ref, wba_ref, wu_ref, None, None,
                   qkv_ref, ba_ref, u_ref, None, None, d=d)


def _in_proj(x, mods, mod_row0, norm_w, wqkv, wba, wu, wz=None, wg=None, *, tm):
    b, t, d = x.shape
    full = wz is not None
    tok = lambda n: pl.BlockSpec((1, tm, n), lambda i, j: (i, j, 0))
    in_specs = [tok(d),
                pl.BlockSpec((1, 1, mods.shape[-1]), lambda i, j: (mod_row0(i), 0, 0)),
                _resident((1, d)), _resident(wqkv.shape), _resident(wba.shape),
                _resident(wu.shape)]
    args = [x, mods, norm_w.reshape(1, d), wqkv, wba, wu]
    out_specs = [tok(wqkv.shape[1]), tok(wba.shape[1]), tok(wu.shape[1])]
    out_shape = [jax.ShapeDtypeStruct((b, t, wqkv.shape[1]), BF16),
                 jax.ShapeDtypeStruct((b, t, wba.shape[1]), F32),
                 jax.ShapeDtypeStruct((b, t, wu.shape[1]), F32)]
    if full:
        in_specs += [_resident(wz.shape), _resident(wg.shape)]
        args += [wz, wg]
        out_specs += [tok(wz.shape[1]), tok(wg.shape[1])]
        out_shape += [jax.ShapeDtypeStruct((b, t, wz.shape[1]), BF16),
                      jax.ShapeDtypeStruct((b, t, wg.shape[1]), BF16)]
    body = functools.partial(_inproj_kernel if full else _inproj_ctx_kernel, d=d)
    return pl.pallas_call(
        body, grid=(b, t // tm), in_specs=in_specs, out_specs=out_specs, out_shape=out_shape,
        compiler_params=_params(("parallel", "parallel")),
        name="in_proj" if full else "in_proj_ctx",
    )(*args)


HALO = 16


def _delta_inputs(x_ref, xp_ref, xn_ref, ba_ref, cw_ref, alog_ref, dtb_ref, *, tm, nt):
    t = pl.program_id(1)
    x = x_ref[0].astype(F32)
    prow = jnp.where(t == 0, 0.0, xp_ref[0, HALO - 1:HALO, :].astype(F32))
    nrow = jnp.where(t == nt - 1, 0.0, xn_ref[0, 0:1, :].astype(F32))
    sub = lax.broadcasted_iota(jnp.int32, (SUBLANES, x.shape[1]), 0)
    xprev = pltpu.roll(x, 1, 0)
    xnext = pltpu.roll(x, tm - 1, 0)
    xprev = jnp.concatenate([jnp.where(sub == 0, prow, xprev[:SUBLANES]), xprev[SUBLANES:]], axis=0)
    xnext = jnp.concatenate([xnext[:-SUBLANES], jnp.where(sub == SUBLANES - 1, nrow, xnext[-SUBLANES:])], axis=0)
    y = _silu(xprev * cw_ref[0:1, :] + x * cw_ref[1:2, :] + xnext * cw_ref[2:3, :])
    q, k, v = [], [], []
    for h in range(DN_HEADS):
        qh = y[:, h * DN_HEAD_DIM:(h + 1) * DN_HEAD_DIM]
        kh = y[:, DN_WIDTH + h * DN_HEAD_DIM:DN_WIDTH + (h + 1) * DN_HEAD_DIM]
        qn = qh * lax.rsqrt(jnp.sum(qh * qh, axis=-1, keepdims=True) + L2_EPS)
        kn = kh * lax.rsqrt(jnp.sum(kh * kh, axis=-1, keepdims=True) + L2_EPS)
        q.append((qn * (DN_HEAD_DIM ** -0.5)).astype(BF16))
        k.append(kn.astype(BF16))
        v.append(y[:, 2 * DN_WIDTH + h * DN_HEAD_DIM:2 * DN_WIDTH + (h + 1) * DN_HEAD_DIM].astype(BF16))
    ba = ba_ref[0]
    beta = jax.nn.sigmoid(ba[:, 0:LANES])
    g = -jnp.exp(alog_ref[...]) * _softplus(ba[:, LANES:2 * LANES] + dtb_ref[...])
    return q, k, v, beta, g


def _bmm(a, b):
    return jnp.einsum('nik,nkj->nij', a, b, preferred_element_type=F32)


def _bmm_nt(a, b):
    return jnp.einsum('nik,njk->nij', a, b, preferred_element_type=F32)


def _dwy_kernel(x_ref, xp_ref, xn_ref, ba_ref, cw_ref, alog_ref, dtb_ref,
                u_ref, w_ref, qg_ref, a_ref, kdt_ref, eg_ref, *, nc, nt):
    c, dk = DN_CHUNK, DN_HEAD_DIM
    ts = nc * c
    q_h, k_h, v_h, beta, g = _delta_inputs(x_ref, xp_ref, xn_ref, ba_ref, cw_ref, alog_ref, dtb_ref,
                                           tm=ts, nt=nt)
    ii = lax.broadcasted_iota(jnp.int32, (c, c), 0)
    jj = lax.broadcasted_iota(jnp.int32, (c, c), 1)
    ti = lax.broadcasted_iota(jnp.int32, (ts, ts), 0)
    tj = lax.broadcasted_iota(jnp.int32, (ts, ts), 1)
    same_chunk = (ti // c) == (tj // c)
    blocks = [(ch, h) for ch in range(nc) for h in range(DN_HEADS)]
    k_l = [k_h[h][ch * c:(ch + 1) * c] for ch, h in blocks]
    q_l = [q_h[h][ch * c:(ch + 1) * c] for ch, h in blocks]
    v_l = [v_h[h][ch * c:(ch + 1) * c] for ch, h in blocks]
    kkqk = _bmm_nt(jnp.stack([jnp.concatenate([k_, q_], axis=0) for k_, q_ in zip(k_l, q_l)]),
                   jnp.stack(k_l))
    a_pad = jnp.zeros((c, dk - c), BF16)
    neg_l, rhs_l, kd_l, dst = [], [], [], []
    for r in range(N_DIR):
        incl = ii <= jj if r else ii >= jj
        strict = ii < jj if r else ii > jj
        tri = (same_chunk & (ti <= tj if r else ti >= tj)).astype(F32)
        g_cum = _dot_f32(tri, g)
        g_cum_t = g_cum.T
        for n_, (ch, h) in enumerate(blocks):
            s = r * DN_HEADS + h
            rs = slice(ch * c, (ch + 1) * c)
            ls = slice(s * dk, (s + 1) * dk)
            g_c = g_cum[rs, s:s + 1]
            g_r = g_cum_t[s:s + 1, rs]
            g_end = g_c[0:1] if r else g_c[c - 1:c]
            b_c = beta[rs, s:s + 1]
            decay = jnp.where(incl, jnp.exp(jnp.where(incl, g_c - g_r, 0.0)), 0.0)
            neg_l.append(jnp.where(strict, kkqk[n_, :c] * (-b_c) * decay, 0.0))
            eg = jnp.exp(g_c)
            kf = k_l[n_].astype(F32)
            rhs_l.append(jnp.concatenate([(v_l[n_].astype(F32) * b_c).astype(BF16),
                                          (kf * (b_c * eg)).astype(BF16)], axis=1))
            qg_ref[0, rs, ls] = (q_l[n_].astype(F32) * eg).astype(BF16)
            a_ref[0, rs, ls] = jnp.concatenate([(kkqk[n_, c:] * decay).astype(BF16), a_pad], axis=1)
            kd_l.append((kf * jnp.exp(g_end - g_c)).astype(BF16))
            eg_ref[0, ch, s:s + 1, :] = jnp.broadcast_to(jnp.exp(g_end), (1, LANES))
            dst.append((rs, ls, ch, s))
    eye_k = (lax.broadcasted_iota(jnp.int32, (dk, dk), 0)
             == lax.broadcasted_iota(jnp.int32, (dk, dk), 1)).astype(BF16)
    kdt = _bmm_nt(jnp.broadcast_to(eye_k, (len(kd_l), dk, dk)), jnp.stack(kd_l))
    for n_, (_, _, ch, s) in enumerate(dst):
        kdt_ref[0, ch, s] = kdt[n_].astype(BF16)
    wi = lax.broadcasted_iota(jnp.int32, (c, 2 * c), 0)
    wj = lax.broadcasted_iota(jnp.int32, (c, 2 * c), 1)
    right = wj >= c
    eye_r = (wj - c == wi).astype(F32)
    a = jnp.stack(neg_l)
    a_wide = jnp.concatenate([a, jnp.zeros_like(a)], axis=2)
    ps = _bmm(a.astype(BF16), (a_wide + eye_r).astype(BF16)) + eye_r
    m = 2
    while m < c:
        ps = _bmm(ps[:, :, :c].astype(BF16), ps.astype(BF16)) + jnp.where(right, ps, 0.0)
        m *= 2
    rhs = jnp.stack(rhs_l)
    sol = _bmm(ps.astype(BF16), jnp.concatenate([jnp.zeros_like(rhs), rhs], axis=1))
    for n_, (rs, ls, _, _) in enumerate(dst):
        u_ref[0, rs, ls] = sol[n_, :, :dk].astype(BF16)
        w_ref[0, rs, ls] = sol[n_, :, dk:].astype(BF16)


def _delta_wy(qkv, ba, conv_w, alog_row, dtb_row, *, ts):
    b, t, c3 = qkv.shape
    w = c3 // 3
    nc = ts // DN_CHUNK
    ns = N_DIR * DN_HEADS
    nt = t // ts
    r = ts // HALO
    tok = lambda n: pl.BlockSpec((1, ts, n), lambda i, j: (i, j, 0))
    return pl.pallas_call(
        functools.partial(_dwy_kernel, nc=nc, nt=nt),
        grid=(b, nt),
        in_specs=[tok(c3),
                  pl.BlockSpec((1, HALO, c3), lambda i, j: (i, jnp.maximum(j * r - 1, 0), 0)),
                  pl.BlockSpec((1, HALO, c3), lambda i, j: (i, jnp.minimum((j + 1) * r, t // HALO - 1), 0)),
                  tok(ba.shape[-1]),
                  _resident(conv_w.shape), _resident(alog_row.shape), _resident(dtb_row.shape)],
        out_specs=[tok(N_DIR * w)] * 4
        + [pl.BlockSpec((1, nc, ns, DN_HEAD_DIM, DN_CHUNK), lambda i, j: (i, j, 0, 0, 0)),
           pl.BlockSpec((1, nc, ns, LANES), lambda i, j: (i, j, 0, 0))],
        out_shape=[jax.ShapeDtypeStruct((b, t, N_DIR * w), BF16)] * 4
        + [jax.ShapeDtypeStruct((b, t // DN_CHUNK, ns, DN_HEAD_DIM, DN_CHUNK), BF16),
           jax.ShapeDtypeStruct((b, t // DN_CHUNK, ns, LANES), F32)],
        compiler_params=_params(("parallel", "parallel")),
        name="delta_wy",
    )(qkv, qkv, qkv, ba, conv_w, alog_row, dtb_row)


def _dscan_kernel(uf_ref, wf_ref, qgf_ref, af_ref, kdf_ref, egf_ref,
                  ur_ref, wr_ref, qgr_ref, ar_ref, kdr_ref, egr_ref, s0_ref,
                  of_ref, or_ref, sfin_ref, s_ref, *, nc, nb):
    i = pl.program_id(0)

    @pl.when(i == 0)
    def _():
        s_ref[...] = s0_ref[...]

    c, dk = DN_CHUNK, DN_HEAD_DIM
    dirs = ((uf_ref, wf_ref, qgf_ref, af_ref, kdf_ref, egf_ref, of_ref),
            (ur_ref, wr_ref, qgr_ref, ar_ref, kdr_ref, egr_ref, or_ref))

    def chunk_step(j, carry):
        wq_l, u_l, a_l, kd_l, eg_l, s_l, dst = [], [], [], [], [], [], []
        for r, (u_ref, w_ref, qg_ref, a_ref, kd_ref, eg_ref, o_ref) in enumerate(dirs):
            ch = nc - 1 - j if r else j
            rs = pl.ds(pl.multiple_of(ch * c, c), c)
            for b in range(nb):
                for h in range(DN_HEADS):
                    hs = slice(h * dk, (h + 1) * dk)
                    s = r * DN_HEADS + h
                    wq_l.append(jnp.concatenate([w_ref[b, rs, hs], qg_ref[b, rs, hs]], axis=0))
                    u_l.append(u_ref[b, rs, hs])
                    a_l.append(a_ref[b, rs, hs][:, :c])
                    kd_l.append(kd_ref[b, ch, h])
                    eg_l.append(eg_ref[b, ch, s:s + 1, :])
                    s_l.append(s_ref[b, s])
                    dst.append((o_ref, b, rs, hs, s))
        st = jnp.stack(s_l)
        ws_qs = _bmm(jnp.stack(wq_l), st.astype(BF16))
        v_new = (jnp.stack(u_l).astype(F32) - ws_qs[:, :c]).astype(BF16)
        o = ws_qs[:, c:] + _bmm(jnp.stack(a_l), v_new)
        s_new = st * jnp.stack(eg_l) + _bmm(jnp.stack(kd_l), v_new)
        for n_, (o_ref, b, rs, hs, s) in enumerate(dst):
            o_ref[b, rs, hs] = o[n_].astype(BF16)
            s_ref[b, s] = s_new[n_]
        return carry

    lax.fori_loop(0, nc, chunk_step, 0)

    @pl.when(i == pl.num_programs(0) - 1)
    def _():
        sfin_ref[...] = s_ref[...]


def _delta_scan(u, w, qg, a, kdt, eg, s0, *, ts):
    b, t, w2 = u.shape
    wd = w2 // N_DIR
    n = t // ts
    nc = ts // DN_CHUNK
    fwd = pl.BlockSpec((b, ts, wd), lambda i: (0, i, 0))
    rev = pl.BlockSpec((b, ts, wd), lambda i: (0, n - 1 - i, 1))
    kd_f = pl.BlockSpec((b, nc, DN_HEADS) + kdt.shape[3:], lambda i: (0, i, 0, 0, 0))
    kd_r = pl.BlockSpec((b, nc, DN_HEADS) + kdt.shape[3:], lambda i: (0, n - 1 - i, 1, 0, 0))
    eg_f = pl.BlockSpec((b, nc) + eg.shape[2:], lambda i: (0, i, 0, 0))
    eg_r = pl.BlockSpec((b, nc) + eg.shape[2:], lambda i: (0, n - 1 - i, 0, 0))
    out_f = pl.BlockSpec((b, ts, wd), lambda i: (0, i, 0))
    out_r = pl.BlockSpec((b, ts, wd), lambda i: (0, n - 1 - i, 0))
    return pl.pallas_call(
        functools.partial(_dscan_kernel, nc=nc, nb=b),
        grid=(n,),
        in_specs=[fwd, fwd, fwd, fwd, kd_f, eg_f, rev, rev, rev, rev, kd_r, eg_r, _resident(s0.shape)],
        out_specs=[out_f, out_r, pl.BlockSpec(s0.shape, lambda i: (0, 0, 0, 0))],
        out_shape=[jax.ShapeDtypeStruct((b, t, wd), BF16)] * 2 + [jax.ShapeDtypeStruct(s0.shape, F32)],
        scratch_shapes=[pltpu.VMEM(s0.shape, F32)],
        compiler_params=_params(("arbitrary",)),
        name="delta_scan",
    )(u, w, qg, a, kdt, eg, u, w, qg, a, kdt, eg, s0)


S5_POW_ROWS = 24


def _cexp(re, im):
    m = jnp.exp(re)
    return m * jnp.cos(im), m * jnp.sin(im)


def _s5_op_kernel(are_ref, aim_ref, ls_ref, btr_ref, bti_ref, ctr_ref, cti_ref,
                  win_ref, wm_ref, kbd_ref, lam_ref):
    l, p, n = S5_CHUNK, S5_GROUP, S5_STATE
    w = S5_BLK * n
    mask_w = (lax.broadcasted_iota(jnp.int32, (LANES, w), 0) // p
              == lax.broadcasted_iota(jnp.int32, (LANES, w), 1) // n)
    mask_c = (lax.broadcasted_iota(jnp.int32, (w, LANES), 0) // n
              == lax.broadcasted_iota(jnp.int32, (w, LANES), 1) // p)
    for r in range(N_DIR):
        are, aim = are_ref[r, 0], aim_ref[r, 0]
        dt = jnp.exp(ls_ref[r, 0])
        kk = lax.broadcasted_iota(jnp.int32, (S5_POW_ROWS, w), 0).astype(F32)
        pr, pi = _cexp(kk * (are * dt), kk * (aim * dt))
        lbr, lbi = pr[1:2], pi[1:2]
        den = are * are + aim * aim
        nr, ni = lbr - 1.0, lbi
        cr = (nr * are + ni * aim) / den
        ci = (ni * are - nr * aim) / den
        bre, bim = btr_ref[r, 0], bti_ref[r, 0]
        bmr = jnp.where(mask_w, jnp.concatenate([cr * bre - ci * bim] * S5_BLK, axis=0), 0.0)
        bmi = jnp.where(mask_w, jnp.concatenate([cr * bim + ci * bre] * S5_BLK, axis=0), 0.0)
        wr_l, wi_l = [], []
        for k in range(l):
            wr = bmr * pr[k:k + 1] - bmi * pi[k:k + 1]
            wi = bmr * pi[k:k + 1] + bmi * pr[k:k + 1]
            s = k if r else l - 1 - k
            win_ref[0, s * LANES:(s + 1) * LANES, (2 * r) * w:(2 * r + 1) * w] = wr.astype(BF16)
            win_ref[0, s * LANES:(s + 1) * LANES, (2 * r + 1) * w:(2 * r + 2) * w] = wi.astype(BF16)
            wr_l.append(wr)
            wi_l.append(wi)
        ctr = jnp.where(mask_c, ctr_ref[r, 0], 0.0)
        cti = jnp.where(mask_c, cti_ref[r, 0], 0.0)
        kbd_ref[0, r] = (_dot(jnp.concatenate(wr_l, axis=0).astype(BF16), ctr.astype(BF16))
                         - _dot(jnp.concatenate(wi_l, axis=0).astype(BF16), cti.astype(BF16)))
        zpad = jnp.zeros((LANES - S5_POW_ROWS, w), F32)
        ptr = jnp.concatenate([pr, zpad], axis=0).T
        pti = jnp.concatenate([pi, zpad], axis=0).T
        for s in range(l):
            e = l - s if r else s + 1
            er, ei = ptr[:, e:e + 1], pti[:, e:e + 1]
            wm_ref[0, (2 * r) * w:(2 * r + 1) * w, s * LANES:(s + 1) * LANES] = (ctr * er - cti * ei).astype(BF16)
            wm_ref[0, (2 * r + 1) * w:(2 * r + 2) * w, s * LANES:(s + 1) * LANES] = (
                -(ctr * ei + cti * er)).astype(BF16)
        lam_ref[0, r:r + 1, :] = pr[l:l + 1]
        lam_ref[1, r:r + 1, :] = pi[l:l + 1]


def _s5_toep_kernel(kbd_ref, wt_ref):
    l = S5_CHUNK
    k0 = (kbd_ref[0, 0, 0:LANES, :] + kbd_ref[0, 1, 0:LANES, :]).astype(BF16)
    kf = [kbd_ref[0, 0, d * LANES:(d + 1) * LANES, :].astype(BF16) for d in range(l)]
    kr = [kbd_ref[0, 1, d * LANES:(d + 1) * LANES, :].astype(BF16) for d in range(l)]
    for sp in range(l):
        for s in range(l):
            blk = kf[s - sp] if s > sp else kr[sp - s] if s < sp else k0
            wt_ref[0, sp * LANES:(sp + 1) * LANES, s * LANES:(s + 1) * LANES] = blk


def _s5_operators(a_re, a_im, log_step, b_re, b_im, c_re, c_im):
    g, n, p, l = S5_GROUPS, S5_STATE, S5_GROUP, S5_CHUNK
    nb, g8 = S5_NBLK, S5_BLK
    w = g8 * n
    lane_row = lambda a: a.reshape(N_DIR, nb, 1, w)
    ls = lane_row(jnp.repeat(log_step, n, axis=1))
    bt = lambda b: jnp.transpose(b.reshape(N_DIR, nb, g8, n, p), (0, 1, 4, 2, 3)).reshape(N_DIR, nb, p, w)
    ct = lambda c: jnp.tile(jnp.swapaxes(c, 2, 3).reshape(N_DIR, nb, w, p), (1, 1, 1, g8))
    blk = lambda r, c: pl.BlockSpec((N_DIR, 1, r, c), lambda j: (0, j, 0, 0))
    ops = lambda r, c: pl.BlockSpec((1, r, c), lambda j: (j, 0, 0))
    lw = l * LANES
    w_in, w_m, kbd, lam = pl.pallas_call(
        _s5_op_kernel,
        grid=(nb,),
        in_specs=[blk(1, w)] * 3 + [blk(p, w)] * 2 + [blk(w, LANES)] * 2,
        out_specs=[ops(lw, 2 * N_DIR * w), ops(2 * N_DIR * w, lw),
                   pl.BlockSpec((1, N_DIR, lw, LANES), lambda j: (j, 0, 0, 0)),
                   pl.BlockSpec((2, N_DIR, w), lambda j: (0, 0, j))],
        out_shape=[jax.ShapeDtypeStruct((nb, lw, 2 * N_DIR * w), BF16),
                   jax.ShapeDtypeStruct((nb, 2 * N_DIR * w, lw), BF16),
                   jax.ShapeDtypeStruct((nb, N_DIR, lw, LANES), F32),
                   jax.ShapeDtypeStruct((2, N_DIR, g * n), F32)],
        compiler_params=_params(("parallel",)),
        name="s5_params",
    )(lane_row(a_re), lane_row(a_im), ls, bt(b_re), bt(b_im), ct(c_re), ct(c_im))
    w_t = pl.pallas_call(
        _s5_toep_kernel,
        grid=(nb,),
        in_specs=[pl.BlockSpec((1, N_DIR, lw, LANES), lambda j: (j, 0, 0, 0))],
        out_specs=pl.BlockSpec((1, lw, lw), lambda j: (j, 0, 0)),
        out_shape=jax.ShapeDtypeStruct((nb, lw, lw), BF16),
        compiler_params=_params(("parallel",)),
        name="s5_toeplitz",
    )(kbd)
    return w_in, w_t, w_m, lam


def _chunk_rows(u_ref, nct):
    return jnp.concatenate([u_ref[0, pl.ds(s, nct, stride=S5_CHUNK), :] for s in range(S5_CHUNK)],
                           axis=1).astype(BF16)


def _s5_in_kernel(uc_ref, ul_ref, w_ref, *z_refs, ncc, ncl):
    v = jnp.concatenate([_chunk_rows(uc_ref, ncc), _chunk_rows(ul_ref, ncl)], axis=0)
    z = _dot(v, w_ref[0])
    wd = z_refs[0].shape[-1]
    for k in range(4):
        z_refs[k][...] = z[:ncc, k * wd:(k + 1) * wd]
        z_refs[4 + k][...] = z[ncc:, k * wd:(k + 1) * wd]


def _s5_chunk_in(u_c, u_l, w_in):
    b, tc, _ = u_c.shape
    t = u_l.shape[1]
    ncc, ncl = tc // S5_CHUNK, t // S5_CHUNK
    wd = S5_BLK * S5_STATE
    out = lambda rows: pl.BlockSpec((rows, wd), lambda j, i: (i, j))
    sds = lambda rows: jax.ShapeDtypeStruct((b * rows, S5_NBLK * wd), F32)
    outs = pl.pallas_call(
        functools.partial(_s5_in_kernel, ncc=ncc, ncl=ncl),
        grid=(S5_NBLK, b),
        in_specs=[pl.BlockSpec((1, tc, LANES), lambda j, i: (i, 0, j)),
                  pl.BlockSpec((1, t, LANES), lambda j, i: (i, 0, j)),
                  pl.BlockSpec((1,) + w_in.shape[1:], lambda j, i: (j, 0, 0))],
        out_specs=[out(ncc)] * 4 + [out(ncl)] * 4,
        out_shape=[sds(ncc)] * 4 + [sds(ncl)] * 4,
        compiler_params=_params(("parallel", "parallel")),
        name="s5_chunk_in",
    )(u_c, u_l, w_in)
    return list(outs[:4]), list(outs[4:])


SUBLANES = 8


def _cmul(ar, ai, br, bi):
    return ar * br - ai * bi, ar * bi + ai * br


def _s5_scan_tables(lr, li, row, rev):
    pw = [(lr, li)]
    for _ in range(SUBLANES - 1):
        pw.append(_cmul(pw[-1][0], pw[-1][1], lr, li))

    def by_row(power_of):
        tr, ti = jnp.zeros(row.shape, F32), jnp.zeros(row.shape, F32)
        for i in range(SUBLANES):
            k = power_of(i)
            if k:
                tr, ti = jnp.where(row == i, pw[k - 1][0], tr), jnp.where(row == i, pw[k - 1][1], ti)
        return tr, ti

    carry_w = by_row(lambda i: SUBLANES - i if rev else i + 1)
    steps = [by_row(lambda i, d=d: d if ((i <= SUBLANES - 1 - d) if rev else (i >= d)) else 0)
             for d in (1, 2, 4)]
    return carry_w, steps


def _s5_scan_tile(zr, zi, cr, ci, carry_w, steps, row, rev):
    yr, yi = zr, zi
    for d, (tr, ti) in zip((1, 2, 4), steps):
        sh = SUBLANES - d if rev else d
        ar, ai = _cmul(tr, ti, pltpu.roll(yr, sh, 0), pltpu.roll(yi, sh, 0))
        yr, yi = yr + ar, yi + ai
    ar, ai = _cmul(carry_w[0], carry_w[1], cr, ci)
    xr, xi = yr + ar, yi + ai
    sh, edge, last = (SUBLANES - 1, SUBLANES - 1, 0) if rev else (1, 0, SUBLANES - 1)
    er = jnp.where(row == edge, cr, pltpu.roll(xr, sh, 0))
    ei = jnp.where(row == edge, ci, pltpu.roll(xi, sh, 0))
    return er, ei, xr[last:last + 1], xi[last:last + 1]


def _s5_scan_kernel(cfr_ref, cfi_ref, crr_ref, cri_ref, zfr_ref, zfi_ref, zrr_ref, zri_ref, lam_ref,
                    xfr_ref, xfi_ref, xrr_ref, xri_ref, *, ncc, ncl, nb):
    tl = zfr_ref.shape[-1]
    row = lax.broadcasted_iota(jnp.int32, (SUBLANES, tl), 0)
    cw_f, st_f = _s5_scan_tables(lam_ref[0, 0:1, :], lam_ref[1, 0:1, :], row, False)
    cw_r, st_r = _s5_scan_tables(lam_ref[0, 1:2, :], lam_ref[1, 1:2, :], row, True)

    def make_step(zf, zr, nc, outs):
        nt = nc // SUBLANES

        def step(t, carry):
            new = []
            for b in range(nb):
                cfr, cfi, crr, cri = carry[b]
                rows = pl.ds(pl.multiple_of(b * nc + t * SUBLANES, SUBLANES), SUBLANES)
                er, ei, cfr, cfi = _s5_scan_tile(zf[0][rows, :], zf[1][rows, :], cfr, cfi, cw_f, st_f, row, False)
                if outs is not None:
                    outs[0][rows, :], outs[1][rows, :] = er, ei
                rows = pl.ds(pl.multiple_of(b * nc + (nt - 1 - t) * SUBLANES, SUBLANES), SUBLANES)
                er, ei, crr, cri = _s5_scan_tile(zr[0][rows, :], zr[1][rows, :], crr, cri, cw_r, st_r, row, True)
                if outs is not None:
                    outs[2][rows, :], outs[3][rows, :] = er, ei
                new.append((cfr, cfi, crr, cri))
            return tuple(new)

        return nt, step

    zero = jnp.zeros((1, tl), F32)
    carry = tuple((zero, zero, zero, zero) for _ in range(nb))
    nt, step = make_step((cfr_ref, cfi_ref), (crr_ref, cri_ref), ncc, None)
    carry = lax.fori_loop(0, nt, step, carry)
    nt, step = make_step((zfr_ref, zfi_ref), (zrr_ref, zri_ref), ncl, (xfr_ref, xfi_ref, xrr_ref, xri_ref))
    lax.fori_loop(0, nt, step, carry)


def _s5_state_scan(z_ctx, z_lat, lam, *, nb, tl):
    rc, w = z_ctx[0].shape
    rl = z_lat[0].shape[0]
    cb = pl.BlockSpec((rc, tl), lambda j: (0, j))
    zb = pl.BlockSpec((rl, tl), lambda j: (0, j))
    return pl.pallas_call(
        functools.partial(_s5_scan_kernel, ncc=rc // nb, ncl=rl // nb, nb=nb),
        grid=(w // tl,),
        in_specs=[cb] * 4 + [zb] * 4 + [pl.BlockSpec((2, N_DIR, tl), lambda j: (0, 0, j))],
        out_specs=[zb] * 4,
        out_shape=[jax.ShapeDtypeStruct((rl, w), F32)] * 4,
        compiler_params=_params(("parallel",)),
        name="s5_state_scan",
    )(*z_ctx, *z_lat, lam)


def _s5_out_kernel(u_ref, xfr_ref, xfi_ref, xrr_ref, xri_ref, wt_ref, wm_ref, y_ref, *, nct):
    x = jnp.concatenate([r[...].astype(BF16) for r in (xfr_ref, xfi_ref, xrr_ref, xri_ref)], axis=1)
    y = _dot(_chunk_rows(u_ref, nct), wt_ref[0]) + _dot(x, wm_ref[0])
    for s in range(S5_CHUNK):
        y_ref[0, pl.ds(s, nct, stride=S5_CHUNK), :] = y[:, s * LANES:(s + 1) * LANES]


def _s5_chunk_out(u, x4, w_t, w_m):
    b, t, _ = u.shape
    nct = t // S5_CHUNK
    wd = S5_BLK * S5_STATE
    xb = pl.BlockSpec((nct, wd), lambda j, i: (i, j))
    ub = pl.BlockSpec((1, t, LANES), lambda j, i: (i, 0, j))
    op = lambda a: pl.BlockSpec((1,) + a.shape[1:], lambda j, i: (j, 0, 0))
    return pl.pallas_call(
        functools.partial(_s5_out_kernel, nct=nct),
        grid=(S5_NBLK, b),
        in_specs=[ub] + [xb] * 4 + [op(w_t), op(w_m)],
        out_specs=ub,
        out_shape=jax.ShapeDtypeStruct(u.shape, F32),
        compiler_params=_params(("parallel", "parallel")),
        name="s5_chunk_out",
    )(u, *x4, w_t, w_m)


def _gelu_tanh(x):
    return 0.5 * x * (1.0 + jnp.tanh(0.7978845608028654 * (x + 0.044715 * x * x * x)))


def _merge_kernel(of_ref, or_ref, z_ref, ys_ref, u_ref, g_ref, x_ref, mod_ref,
                  dnw_ref, wa_ref, dsk_ref, wglu_ref, bglu_ref, wb_ref, wo_ref, n2w_ref,
                  xl_ref, h2_ref, *, d):
    o = of_ref[0].astype(F32) + or_ref[0].astype(F32)
    z = z_ref[0].astype(F32)
    heads = []
    for h in range(DN_HEADS):
        sl = slice(h * DN_HEAD_DIM, (h + 1) * DN_HEAD_DIM)
        oh = o[:, sl]
        on = oh * lax.rsqrt(jnp.mean(oh * oh, axis=-1, keepdims=True) + RMS_EPS) * dnw_ref[...]
        heads.append((on * _silu(z[:, sl])).astype(BF16))
    ya = _dot(jnp.concatenate(heads, axis=1), wa_ref[...])
    ys = ys_ref[0] + dsk_ref[...] * u_ref[0]
    zz = _dot(_gelu_tanh(ys).astype(BF16), wglu_ref[...]) + bglu_ref[...]
    yb = _dot((zz[:, :S5_WIDTH] * jax.nn.sigmoid(zz[:, S5_WIDTH:])).astype(BF16), wb_ref[...])
    gates = g_ref[0].astype(F32)
    mix = jax.nn.sigmoid(gates[:, :d]) * ya + jax.nn.sigmoid(gates[:, d:]) * yb
    xl = x_ref[0] + mod_ref[0, :, 2 * d:3 * d] * _dot(mix.astype(BF16), wo_ref[...])
    xl_ref[0] = xl
    hn = xl * lax.rsqrt(jnp.mean(xl * xl, axis=-1, keepdims=True) + RMS_EPS) * n2w_ref[...]
    h2_ref[0] = (hn * (1.0 + mod_ref[0, :, 4 * d:5 * d]) + mod_ref[0, :, 3 * d:4 * d]).astype(BF16)


def _mix_merge(o_f, o_r, z, ys, u, gates, x, mods, dn_norm_w, w_a_out, s5_d, w_glu, b_glu,
               w_b_out, w_o, norm2_w, *, tm):
    b, t, d = x.shape
    tok = lambda n: pl.BlockSpec((1, tm, n), lambda i, j: (i, j, 0))
    consts = [dn_norm_w.reshape(1, -1), w_a_out, s5_d.reshape(1, -1), w_glu, b_glu.reshape(1, -1),
              w_b_out, w_o, norm2_w.reshape(1, -1)]
    return pl.pallas_call(
        functools.partial(_merge_kernel, d=d),
        grid=(b, t // tm),
        in_specs=[tok(o_f.shape[-1]), tok(o_r.shape[-1]), tok(z.shape[-1]), tok(ys.shape[-1]),
                  tok(u.shape[-1]), tok(gates.shape[-1]), tok(d),
                  pl.BlockSpec((1, 1, mods.shape[-1]), lambda i, j: (i, 0, 0))]
        + [_resident(c.shape) for c in consts],
        out_specs=[tok(d), tok(d)],
        out_shape=[jax.ShapeDtypeStruct((b, t, d), F32), jax.ShapeDtypeStruct((b, t, d), BF16)],
        compiler_params=_params(("parallel", "parallel")),
        name="mix_merge",
    )(o_f, o_r, z, ys, u, gates, x, mods, *consts)


FFN_ROWS = 8
FFN_CB = 256


FFN_DOWN_GROUP = 4
FFN_AHEAD = 2


def _conv_row(e_ref, slot, part, r, cw):
    rows = [e_ref[slot, part, (r + i) * GRID_W:(r + i + 1) * GRID_W, :] for i in range(3)]
    taps = [(rows[0] * cw[j:j + 1, :] + rows[1] * cw[3 + j:4 + j, :] + rows[2] * cw[6 + j:7 + j, :]).astype(F32)
            for j in range(3)]
    left = pltpu.roll(taps[0], 1, 0)
    right = pltpu.roll(taps[2], GRID_W - 1, 0)
    sub = lax.broadcasted_iota(jnp.int32, (SUBLANES, left.shape[1]), 0)
    left = jnp.concatenate([jnp.where(sub == 0, 0.0, left[:SUBLANES]), left[SUBLANES:]], axis=0)
    right = jnp.concatenate([right[:-SUBLANES], jnp.where(sub == SUBLANES - 1, 0.0, right[-SUBLANES:])], axis=0)
    return left + taps[1] + right


def _ffn_kernel(h_ref, hp_ref, hn_ref, xl_ref, mod_ref, wup_ref, cw_ref, wd_ref, nfw_ref,
                o_ref, hext_ref, e_ref, act_ref, acc_ref, *, d, dff, nt):
    t = pl.program_id(1)
    n_out = h_ref.shape[1]
    hext_ref[0:GRID_W] = jnp.where(t == 0, jnp.zeros_like(hp_ref[0]), hp_ref[0])
    hext_ref[GRID_W:GRID_W + n_out] = h_ref[0]
    hext_ref[GRID_W + n_out:] = jnp.where(t == nt - 1, jnp.zeros_like(hn_ref[0]), hn_ref[0])
    ncb = dff // FFN_CB

    def up(k, slot):
        e_ref[slot, 0] = _dot(hext_ref[...], wup_ref[:, k * FFN_CB:(k + 1) * FFN_CB]).astype(BF16)
        e_ref[slot, 1] = _dot(hext_ref[...], wup_ref[:, dff + k * FFN_CB:dff + (k + 1) * FFN_CB]).astype(BF16)

    for k in range(FFN_AHEAD):
        up(k, k)
    done = 0
    for k in range(ncb):
        slot = k % (FFN_AHEAD + 1)
        if k + FFN_AHEAD < ncb:
            up(k + FFN_AHEAD, (k + FFN_AHEAD) % (FFN_AHEAD + 1))
        gs = slice(k * FFN_CB, (k + 1) * FFN_CB)
        cg = cw_ref[:, gs]
        cv = cw_ref[:, dff + k * FFN_CB:dff + (k + 1) * FFN_CB]
        for r in range(n_out // GRID_W):
            gate = _conv_row(e_ref, slot, 0, r, cg)
            val = _conv_row(e_ref, slot, 1, r, cv)
            act_ref[r * GRID_W:(r + 1) * GRID_W, gs] = (_silu(gate) * val).astype(BF16)
        if (k + 1) % FFN_DOWN_GROUP == 0 or k == ncb - 1:
            ks = slice(done * FFN_CB, (k + 1) * FFN_CB)
            part = _dot(act_ref[:, ks], wd_ref[ks, :])
            if done == 0:
                acc_ref[...] = part
            else:
                acc_ref[...] += part
            done = k + 1
    xo = xl_ref[0] + mod_ref[0, :, 5 * d:6 * d] * acc_ref[...]
    o_ref[0] = xo * lax.rsqrt(jnp.mean(xo * xo, axis=-1, keepdims=True) + RMS_EPS) * nfw_ref[...]


def _conv_ffn(h2, xl, mods, w_up, conv_w, w_down, norm_f_w):
    b, t, d = xl.shape
    dff = w_down.shape[0]
    tm = FFN_ROWS * GRID_W
    nt = t // tm
    nrow = t // GRID_W
    cw = conv_w.reshape(9, 2 * dff).astype(BF16)
    tok = lambda: pl.BlockSpec((1, tm, d), lambda i, j: (i, j, 0))
    return pl.pallas_call(
        functools.partial(_ffn_kernel, d=d, dff=dff, nt=nt),
        grid=(b, nt),
        in_specs=[tok(),
                  pl.BlockSpec((1, GRID_W, d), lambda i, j: (i, jnp.maximum(j * FFN_ROWS - 1, 0), 0)),
                  pl.BlockSpec((1, GRID_W, d), lambda i, j: (i, jnp.minimum((j + 1) * FFN_ROWS, nrow - 1), 0)),
                  tok(),
                  pl.BlockSpec((1, 1, mods.shape[-1]), lambda i, j: (i, 0, 0)),
                  _resident(w_up.shape), _resident(cw.shape), _resident(w_down.shape), _resident((1, d))],
        out_specs=tok(),
        out_shape=jax.ShapeDtypeStruct((b, t, d), F32),
        scratch_shapes=[pltpu.VMEM((tm + 2 * GRID_W, d), BF16),
                        pltpu.VMEM((FFN_AHEAD + 1, 2, tm + 2 * GRID_W, FFN_CB), BF16),
                        pltpu.VMEM((tm, dff), BF16),
                        pltpu.VMEM((tm, d), F32)],
        compiler_params=_params(("parallel", "parallel")),
        name="conv_ffn",
    )(h2, h2, h2, xl, mods, w_up.astype(BF16), cw, w_down.astype(BF16), norm_f_w.reshape(1, d))


def _lane_row(a):
    a = a.reshape(-1).astype(F32)
    return jnp.pad(a, (0, LANES - a.shape[0])).reshape(1, LANES)


def _pad_cols(w):
    return jnp.pad(w, ((0, 0), (0, LANES - w.shape[1])))


def kernel(x, c, ctx, c_ctx, w_ada, b_ada, norm1_w, w_in, dn_conv_w, dn_a_log, dn_dt_bias, dn_norm_w,
           w_a_out, s5_a_re, s5_a_im, s5_log_step, s5_b_re, s5_b_im, s5_c_re, s5_c_im, s5_d, w_glu,
           b_glu, w_b_out, w_o, norm2_w, w_up, ffn_conv_w, w_down, norm_f_w):
    assert w_ada.shape[0] == 1, "single-layer block"
    b, t, d = x.shape
    tc = ctx.shape[1]
    nh = N_DIR * DN_HEADS
    assert b < 8, "batch rows and the context row share one 8-row modulation block"

    c_rows = jnp.zeros((8, d), F32).at[:b].set(c).at[b].set(c_ctx)
    mods = _modulation(c_rows, w_ada[0], b_ada[0]).reshape(8, 1, N_MOD * d)

    w = w_in[0]
    o_z, o_b, o_a = 3 * DN_WIDTH, 4 * DN_WIDTH, 4 * DN_WIDTH + nh
    o_u = o_a + nh
    o_g = o_u + S5_WIDTH
    wqkv = w[:, :o_z].astype(BF16)
    wz = w[:, o_z:o_b].astype(BF16)
    wba = jnp.concatenate([_pad_cols(w[:, o_b:o_a]), _pad_cols(w[:, o_a:o_u])], axis=1).astype(BF16)
    wu = w[:, o_u:o_g].astype(BF16)
    wg = w[:, o_g:].astype(BF16)

    qkv_l, ba_l, u_l, z_l, gates_l = _in_proj(x, mods, lambda i: i, norm1_w[0], wqkv, wba, wu, wz, wg, tm=512)
    qkv_c, ba_c, u_c = _in_proj(ctx, mods, lambda i: b, norm1_w[0], wqkv, wba, wu, tm=tc)

    alog_row, dtb_row = _lane_row(dn_a_log[0]), _lane_row(dn_dt_bias[0])
    s0 = jnp.zeros((b, nh, DN_HEAD_DIM, DN_HEAD_DIM), F32)
    wy_c = _delta_wy(qkv_c, ba_c, dn_conv_w[0], alog_row, dtb_row, ts=tc)
    wy_l = _delta_wy(qkv_l, ba_l, dn_conv_w[0], alog_row, dtb_row, ts=256)
    _, _, s_ctx = _delta_scan(*wy_c, s0, ts=128)
    o_f, o_r, _ = _delta_scan(*wy_l, s_ctx, ts=128)

    w_s5in, w_s5t, w_s5m, lam = _s5_operators(s5_a_re[0], s5_a_im[0], s5_log_step[0], s5_b_re[0],
                                              s5_b_im[0], s5_c_re[0], s5_c_im[0])
    zs_c, zs_l = _s5_chunk_in(u_c, u_l, w_s5in)
    x_l = _s5_state_scan(zs_c, zs_l, lam, nb=b, tl=256)
    y_s5 = _s5_chunk_out(u_l, x_l, w_s5t, w_s5m)

    xl, h2 = _mix_merge(o_f, o_r, z_l, y_s5, u_l, gates_l, x, mods, dn_norm_w[0], w_a_out[0].astype(BF16),
                        s5_d[0], w_glu[0].astype(BF16), b_glu[0], w_b_out[0].astype(BF16),
                        w_o[0].astype(BF16), norm2_w[0], tm=512)
    return _conv_ffn(h2, xl, mods, w_up[0], ffn_conv_w[0], w_down[0], norm_f_w)
```

```python
import functools

import jax
import jax.numpy as jnp
from jax import lax
from jax.experimental import pallas as pl
from jax.experimental.pallas import tpu as pltpu

F32 = jnp.float32
BF16 = jnp.bfloat16

GRID_W = 64
N_DIR = 2
DN_HEADS = 4
DN_HEAD_DIM = 128
DN_WIDTH = DN_HEADS * DN_HEAD_DIM
DN_CHUNK = 64
S5_WIDTH = 512
S5_GROUP = 16
S5_GROUPS = S5_WIDTH // S5_GROUP
S5_STATE = 64
S5_CHUNK = 16
N_MOD = 6
RMS_EPS = 1e-6
L2_EPS = 1e-6
LANES = 128
S5_BLK = LANES // S5_GROUP
S5_NBLK = S5_GROUPS // S5_BLK
VMEM_LIMIT = 56 * 1024 * 1024


def _dot(a, b):
    return jnp.dot(a, b, preferred_element_type=F32)


def _dot_f32(a, b):
    return jnp.dot(a, b, preferred_element_type=F32, precision=lax.Precision.HIGHEST)


def _silu(x):
    return x * jax.nn.sigmoid(x)


def _softplus(x):
    return jnp.maximum(x, 0.0) + jnp.log(1.0 + jnp.exp(-jnp.abs(x)))


def _params(sem, vmem=VMEM_LIMIT, flags=None):
    return pltpu.CompilerParams(dimension_semantics=sem, vmem_limit_bytes=vmem, flags=flags)


def _resident(shape):
    nd = len(shape)
    return pl.BlockSpec(shape, lambda *_: (0,) * nd, pipeline_mode=pl.Buffered(1))


def _mod_kernel(c_ref, w_ref, b_ref, o_ref):
    sc = _silu(c_ref[...])
    o_ref[...] = _dot(sc.astype(BF16), w_ref[...].astype(BF16)) + b_ref[...]


def _modulation(c_rows, w_ada, b_ada):
    d, n = w_ada.shape
    tn = n // 4
    return pl.pallas_call(
        _mod_kernel,
        grid=(n // tn,),
        in_specs=[pl.BlockSpec(c_rows.shape, lambda j: (0, 0)),
                  pl.BlockSpec((d, tn), lambda j: (0, j)),
                  pl.BlockSpec((1, tn), lambda j: (0, j))],
        out_specs=pl.BlockSpec((c_rows.shape[0], tn), lambda j: (0, j)),
        out_shape=jax.ShapeDtypeStruct((c_rows.shape[0], n), F32),
        compiler_params=_params(("arbitrary",)),
        name="adaln_mod",
    )(c_rows, w_ada, b_ada.reshape(1, n))


def _inproj_kernel(x_ref, mod_ref, nw_ref, wqkv_ref, wba_ref, wu_ref, wz_ref, wg_ref,
                   qkv_ref, ba_ref, u_ref, z_ref, g_ref, *, d):
    x = x_ref[0]
    ms = jnp.mean(x * x, axis=-1, keepdims=True)
    h = x * lax.rsqrt(ms + RMS_EPS) * nw_ref[...]
    shift = mod_ref[0, :, 0:d]
    scale = mod_ref[0, :, d:2 * d]
    hb = (h * (1.0 + scale) + shift).astype(BF16)
    qkv_ref[0] = _dot(hb, wqkv_ref[...]).astype(BF16)
    ba_ref[0] = _dot(hb, wba_ref[...])
    u_ref[0] = _dot(hb, wu_ref[...])
    if z_ref is not None:
        z_ref[0] = _dot(hb, wz_ref[...]).astype(BF16)
        g_ref[0] = _dot(hb, wg_ref[...]).astype(BF16)


def _inproj_ctx_kernel(x_ref, mod_ref, nw_ref, wqkv_ref, wba_ref, wu_ref,
                       qkv_ref, ba_ref, u_ref, *, d):
    _inproj_kernel(x_ref, mod_ref, nw_ref, wqkv_ref, wba_ref, wu_ref, None, None,
                   qkv_ref, ba_ref, u_ref, None, None, d=d)


def _in_proj(x, mods, mod_row0, norm_w, wqkv, wba, wu, wz=None, wg=None, *, tm):
    b, t, d = x.shape
    full = wz is not None
    tok = lambda n: pl.BlockSpec((1, tm, n), lambda i, j: (i, j, 0))
    in_specs = [tok(d),
                pl.BlockSpec((1, 1, mods.shape[-1]), lambda i, j: (mod_row0(i), 0, 0)),
                _resident((1, d)), _resident(wqkv.shape), _resident(wba.shape),
                _resident(wu.shape)]
    args = [x, mods, norm_w.reshape(1, d), wqkv, wba, wu]
    out_specs = [tok(wqkv.shape[1]), tok(wba.shape[1]), tok(wu.shape[1])]
    out_shape = [jax.ShapeDtypeStruct((b, t, wqkv.shape[1]), BF16),
                 jax.ShapeDtypeStruct((b, t, wba.shape[1]), F32),
                 jax.ShapeDtypeStruct((b, t, wu.shape[1]), F32)]
    if full:
        in_specs += [_resident(wz.shape), _resident(wg.shape)]
        args += [wz, wg]
        out_specs += [tok(wz.shape[1]), tok(wg.shape[1])]
        out_shape += [jax.ShapeDtypeStruct((b, t, wz.shape[1]), BF16),
                      jax.ShapeDtypeStruct((b, t, wg.shape[1]), BF16)]
    body = functools.partial(_inproj_kernel if full else _inproj_ctx_kernel, d=d)
    return pl.pallas_call(
        body, grid=(b, t // tm), in_specs=in_specs, out_specs=out_specs, out_shape=out_shape,
        compiler_params=_params(("parallel", "parallel")),
        name="in_proj" if full else "in_proj_ctx",
    )(*args)


HALO = 16


def _delta_inputs(x_ref, xp_ref, xn_ref, ba_ref, cw_ref, alog_ref, dtb_ref, *, tm, nt):
    t = pl.program_id(1)
    x = x_ref[0].astype(F32)
    prow = jnp.where(t == 0, 0.0, xp_ref[0, HALO - 1:HALO, :].astype(F32))
    nrow = jnp.where(t == nt - 1, 0.0, xn_ref[0, 0:1, :].astype(F32))
    sub = lax.broadcasted_iota(jnp.int32, (SUBLANES, x.shape[1]), 0)
    xprev = pltpu.roll(x, 1, 0)
    xnext = pltpu.roll(x, tm - 1, 0)
    xprev = jnp.concatenate([jnp.where(sub == 0, prow, xprev[:SUBLANES]), xprev[SUBLANES:]], axis=0)
    xnext = jnp.concatenate([xnext[:-SUBLANES], jnp.where(sub == SUBLANES - 1, nrow, xnext[-SUBLANES:])], axis=0)
    y = _silu(xprev * cw_ref[0:1, :] + x * cw_ref[1:2, :] + xnext * cw_ref[2:3, :])
    q, k, v = [], [], []
    for h in range(DN_HEADS):
        qh = y[:, h * DN_HEAD_DIM:(h + 1) * DN_HEAD_DIM]
        kh = y[:, DN_WIDTH + h * DN_HEAD_DIM:DN_WIDTH + (h + 1) * DN_HEAD_DIM]
        qn = qh * lax.rsqrt(jnp.sum(qh * qh, axis=-1, keepdims=True) + L2_EPS)
        kn = kh * lax.rsqrt(jnp.sum(kh * kh, axis=-1, keepdims=True) + L2_EPS)
        q.append((qn * (DN_HEAD_DIM ** -0.5)).astype(BF16))
        k.append(kn.astype(BF16))
        v.append(y[:, 2 * DN_WIDTH + h * DN_HEAD_DIM:2 * DN_WIDTH + (h + 1) * DN_HEAD_DIM].astype(BF16))
    ba = ba_ref[0]
    beta = jax.nn.sigmoid(ba[:, 0:LANES])
    g = -jnp.exp(alog_ref[...]) * _softplus(ba[:, LANES:2 * LANES] + dtb_ref[...])
    return q, k, v, beta, g


def _bmm(a, b):
    return jnp.einsum('nik,nkj->nij', a, b, preferred_element_type=F32)


def _bmm_nt(a, b):
    return jnp.einsum('nik,njk->nij', a, b, preferred_element_type=F32)


def _dwy_kernel(x_ref, xp_ref, xn_ref, ba_ref, cw_ref, alog_ref, dtb_ref,
                u_ref, w_ref, qg_ref, a_ref, kdt_ref, eg_ref, *, nc, nt):
    c, dk = DN_CHUNK, DN_HEAD_DIM
    ts = nc * c
    q_h, k_h, v_h, beta, g = _delta_inputs(x_ref, xp_ref, xn_ref, ba_ref, cw_ref, alog_ref, dtb_ref,
                                           tm=ts, nt=nt)
    ii = lax.broadcasted_iota(jnp.int32, (c, c), 0)
    jj = lax.broadcasted_iota(jnp.int32, (c, c), 1)
    ti = lax.broadcasted_iota(jnp.int32, (ts, ts), 0)
    tj = lax.broadcasted_iota(jnp.int32, (ts, ts), 1)
    same_chunk = (ti // c) == (tj // c)
    blocks = [(ch, h) for ch in range(nc) for h in range(DN_HEADS)]
    k_l = [k_h[h][ch * c:(ch + 1) * c] for ch, h in blocks]
    q_l = [q_h[h][ch * c:(ch + 1) * c] for ch, h in blocks]
    v_l = [v_h[h][ch * c:(ch + 1) * c] for ch, h in blocks]
    kkqk = _bmm_nt(jnp.stack([jnp.concatenate([k_, q_], axis=0) for k_, q_ in zip(k_l, q_l)]),
                   jnp.stack(k_l))
    a_pad = jnp.zeros((c, dk - c), BF16)
    neg_l, rhs_l, kd_l, dst = [], [], [], []
    for r in range(N_DIR):
        incl = ii <= jj if r else ii >= jj
        strict = ii < jj if r else ii > jj
        tri = (same_chunk & (ti <= tj if r else ti >= tj)).astype(F32)
        g_cum = _dot_f32(tri, g)
        g_cum_t = g_cum.T
        for n_, (ch, h) in enumerate(blocks):
            s = r * DN_HEADS + h
            rs = slice(ch * c, (ch + 1) * c)
            ls = slice(s * dk, (s + 1) * dk)
            g_c = jnp.broadcast_to(g_cum[rs, s:s + 1], (c, dk))
            b_c = jnp.broadcast_to(beta[rs, s:s + 1], (c, dk))
            g_r = g_cum_t[s:s + 1, rs]
            g_end = g_c[0:1] if r else g_c[c - 1:c]
            decay = jnp.where(incl, jnp.exp(jnp.where(incl, g_c[:, :c] - g_r, 0.0)), 0.0)
            neg_l.append(jnp.where(strict, kkqk[n_, :c] * (-b_c[:, :c]) * decay, 0.0))
            eg = jnp.exp(g_c)
            kf = k_l[n_].astype(F32)
            rhs_l.append(jnp.concatenate([(v_l[n_].astype(F32) * b_c).astype(BF16),
                                          (kf * (b_c * eg)).astype(BF16)], axis=1))
            qg_ref[0, rs, ls] = (q_l[n_].astype(F32) * eg).astype(BF16)
            a_ref[0, rs, ls] = jnp.concatenate([(kkqk[n_, c:] * decay).astype(BF16), a_pad], axis=1)
            kd_l.append((kf * jnp.exp(g_end - g_c)).astype(BF16))
            eg_ref[0, ch, s:s + 1, :] = jnp.exp(g_end)
            dst.append((rs, ls, ch, s))
    eye_k = (lax.broadcasted_iota(jnp.int32, (dk, dk), 0)
             == lax.broadcasted_iota(jnp.int32, (dk, dk), 1)).astype(BF16)
    kdt = _bmm_nt(jnp.broadcast_to(eye_k, (len(kd_l), dk, dk)), jnp.stack(kd_l))
    for n_, (_, _, ch, s) in enumerate(dst):
        kdt_ref[0, ch, s] = kdt[n_].astype(BF16)
    wi = lax.broadcasted_iota(jnp.int32, (c, 2 * c), 0)
    wj = lax.broadcasted_iota(jnp.int32, (c, 2 * c), 1)
    right = wj >= c
    eye_r = (wj - c == wi).astype(F32)
    a = jnp.stack(neg_l)
    a_wide = jnp.concatenate([a, jnp.zeros_like(a)], axis=2)
    ps = _bmm(a.astype(BF16), (a_wide + eye_r).astype(BF16)) + eye_r
    m = 2
    while m < c:
        ps = _bmm(ps[:, :, :c].astype(BF16), ps.astype(BF16)) + jnp.where(right, ps, 0.0)
        m *= 2
    rhs = jnp.stack(rhs_l)
    sol = _bmm(ps.astype(BF16), jnp.concatenate([jnp.zeros_like(rhs), rhs], axis=1))
    for n_, (rs, ls, _, _) in enumerate(dst):
        u_ref[0, rs, ls] = sol[n_, :, :dk].astype(BF16)
        w_ref[0, rs, ls] = sol[n_, :, dk:].astype(BF16)


def _delta_wy(qkv, ba, conv_w, alog_row, dtb_row, *, ts):
    b, t, c3 = qkv.shape
    w = c3 // 3
    nc = ts // DN_CHUNK
    ns = N_DIR * DN_HEADS
    nt = t // ts
    r = ts // HALO
    tok = lambda n: pl.BlockSpec((1, ts, n), lambda i, j: (i, j, 0))
    return pl.pallas_call(
        functools.partial(_dwy_kernel, nc=nc, nt=nt),
        grid=(b, nt),
        in_specs=[tok(c3),
                  pl.BlockSpec((1, HALO, c3), lambda i, j: (i, jnp.maximum(j * r - 1, 0), 0)),
                  pl.BlockSpec((1, HALO, c3), lambda i, j: (i, jnp.minimum((j + 1) * r, t // HALO - 1), 0)),
                  tok(ba.shape[-1]),
                  _resident(conv_w.shape), _resident(alog_row.shape), _resident(dtb_row.shape)],
        out_specs=[tok(N_DIR * w)] * 4
        + [pl.BlockSpec((1, nc, ns, DN_HEAD_DIM, DN_CHUNK), lambda i, j: (i, j, 0, 0, 0)),
           pl.BlockSpec((1, nc, ns, LANES), lambda i, j: (i, j, 0, 0))],
        out_shape=[jax.ShapeDtypeStruct((b, t, N_DIR * w), BF16)] * 4
        + [jax.ShapeDtypeStruct((b, t // DN_CHUNK, ns, DN_HEAD_DIM, DN_CHUNK), BF16),
           jax.ShapeDtypeStruct((b, t // DN_CHUNK, ns, LANES), F32)],
        compiler_params=_params(("parallel", "parallel")),
        name="delta_wy",
    )(qkv, qkv, qkv, ba, conv_w, alog_row, dtb_row)


def _dscan_kernel(uf_ref, wf_ref, qgf_ref, af_ref, kdf_ref, egf_ref,
                  ur_ref, wr_ref, qgr_ref, ar_ref, kdr_ref, egr_ref, s0_ref,
                  of_ref, or_ref, sfin_ref, s_ref, *, nc, nb):
    i = pl.program_id(0)

    @pl.when(i == 0)
    def _():
        s_ref[...] = s0_ref[...]

    c, dk = DN_CHUNK, DN_HEAD_DIM
    dirs = ((uf_ref, wf_ref, qgf_ref, af_ref, kdf_ref, egf_ref, of_ref),
            (ur_ref, wr_ref, qgr_ref, ar_ref, kdr_ref, egr_ref, or_ref))

    def chunk_step(j, carry):
        wq_l, u_l, a_l, kd_l, eg_l, s_l, dst = [], [], [], [], [], [], []
        for r, (u_ref, w_ref, qg_ref, a_ref, kd_ref, eg_ref, o_ref) in enumerate(dirs):
            ch = nc - 1 - j if r else j
            rs = pl.ds(pl.multiple_of(ch * c, c), c)
            for b in range(nb):
                for h in range(DN_HEADS):
                    hs = slice(h * dk, (h + 1) * dk)
                    s = r * DN_HEADS + h
                    wq_l.append(jnp.concatenate([w_ref[b, rs, hs], qg_ref[b, rs, hs]], axis=0))
                    u_l.append(u_ref[b, rs, hs])
                    a_l.append(a_ref[b, rs, hs][:, :c])
                    kd_l.append(kd_ref[b, ch, h])
                    eg_l.append(eg_ref[b, ch, s:s + 1, :])
                    s_l.append(s_ref[b, s])
                    dst.append((o_ref, b, rs, hs, s))
        st = jnp.stack(s_l)
        ws_qs = _bmm(jnp.stack(wq_l), st.astype(BF16))
        v_new = (jnp.stack(u_l).astype(F32) - ws_qs[:, :c]).astype(BF16)
        o = ws_qs[:, c:] + _bmm(jnp.stack(a_l), v_new)
        s_new = st * jnp.stack(eg_l) + _bmm(jnp.stack(kd_l), v_new)
        for n_, (o_ref, b, rs, hs, s) in enumerate(dst):
            o_ref[b, rs, hs] = o[n_].astype(BF16)
            s_ref[b, s] = s_new[n_]
        return carry

    lax.fori_loop(0, nc, chunk_step, 0)

    @pl.when(i == pl.num_programs(0) - 1)
    def _():
        sfin_ref[...] = s_ref[...]


def _delta_scan(u, w, qg, a, kdt, eg, s0, *, ts):
    b, t, w2 = u.shape
    wd = w2 // N_DIR
    n = t // ts
    nc = ts // DN_CHUNK
    fwd = pl.BlockSpec((b, ts, wd), lambda i: (0, i, 0))
    rev = pl.BlockSpec((b, ts, wd), lambda i: (0, n - 1 - i, 1))
    kd_f = pl.BlockSpec((b, nc, DN_HEADS) + kdt.shape[3:], lambda i: (0, i, 0, 0, 0))
    kd_r = pl.BlockSpec((b, nc, DN_HEADS) + kdt.shape[3:], lambda i: (0, n - 1 - i, 1, 0, 0))
    eg_f = pl.BlockSpec((b, nc) + eg.shape[2:], lambda i: (0, i, 0, 0))
    eg_r = pl.BlockSpec((b, nc) + eg.shape[2:], lambda i: (0, n - 1 - i, 0, 0))
    out_f = pl.BlockSpec((b, ts, wd), lambda i: (0, i, 0))
    out_r = pl.BlockSpec((b, ts, wd), lambda i: (0, n - 1 - i, 0))
    return pl.pallas_call(
        functools.partial(_dscan_kernel, nc=nc, nb=b),
        grid=(n,),
        in_specs=[fwd, fwd, fwd, fwd, kd_f, eg_f, rev, rev, rev, rev, kd_r, eg_r, _resident(s0.shape)],
        out_specs=[out_f, out_r, pl.BlockSpec(s0.shape, lambda i: (0, 0, 0, 0))],
        out_shape=[jax.ShapeDtypeStruct((b, t, wd), BF16)] * 2 + [jax.ShapeDtypeStruct(s0.shape, F32)],
        scratch_shapes=[pltpu.VMEM(s0.shape, F32)],
        compiler_params=_params(("arbitrary",)),
        name="delta_scan",
    )(u, w, qg, a, kdt, eg, u, w, qg, a, kdt, eg, s0)


S5_POW_ROWS = 24


def _cexp(re, im):
    m = jnp.exp(re)
    return m * jnp.cos(im), m * jnp.sin(im)


def _s5_op_kernel(are_ref, aim_ref, ls_ref, btr_ref, bti_ref, ctr_ref, cti_ref,
                  win_ref, wm_ref, kbd_ref, lam_ref):
    l, p, n = S5_CHUNK, S5_GROUP, S5_STATE
    w = S5_BLK * n
    mask_w = (lax.broadcasted_iota(jnp.int32, (LANES, w), 0) // p
              == lax.broadcasted_iota(jnp.int32, (LANES, w), 1) // n)
    mask_c = (lax.broadcasted_iota(jnp.int32, (w, LANES), 0) // n
              == lax.broadcasted_iota(jnp.int32, (w, LANES), 1) // p)
    for r in range(N_DIR):
        are, aim = are_ref[r, 0], aim_ref[r, 0]
        dt = jnp.exp(ls_ref[r, 0])
        kk = lax.broadcasted_iota(jnp.int32, (S5_POW_ROWS, w), 0).astype(F32)
        pr, pi = _cexp(kk * (are * dt), kk * (aim * dt))
        lbr, lbi = pr[1:2], pi[1:2]
        den = are * are + aim * aim
        nr, ni = lbr - 1.0, lbi
        cr = (nr * are + ni * aim) / den
        ci = (ni * are - nr * aim) / den
        bre, bim = btr_ref[r, 0], bti_ref[r, 0]
        bmr = jnp.where(mask_w, jnp.concatenate([cr * bre - ci * bim] * S5_BLK, axis=0), 0.0)
        bmi = jnp.where(mask_w, jnp.concatenate([cr * bim + ci * bre] * S5_BLK, axis=0), 0.0)
        wr_l, wi_l = [], []
        for k in range(l):
            wr = bmr * pr[k:k + 1] - bmi * pi[k:k + 1]
            wi = bmr * pi[k:k + 1] + bmi * pr[k:k + 1]
            s = k if r else l - 1 - k
            win_ref[0, s * LANES:(s + 1) * LANES, (2 * r) * w:(2 * r + 1) * w] = wr.astype(BF16)
            win_ref[0, s * LANES:(s + 1) * LANES, (2 * r + 1) * w:(2 * r + 2) * w] = wi.astype(BF16)
            wr_l.append(wr)
            wi_l.append(wi)
        ctr = jnp.where(mask_c, ctr_ref[r, 0], 0.0)
        cti = jnp.where(mask_c, cti_ref[r, 0], 0.0)
        kbd_ref[0, r] = (_dot(jnp.concatenate(wr_l, axis=0).astype(BF16), ctr.astype(BF16))
                         - _dot(jnp.concatenate(wi_l, axis=0).astype(BF16), cti.astype(BF16)))
        zpad = jnp.zeros((LANES - S5_POW_ROWS, w), F32)
        ptr = jnp.concatenate([pr, zpad], axis=0).T
        pti = jnp.concatenate([pi, zpad], axis=0).T
        for s in range(l):
            e = l - s if r else s + 1
            er, ei = ptr[:, e:e + 1], pti[:, e:e + 1]
            wm_ref[0, (2 * r) * w:(2 * r + 1) * w, s * LANES:(s + 1) * LANES] = (ctr * er - cti * ei).astype(BF16)
            wm_ref[0, (2 * r + 1) * w:(2 * r + 2) * w, s * LANES:(s + 1) * LANES] = (
                -(ctr * ei + cti * er)).astype(BF16)
        lam_ref[0, r:r + 1, :] = pr[l:l + 1]
        lam_ref[1, r:r + 1, :] = pi[l:l + 1]


def _s5_toep_kernel(kbd_ref, wt_ref):
    l = S5_CHUNK
    k0 = (kbd_ref[0, 0, 0:LANES, :] + kbd_ref[0, 1, 0:LANES, :]).astype(BF16)
    kf = [kbd_ref[0, 0, d * LANES:(d + 1) * LANES, :].astype(BF16) for d in range(l)]
    kr = [kbd_ref[0, 1, d * LANES:(d + 1) * LANES, :].astype(BF16) for d in range(l)]
    for sp in range(l):
        for s in range(l):
            blk = kf[s - sp] if s > sp else kr[sp - s] if s < sp else k0
            wt_ref[0, sp * LANES:(sp + 1) * LANES, s * LANES:(s + 1) * LANES] = blk


def _s5_operators(a_re, a_im, log_step, b_re, b_im, c_re, c_im):
    g, n, p, l = S5_GROUPS, S5_STATE, S5_GROUP, S5_CHUNK
    nb, g8 = S5_NBLK, S5_BLK
    w = g8 * n
    lane_row = lambda a: a.reshape(N_DIR, nb, 1, w)
    ls = lane_row(jnp.repeat(log_step, n, axis=1))
    bt = lambda b: jnp.transpose(b.reshape(N_DIR, nb, g8, n, p), (0, 1, 4, 2, 3)).reshape(N_DIR, nb, p, w)
    ct = lambda c: jnp.tile(jnp.swapaxes(c, 2, 3).reshape(N_DIR, nb, w, p), (1, 1, 1, g8))
    blk = lambda r, c: pl.BlockSpec((N_DIR, 1, r, c), lambda j: (0, j, 0, 0))
    ops = lambda r, c: pl.BlockSpec((1, r, c), lambda j: (j, 0, 0))
    lw = l * LANES
    w_in, w_m, kbd, lam = pl.pallas_call(
        _s5_op_kernel,
        grid=(nb,),
        in_specs=[blk(1, w)] * 3 + [blk(p, w)] * 2 + [blk(w, LANES)] * 2,
        out_specs=[ops(lw, 2 * N_DIR * w), ops(2 * N_DIR * w, lw),
                   pl.BlockSpec((1, N_DIR, lw, LANES), lambda j: (j, 0, 0, 0)),
                   pl.BlockSpec((2, N_DIR, w), lambda j: (0, 0, j))],
        out_shape=[jax.ShapeDtypeStruct((nb, lw, 2 * N_DIR * w), BF16),
                   jax.ShapeDtypeStruct((nb, 2 * N_DIR * w, lw), BF16),
                   jax.ShapeDtypeStruct((nb, N_DIR, lw, LANES), F32),
                   jax.ShapeDtypeStruct((2, N_DIR, g * n), F32)],
        compiler_params=_params(("parallel",)),
        name="s5_params",
    )(lane_row(a_re), lane_row(a_im), ls, bt(b_re), bt(b_im), ct(c_re), ct(c_im))
    w_t = pl.pallas_call(
        _s5_toep_kernel,
        grid=(nb,),
        in_specs=[pl.BlockSpec((1, N_DIR, lw, LANES), lambda j: (j, 0, 0, 0))],
        out_specs=pl.BlockSpec((1, lw, lw), lambda j: (j, 0, 0)),
        out_shape=jax.ShapeDtypeStruct((nb, lw, lw), BF16),
        compiler_params=_params(("parallel",)),
        name="s5_toeplitz",
    )(kbd)
    return w_in, w_t, w_m, lam


def _chunk_rows(u_ref, nct):
    return jnp.concatenate([u_ref[0, pl.ds(s, nct, stride=S5_CHUNK), :] for s in range(S5_CHUNK)],
                           axis=1).astype(BF16)


def _s5_in_kernel(uc_ref, ul_ref, w_ref, *z_refs, ncc, ncl):
    v = jnp.concatenate([_chunk_rows(uc_ref, ncc), _chunk_rows(ul_ref, ncl)], axis=0)
    z = _dot(v, w_ref[0])
    wd = z_refs[0].shape[-1]
    for k in range(4):
        z_refs[k][...] = z[:ncc, k * wd:(k + 1) * wd]
        z_refs[4 + k][...] = z[ncc:, k * wd:(k + 1) * wd]


def _s5_chunk_in(u_c, u_l, w_in):
    b, tc, _ = u_c.shape
    t = u_l.shape[1]
    ncc, ncl = tc // S5_CHUNK, t // S5_CHUNK
    wd = S5_BLK * S5_STATE
    out = lambda rows: pl.BlockSpec((rows, wd), lambda j, i: (i, j))
    sds = lambda rows: jax.ShapeDtypeStruct((b * rows, S5_NBLK * wd), F32)
    outs = pl.pallas_call(
        functools.partial(_s5_in_kernel, ncc=ncc, ncl=ncl),
        grid=(S5_NBLK, b),
        in_specs=[pl.BlockSpec((1, tc, LANES), lambda j, i: (i, 0, j)),
                  pl.BlockSpec((1, t, LANES), lambda j, i: (i, 0, j)),
                  pl.BlockSpec((1,) + w_in.shape[1:], lambda j, i: (j, 0, 0))],
        out_specs=[out(ncc)] * 4 + [out(ncl)] * 4,
        out_shape=[sds(ncc)] * 4 + [sds(ncl)] * 4,
        compiler_params=_params(("parallel", "parallel")),
        name="s5_chunk_in",
    )(u_c, u_l, w_in)
    return list(outs[:4]), list(outs[4:])


SUBLANES = 8


def _cmul(ar, ai, br, bi):
    return ar * br - ai * bi, ar * bi + ai * br


def _s5_scan_tables(lr, li, row, rev):
    pw = [(lr, li)]
    for _ in range(SUBLANES - 1):
        pw.append(_cmul(pw[-1][0], pw[-1][1], lr, li))

    def by_row(power_of):
        tr, ti = jnp.zeros(row.shape, F32), jnp.zeros(row.shape, F32)
        for i in range(SUBLANES):
            k = power_of(i)
            if k:
                tr, ti = jnp.where(row == i, pw[k - 1][0], tr), jnp.where(row == i, pw[k - 1][1], ti)
        return tr, ti

    carry_w = by_row(lambda i: SUBLANES - i if rev else i + 1)
    steps = [by_row(lambda i, d=d: d if ((i <= SUBLANES - 1 - d) if rev else (i >= d)) else 0)
             for d in (1, 2, 4)]
    return carry_w, steps


def _s5_scan_tile(zr, zi, cr, ci, carry_w, steps, row, rev):
    yr, yi = zr, zi
    for d, (tr, ti) in zip((1, 2, 4), steps):
        sh = SUBLANES - d if rev else d
        ar, ai = _cmul(tr, ti, pltpu.roll(yr, sh, 0), pltpu.roll(yi, sh, 0))
        yr, yi = yr + ar, yi + ai
    ar, ai = _cmul(carry_w[0], carry_w[1], cr, ci)
    xr, xi = yr + ar, yi + ai
    sh, edge, last = (SUBLANES - 1, SUBLANES - 1, 0) if rev else (1, 0, SUBLANES - 1)
    er = jnp.where(row == edge, cr, pltpu.roll(xr, sh, 0))
    ei = jnp.where(row == edge, ci, pltpu.roll(xi, sh, 0))
    return er, ei, xr[last:last + 1], xi[last:last + 1]


def _s5_scan_kernel(cfr_ref, cfi_ref, crr_ref, cri_ref, zfr_ref, zfi_ref, zrr_ref, zri_ref, lam_ref,
                    xfr_ref, xfi_ref, xrr_ref, xri_ref, *, ncc, ncl, nb):
    tl = zfr_ref.shape[-1]
    row = lax.broadcasted_iota(jnp.int32, (SUBLANES, tl), 0)
    cw_f, st_f = _s5_scan_tables(lam_ref[0, 0:1, :], lam_ref[1, 0:1, :], row, False)
    cw_r, st_r = _s5_scan_tables(lam_ref[0, 1:2, :], lam_ref[1, 1:2, :], row, True)

    def make_step(zf, zr, nc, outs):
        nt = nc // SUBLANES

        def step(t, carry):
            new = []
            for b in range(nb):
                cfr, cfi, crr, cri = carry[b]
                rows = pl.ds(pl.multiple_of(b * nc + t * SUBLANES, SUBLANES), SUBLANES)
                er, ei, cfr, cfi = _s5_scan_tile(zf[0][rows, :], zf[1][rows, :], cfr, cfi, cw_f, st_f, row, False)
                if outs is not None:
                    outs[0][rows, :], outs[1][rows, :] = er, ei
                rows = pl.ds(pl.multiple_of(b * nc + (nt - 1 - t) * SUBLANES, SUBLANES), SUBLANES)
                er, ei, crr, cri = _s5_scan_tile(zr[0][rows, :], zr[1][rows, :], crr, cri, cw_r, st_r, row, True)
                if outs is not None:
                    outs[2][rows, :], outs[3][rows, :] = er, ei
                new.append((cfr, cfi, crr, cri))
            return tuple(new)

        return nt, step

    zero = jnp.zeros((1, tl), F32)
    carry = tuple((zero, zero, zero, zero) for _ in range(nb))
    nt, step = make_step((cfr_ref, cfi_ref), (crr_ref, cri_ref), ncc, None)
    carry = lax.fori_loop(0, nt, step, carry)
    nt, step = make_step((zfr_ref, zfi_ref), (zrr_ref, zri_ref), ncl, (xfr_ref, xfi_ref, xrr_ref, xri_ref))
    lax.fori_loop(0, nt, step, carry)


def _s5_state_scan(z_ctx, z_lat, lam, *, nb, tl):
    rc, w = z_ctx[0].shape
    rl = z_lat[0].shape[0]
    cb = pl.BlockSpec((rc, tl), lambda j: (0, j))
    zb = pl.BlockSpec((rl, tl), lambda j: (0, j))
    return pl.pallas_call(
        functools.partial(_s5_scan_kernel, ncc=rc // nb, ncl=rl // nb, nb=nb),
        grid=(w // tl,),
        in_specs=[cb] * 4 + [zb] * 4 + [pl.BlockSpec((2, N_DIR, tl), lambda j: (0, 0, j))],
        out_specs=[zb] * 4,
        out_shape=[jax.ShapeDtypeStruct((rl, w), F32)] * 4,
        compiler_params=_params(("parallel",)),
        name="s5_state_scan",
    )(*z_ctx, *z_lat, lam)


def _s5_out_kernel(u_ref, xfr_ref, xfi_ref, xrr_ref, xri_ref, wt_ref, wm_ref, y_ref, *, nct):
    x = jnp.concatenate([r[...].astype(BF16) for r in (xfr_ref, xfi_ref, xrr_ref, xri_ref)], axis=1)
    y = _dot(_chunk_rows(u_ref, nct), wt_ref[0]) + _dot(x, wm_ref[0])
    for s in range(S5_CHUNK):
        y_ref[0, pl.ds(s, nct, stride=S5_CHUNK), :] = y[:, s * LANES:(s + 1) * LANES]


def _s5_chunk_out(u, x4, w_t, w_m):
    b, t, _ = u.shape
    nct = t // S5_CHUNK
    wd = S5_BLK * S5_STATE
    xb = pl.BlockSpec((nct, wd), lambda j, i: (i, j))
    ub = pl.BlockSpec((1, t, LANES), lambda j, i: (i, 0, j))
    op = lambda a: pl.BlockSpec((1,) + a.shape[1:], lambda j, i: (j, 0, 0))
    return pl.pallas_call(
        functools.partial(_s5_out_kernel, nct=nct),
        grid=(S5_NBLK, b),
        in_specs=[ub] + [xb] * 4 + [op(w_t), op(w_m)],
        out_specs=ub,
        out_shape=jax.ShapeDtypeStruct(u.shape, F32),
        compiler_params=_params(("parallel", "parallel")),
        name="s5_chunk_out",
    )(u, *x4, w_t, w_m)


def _gelu_tanh(x):
    return 0.5 * x * (1.0 + jnp.tanh(0.7978845608028654 * (x + 0.044715 * x * x * x)))


def _merge_kernel(of_ref, or_ref, z_ref, ys_ref, u_ref, g_ref, x_ref, mod_ref,
                  dnw_ref, wa_ref, dsk_ref, wglu_ref, bglu_ref, wb_ref, wo_ref, n2w_ref,
                  xl_ref, h2_ref, *, d):
    o = of_ref[0].astype(F32) + or_ref[0].astype(F32)
    z = z_ref[0].astype(F32)
    heads = []
    for h in range(DN_HEADS):
        sl = slice(h * DN_HEAD_DIM, (h + 1) * DN_HEAD_DIM)
        oh = o[:, sl]
        on = oh * lax.rsqrt(jnp.mean(oh * oh, axis=-1, keepdims=True) + RMS_EPS) * dnw_ref[...]
        heads.append((on * _silu(z[:, sl])).astype(BF16))
    ya = _dot(jnp.concatenate(heads, axis=1), wa_ref[...])
    ys = ys_ref[0] + dsk_ref[...] * u_ref[0]
    zz = _dot(_gelu_tanh(ys).astype(BF16), wglu_ref[...]) + bglu_ref[...]
    yb = _dot((zz[:, :S5_WIDTH] * jax.nn.sigmoid(zz[:, S5_WIDTH:])).astype(BF16), wb_ref[...])
    gates = g_ref[0].astype(F32)
    mix = jax.nn.sigmoid(gates[:, :d]) * ya + jax.nn.sigmoid(gates[:, d:]) * yb
    xl = x_ref[0] + mod_ref[0, :, 2 * d:3 * d] * _dot(mix.astype(BF16), wo_ref[...])
    xl_ref[0] = xl
    hn = xl * lax.rsqrt(jnp.mean(xl * xl, axis=-1, keepdims=True) + RMS_EPS) * n2w_ref[...]
    h2_ref[0] = (hn * (1.0 + mod_ref[0, :, 4 * d:5 * d]) + mod_ref[0, :, 3 * d:4 * d]).astype(BF16)


def _mix_merge(o_f, o_r, z, ys, u, gates, x, mods, dn_norm_w, w_a_out, s5_d, w_glu, b_glu,
               w_b_out, w_o, norm2_w, *, tm):
    b, t, d = x.shape
    tok = lambda n: pl.BlockSpec((1, tm, n), lambda i, j: (i, j, 0))
    consts = [dn_norm_w.reshape(1, -1), w_a_out, s5_d.reshape(1, -1), w_glu, b_glu.reshape(1, -1),
              w_b_out, w_o, norm2_w.reshape(1, -1)]
    return pl.pallas_call(
        functools.partial(_merge_kernel, d=d),
        grid=(b, t // tm),
        in_specs=[tok(o_f.shape[-1]), tok(o_r.shape[-1]), tok(z.shape[-1]), tok(ys.shape[-1]),
                  tok(u.shape[-1]), tok(gates.shape[-1]), tok(d),
                  pl.BlockSpec((1, 1, mods.shape[-1]), lambda i, j: (i, 0, 0))]
        + [_resident(c.shape) for c in consts],
        out_specs=[tok(d), tok(d)],
        out_shape=[jax.ShapeDtypeStruct((b, t, d), F32), jax.ShapeDtypeStruct((b, t, d), BF16)],
        compiler_params=_params(("parallel", "parallel")),
        name="mix_merge",
    )(o_f, o_r, z, ys, u, gates, x, mods, *consts)


FFN_ROWS = 8
FFN_CB = 256


FFN_DOWN_GROUP = 6
FFN_AHEAD = 2


def _conv_row(e_ref, slot, part, r, cw):
    rows = [e_ref[slot, part, (r + i) * GRID_W:(r + i + 1) * GRID_W, :] for i in range(3)]
    taps = [(rows[0] * cw[j:j + 1, :] + rows[1] * cw[3 + j:4 + j, :] + rows[2] * cw[6 + j:7 + j, :]).astype(F32)
            for j in range(3)]
    left = pltpu.roll(taps[0], 1, 0)
    right = pltpu.roll(taps[2], GRID_W - 1, 0)
    sub = lax.broadcasted_iota(jnp.int32, (SUBLANES, left.shape[1]), 0)
    left = jnp.concatenate([jnp.where(sub == 0, 0.0, left[:SUBLANES]), left[SUBLANES:]], axis=0)
    right = jnp.concatenate([right[:-SUBLANES], jnp.where(sub == SUBLANES - 1, 0.0, right[-SUBLANES:])], axis=0)
    return left + taps[1] + right


def _ffn_kernel(h_ref, hp_ref, hn_ref, xl_ref, mod_ref, wup_ref, cw_ref, wd_ref, nfw_ref,
                o_ref, hext_ref, e_ref, act_ref, acc_ref, *, d, dff, nt):
    t = pl.program_id(1)
    n_out = h_ref.shape[1]
    hext_ref[0:GRID_W] = jnp.where(t == 0, jnp.zeros_like(hp_ref[0]), hp_ref[0])
    hext_ref[GRID_W:GRID_W + n_out] = h_ref[0]
    hext_ref[GRID_W + n_out:] = jnp.where(t == nt - 1, jnp.zeros_like(hn_ref[0]), hn_ref[0])
    ncb = dff // FFN_CB

    def up(k, slot):
        e_ref[slot, 0] = _dot(hext_ref[...], wup_ref[:, k * FFN_CB:(k + 1) * FFN_CB]).astype(BF16)
        e_ref[slot, 1] = _dot(hext_ref[...], wup_ref[:, dff + k * FFN_CB:dff + (k + 1) * FFN_CB]).astype(BF16)

    for k in range(FFN_AHEAD):
        up(k, k)
    done = 0
    for k in range(ncb):
        slot = k % (FFN_AHEAD + 1)
        if k + FFN_AHEAD < ncb:
            up(k + FFN_AHEAD, (k + FFN_AHEAD) % (FFN_AHEAD + 1))
        gs = slice(k * FFN_CB, (k + 1) * FFN_CB)
        cg = cw_ref[:, gs]
        cv = cw_ref[:, dff + k * FFN_CB:dff + (k + 1) * FFN_CB]
        for r in range(n_out // GRID_W):
            gate = _conv_row(e_ref, slot, 0, r, cg)
            val = _conv_row(e_ref, slot, 1, r, cv)
            act_ref[r * GRID_W:(r + 1) * GRID_W, gs] = (_silu(gate) * val).astype(BF16)
        if (k + 1) % FFN_DOWN_GROUP == 0 or k == ncb - 1:
            ks = slice(done * FFN_CB, (k + 1) * FFN_CB)
            part = _dot(act_ref[:, ks], wd_ref[ks, :])
            if done == 0:
                acc_ref[...] = part
            else:
                acc_ref[...] += part
            done = k + 1
    xo = xl_ref[0] + mod_ref[0, :, 5 * d:6 * d] * acc_ref[...]
    o_ref[0] = xo * lax.rsqrt(jnp.mean(xo * xo, axis=-1, keepdims=True) + RMS_EPS) * nfw_ref[...]


def _conv_ffn(h2, xl, mods, w_up, conv_w, w_down, norm_f_w):
    b, t, d = xl.shape
    dff = w_down.shape[0]
    tm = FFN_ROWS * GRID_W
    nt = t // tm
    nrow = t // GRID_W
    cw = conv_w.reshape(9, 2 * dff).astype(BF16)
    tok = lambda: pl.BlockSpec((1, tm, d), lambda i, j: (i, j, 0))
    return pl.pallas_call(
        functools.partial(_ffn_kernel, d=d, dff=dff, nt=nt),
        grid=(b, nt),
        in_specs=[tok(),
                  pl.BlockSpec((1, GRID_W, d), lambda i, j: (i, jnp.maximum(j * FFN_ROWS - 1, 0), 0)),
                  pl.BlockSpec((1, GRID_W, d), lambda i, j: (i, jnp.minimum((j + 1) * FFN_ROWS, nrow - 1), 0)),
                  tok(),
                  pl.BlockSpec((1, 1, mods.shape[-1]), lambda i, j: (i, 0, 0)),
                  _resident(w_up.shape), _resident(cw.shape), _resident(w_down.shape), _resident((1, d))],
        out_specs=tok(),
        out_shape=jax.ShapeDtypeStruct((b, t, d), F32),
        scratch_shapes=[pltpu.VMEM((tm + 2 * GRID_W, d), BF16),
                        pltpu.VMEM((FFN_AHEAD + 1, 2, tm + 2 * GRID_W, FFN_CB), BF16),
                        pltpu.VMEM((tm, dff), BF16),
                        pltpu.VMEM((tm, d), F32)],
        compiler_params=_params(("parallel", "parallel")),
        name="conv_ffn",
    )(h2, h2, h2, xl, mods, w_up.astype(BF16), cw, w_down.astype(BF16), norm_f_w.reshape(1, d))


def _lane_row(a):
    a = a.reshape(-1).astype(F32)
    return jnp.pad(a, (0, LANES - a.shape[0])).reshape(1, LANES)


def _pad_cols(w):
    return jnp.pad(w, ((0, 0), (0, LANES - w.shape[1])))


def kernel(x, c, ctx, c_ctx, w_ada, b_ada, norm1_w, w_in, dn_conv_w, dn_a_log, dn_dt_bias, dn_norm_w,
           w_a_out, s5_a_re, s5_a_im, s5_log_step, s5_b_re, s5_b_im, s5_c_re, s5_c_im, s5_d, w_glu,
           b_glu, w_b_out, w_o, norm2_w, w_up, ffn_conv_w, w_down, norm_f_w):
    assert w_ada.shape[0] == 1, "single-layer block"
    b, t, d = x.shape
    tc = ctx.shape[1]
    nh = N_DIR * DN_HEADS
    assert b < 8, "batch rows and the context row share one 8-row modulation block"

    c_rows = jnp.zeros((8, d), F32).at[:b].set(c).at[b].set(c_ctx)
    mods = _modulation(c_rows, w_ada[0], b_ada[0]).reshape(8, 1, N_MOD * d)

    w = w_in[0]
    o_z, o_b, o_a = 3 * DN_WIDTH, 4 * DN_WIDTH, 4 * DN_WIDTH + nh
    o_u = o_a + nh
    o_g = o_u + S5_WIDTH
    wqkv = w[:, :o_z].astype(BF16)
    wz = w[:, o_z:o_b].astype(BF16)
    wba = jnp.concatenate([_pad_cols(w[:, o_b:o_a]), _pad_cols(w[:, o_a:o_u])], axis=1).astype(BF16)
    wu = w[:, o_u:o_g].astype(BF16)
    wg = w[:, o_g:].astype(BF16)

    qkv_l, ba_l, u_l, z_l, gates_l = _in_proj(x, mods, lambda i: i, norm1_w[0], wqkv, wba, wu, wz, wg, tm=512)
    qkv_c, ba_c, u_c = _in_proj(ctx, mods, lambda i: b, norm1_w[0], wqkv, wba, wu, tm=tc)

    alog_row, dtb_row = _lane_row(dn_a_log[0]), _lane_row(dn_dt_bias[0])
    s0 = jnp.zeros((b, nh, DN_HEAD_DIM, DN_HEAD_DIM), F32)
    wy_c = _delta_wy(qkv_c, ba_c, dn_conv_w[0], alog_row, dtb_row, ts=tc)
    wy_l = _delta_wy(qkv_l, ba_l, dn_conv_w[0], alog_row, dtb_row, ts=256)
    _, _, s_ctx = _delta_scan(*wy_c, s0, ts=128)
    o_f, o_r, _ = _delta_scan(*wy_l, s_ctx, ts=128)

    w_s5in, w_s5t, w_s5m, lam = _s5_operators(s5_a_re[0], s5_a_im[0], s5_log_step[0], s5_b_re[0],
                                              s5_b_im[0], s5_c_re[0], s5_c_im[0])
    zs_c, zs_l = _s5_chunk_in(u_c, u_l, w_s5in)
    x_l = _s5_state_scan(zs_c, zs_l, lam, nb=b, tl=256)
    y_s5 = _s5_chunk_out(u_l, x_l, w_s5t, w_s5m)

    xl, h2 = _mix_merge(o_f, o_r, z_l, y_s5, u_l, gates_l, x, mods, dn_norm_w[0], w_a_out[0].astype(BF16),
                        s5_d[0], w_glu[0].astype(BF16), b_glu[0], w_b_out[0].astype(BF16),
                        w_o[0].astype(BF16), norm2_w[0], tm=512)
    return _conv_ffn(h2, xl, mods, w_up[0], ffn_conv_w[0], w_down[0], norm_f_w)
```

```python
import functools

import jax
import jax.numpy as jnp
from jax import lax
from jax.experimental import pallas as pl
from jax.experimental.pallas import tpu as pltpu

F32 = jnp.float32
BF16 = jnp.bfloat16

GRID_W = 64
N_DIR = 2
DN_HEADS = 4
DN_HEAD_DIM = 128
DN_WIDTH = DN_HEADS * DN_HEAD_DIM
DN_CHUNK = 64
S5_WIDTH = 512
S5_GROUP = 16
S5_GROUPS = S5_WIDTH // S5_GROUP
S5_STATE = 64
S5_CHUNK = 16
N_MOD = 6
RMS_EPS = 1e-6
L2_EPS = 1e-6
LANES = 128
S5_BLK = LANES // S5_GROUP
S5_NBLK = S5_GROUPS // S5_BLK
VMEM_LIMIT = 56 * 1024 * 1024


def _dot(a, b):
    return jnp.dot(a, b, preferred_element_type=F32)


def _dot_f32(a, b):
    return jnp.dot(a, b, preferred_element_type=F32, precision=lax.Precision.HIGHEST)


def _silu(x):
    return x * jax.nn.sigmoid(x)


def _softplus(x):
    return jnp.maximum(x, 0.0) + jnp.log(1.0 + jnp.exp(-jnp.abs(x)))


def _params(sem, vmem=VMEM_LIMIT, flags=None):
    return pltpu.CompilerParams(dimension_semantics=sem, vmem_limit_bytes=vmem, flags=flags)


def _resident(shape):
    nd = len(shape)
    return pl.BlockSpec(shape, lambda *_: (0,) * nd, pipeline_mode=pl.Buffered(1))


def _mod_kernel(c_ref, w_ref, b_ref, o_ref):
    sc = _silu(c_ref[...])
    o_ref[...] = _dot(sc.astype(BF16), w_ref[...].astype(BF16)) + b_ref[...]


def _modulation(c_rows, w_ada, b_ada):
    d, n = w_ada.shape
    tn = n // 4
    return pl.pallas_call(
        _mod_kernel,
        grid=(n // tn,),
        in_specs=[pl.BlockSpec(c_rows.shape, lambda j: (0, 0)),
                  pl.BlockSpec((d, tn), lambda j: (0, j)),
                  pl.BlockSpec((1, tn), lambda j: (0, j))],
        out_specs=pl.BlockSpec((c_rows.shape[0], tn), lambda j: (0, j)),
        out_shape=jax.ShapeDtypeStruct((c_rows.shape[0], n), F32),
        compiler_params=_params(("arbitrary",)),
        name="adaln_mod",
    )(c_rows, w_ada, b_ada.reshape(1, n))


def _inproj_kernel(x_ref, mod_ref, nw_ref, wqkv_ref, wba_ref, wu_ref, wz_ref, wg_ref,
                   qkv_ref, ba_ref, u_ref, z_ref, g_ref, *, d):
    x = x_ref[0]
    ms = jnp.mean(x * x, axis=-1, keepdims=True)
    h = x * lax.rsqrt(ms + RMS_EPS) * nw_ref[...]
    shift = mod_ref[0, :, 0:d]
    scale = mod_ref[0, :, d:2 * d]
    hb = (h * (1.0 + scale) + shift).astype(BF16)
    qkv_ref[0] = _dot(hb, wqkv_ref[...]).astype(BF16)
    ba_ref[0] = _dot(hb, wba_ref[...])
    u_ref[0] = _dot(hb, wu_ref[...])
    if z_ref is not None:
        z_ref[0] = _dot(hb, wz_ref[...]).astype(BF16)
        g_ref[0] = _dot(hb, wg_ref[...]).astype(BF16)


def _inproj_ctx_kernel(x_ref, mod_ref, nw_ref, wqkv_ref, wba_ref, wu_ref,
                       qkv_ref, ba_ref, u_ref, *, d):
    _inproj_kernel(x_ref, mod_ref, nw_ref, wqkv_ref, wba_ref, wu_ref, None, None,
                   qkv_ref, ba_ref, u_ref, None, None, d=d)


def _in_proj(x, mods, mod_row0, norm_w, wqkv, wba, wu, wz=None, wg=None, *, tm):
    b, t, d = x.shape
    full = wz is not None
    tok = lambda n: pl.BlockSpec((1, tm, n), lambda i, j: (i, j, 0))
    in_specs = [tok(d),
                pl.BlockSpec((1, 1, mods.shape[-1]), lambda i, j: (mod_row0(i), 0, 0)),
                _resident((1, d)), _resident(wqkv.shape), _resident(wba.shape),
                _resident(wu.shape)]
    args = [x, mods, norm_w.reshape(1, d), wqkv, wba, wu]
    out_specs = [tok(wqkv.shape[1]), tok(wba.shape[1]), tok(wu.shape[1])]
    out_shape = [jax.ShapeDtypeStruct((b, t, wqkv.shape[1]), BF16),
                 jax.ShapeDtypeStruct((b, t, wba.shape[1]), F32),
                 jax.ShapeDtypeStruct((b, t, wu.shape[1]), F32)]
    if full:
        in_specs += [_resident(wz.shape), _resident(wg.shape)]
        args += [wz, wg]
        out_specs += [tok(wz.shape[1]), tok(wg.shape[1])]
        out_shape += [jax.ShapeDtypeStruct((b, t, wz.shape[1]), BF16),
                      jax.ShapeDtypeStruct((b, t, wg.shape[1]), BF16)]
    body = functools.partial(_inproj_kernel if full else _inproj_ctx_kernel, d=d)
    return pl.pallas_call(
        body, grid=(b, t // tm), in_specs=in_specs, out_specs=out_specs, out_shape=out_shape,
        compiler_params=_params(("parallel", "parallel")),
        name="in_proj" if full else "in_proj_ctx",
    )(*args)


HALO = 16


def _delta_inputs(x_ref, xp_ref, xn_ref, ba_ref, cw_ref, alog_ref, dtb_ref, *, tm, nt):
    t = pl.program_id(1)
    x = x_ref[0].astype(F32)
    prow = jnp.where(t == 0, 0.0, xp_ref[0, HALO - 1:HALO, :].astype(F32))
    nrow = jnp.where(t == nt - 1, 0.0, xn_ref[0, 0:1, :].astype(F32))
    sub = lax.broadcasted_iota(jnp.int32, (SUBLANES, x.shape[1]), 0)
    xprev = pltpu.roll(x, 1, 0)
    xnext = pltpu.roll(x, tm - 1, 0)
    xprev = jnp.concatenate([jnp.where(sub == 0, prow, xprev[:SUBLANES]), xprev[SUBLANES:]], axis=0)
    xnext = jnp.concatenate([xnext[:-SUBLANES], jnp.where(sub == SUBLANES - 1, nrow, xnext[-SUBLANES:])], axis=0)
    y = _silu(xprev * cw_ref[0:1, :] + x * cw_ref[1:2, :] + xnext * cw_ref[2:3, :])
    q, k, v = [], [], []
    for h in range(DN_HEADS):
        qh = y[:, h * DN_HEAD_DIM:(h + 1) * DN_HEAD_DIM]
        kh = y[:, DN_WIDTH + h * DN_HEAD_DIM:DN_WIDTH + (h + 1) * DN_HEAD_DIM]
        qn = qh * lax.rsqrt(jnp.sum(qh * qh, axis=-1, keepdims=True) + L2_EPS)
        kn = kh * lax.rsqrt(jnp.sum(kh * kh, axis=-1, keepdims=True) + L2_EPS)
        q.append((qn * (DN_HEAD_DIM ** -0.5)).astype(BF16))
        k.append(kn.astype(BF16))
        v.append(y[:, 2 * DN_WIDTH + h * DN_HEAD_DIM:2 * DN_WIDTH + (h + 1) * DN_HEAD_DIM].astype(BF16))
    ba = ba_ref[0]
    beta = jax.nn.sigmoid(ba[:, 0:LANES])
    g = -jnp.exp(alog_ref[...]) * _softplus(ba[:, LANES:2 * LANES] + dtb_ref[...])
    return q, k, v, beta, g


def _bmm(a, b):
    return jnp.einsum('nik,nkj->nij', a, b, preferred_element_type=F32)


def _bmm_nt(a, b):
    return jnp.einsum('nik,njk->nij', a, b, preferred_element_type=F32)


def _dwy_kernel(x_ref, xp_ref, xn_ref, ba_ref, cw_ref, alog_ref, dtb_ref,
                u_ref, w_ref, qg_ref, a_ref, kdt_ref, eg_ref, *, nc, nt):
    c, dk = DN_CHUNK, DN_HEAD_DIM
    ts = nc * c
    q_h, k_h, v_h, beta, g = _delta_inputs(x_ref, xp_ref, xn_ref, ba_ref, cw_ref, alog_ref, dtb_ref,
                                           tm=ts, nt=nt)
    ii = lax.broadcasted_iota(jnp.int32, (c, c), 0)
    jj = lax.broadcasted_iota(jnp.int32, (c, c), 1)
    ti = lax.broadcasted_iota(jnp.int32, (ts, ts), 0)
    tj = lax.broadcasted_iota(jnp.int32, (ts, ts), 1)
    same_chunk = (ti // c) == (tj // c)
    blocks = [(ch, h) for ch in range(nc) for h in range(DN_HEADS)]
    k_l = [k_h[h][ch * c:(ch + 1) * c] for ch, h in blocks]
    q_l = [q_h[h][ch * c:(ch + 1) * c] for ch, h in blocks]
    v_l = [v_h[h][ch * c:(ch + 1) * c] for ch, h in blocks]
    kkqk = _bmm_nt(jnp.stack([jnp.concatenate([k_, q_], axis=0) for k_, q_ in zip(k_l, q_l)]),
                   jnp.stack(k_l))
    a_pad = jnp.zeros((c, dk - c), BF16)
    neg_l, rhs_l, kd_l, dst = [], [], [], []
    for r in range(N_DIR):
        incl = ii <= jj if r else ii >= jj
        strict = ii < jj if r else ii > jj
        tri = (same_chunk & (ti <= tj if r else ti >= tj)).astype(F32)
        g_cum = _dot_f32(tri, g)
        g_cum_t = g_cum.T
        for n_, (ch, h) in enumerate(blocks):
            s = r * DN_HEADS + h
            rs = slice(ch * c, (ch + 1) * c)
            ls = slice(s * dk, (s + 1) * dk)
            g_c = jnp.broadcast_to(g_cum[rs, s:s + 1], (c, dk))
            b_c = jnp.broadcast_to(beta[rs, s:s + 1], (c, dk))
            g_r = g_cum_t[s:s + 1, rs]
            g_end = g_c[0:1] if r else g_c[c - 1:c]
            decay = jnp.where(incl, jnp.exp(jnp.where(incl, g_c[:, :c] - g_r, 0.0)), 0.0)
            neg_l.append(jnp.where(strict, kkqk[n_, :c] * (-b_c[:, :c]) * decay, 0.0))
            eg = jnp.exp(g_c)
            kf = k_l[n_].astype(F32)
            rhs_l.append(jnp.concatenate([(v_l[n_].astype(F32) * b_c).astype(BF16),
                                          (kf * (b_c * eg)).astype(BF16)], axis=1))
            qg_ref[0, rs, ls] = (q_l[n_].astype(F32) * eg).astype(BF16)
            a_ref[0, rs, ls] = jnp.concatenate([(kkqk[n_, c:] * decay).astype(BF16), a_pad], axis=1)
            kd_l.append((kf * jnp.exp(g_end - g_c)).astype(BF16))
            eg_ref[0, ch, s:s + 1, :] = jnp.exp(g_end)
            dst.append((rs, ls, ch, s))
    eye_k = (lax.broadcasted_iota(jnp.int32, (dk, dk), 0)
             == lax.broadcasted_iota(jnp.int32, (dk, dk), 1)).astype(BF16)
    kdt = _bmm_nt(jnp.broadcast_to(eye_k, (len(kd_l), dk, dk)), jnp.stack(kd_l))
    for n_, (_, _, ch, s) in enumerate(dst):
        kdt_ref[0, ch, s] = kdt[n_].astype(BF16)
    wi = lax.broadcasted_iota(jnp.int32, (c, 2 * c), 0)
    wj = lax.broadcasted_iota(jnp.int32, (c, 2 * c), 1)
    right = wj >= c
    eye_r = (wj - c == wi).astype(F32)
    a = jnp.stack(neg_l)
    a_wide = jnp.concatenate([a, jnp.zeros_like(a)], axis=2)
    ps = _bmm(a.astype(BF16), (a_wide + eye_r).astype(BF16)) + eye_r
    m = 2
    while m < c:
        ps = _bmm(ps[:, :, :c].astype(BF16), ps.astype(BF16)) + jnp.where(right, ps, 0.0)
        m *= 2
    rhs = jnp.stack(rhs_l)
    sol = _bmm(ps.astype(BF16), jnp.concatenate([jnp.zeros_like(rhs), rhs], axis=1))
    for n_, (rs, ls, _, _) in enumerate(dst):
        u_ref[0, rs, ls] = sol[n_, :, :dk].astype(BF16)
        w_ref[0, rs, ls] = sol[n_, :, dk:].astype(BF16)


def _delta_wy(qkv, ba, conv_w, alog_row, dtb_row, *, ts):
    b, t, c3 = qkv.shape
    w = c3 // 3
    nc = ts // DN_CHUNK
    ns = N_DIR * DN_HEADS
    nt = t // ts
    r = ts // HALO
    tok = lambda n: pl.BlockSpec((1, ts, n), lambda i, j: (i, j, 0))
    return pl.pallas_call(
        functools.partial(_dwy_kernel, nc=nc, nt=nt),
        grid=(b, nt),
        in_specs=[tok(c3),
                  pl.BlockSpec((1, HALO, c3), lambda i, j: (i, jnp.maximum(j * r - 1, 0), 0)),
                  pl.BlockSpec((1, HALO, c3), lambda i, j: (i, jnp.minimum((j + 1) * r, t // HALO - 1), 0)),
                  tok(ba.shape[-1]),
                  _resident(conv_w.shape), _resident(alog_row.shape), _resident(dtb_row.shape)],
        out_specs=[tok(N_DIR * w)] * 4
        + [pl.BlockSpec((1, nc, ns, DN_HEAD_DIM, DN_CHUNK), lambda i, j: (i, j, 0, 0, 0)),
           pl.BlockSpec((1, nc, ns, LANES), lambda i, j: (i, j, 0, 0))],
        out_shape=[jax.ShapeDtypeStruct((b, t, N_DIR * w), BF16)] * 4
        + [jax.ShapeDtypeStruct((b, t // DN_CHUNK, ns, DN_HEAD_DIM, DN_CHUNK), BF16),
           jax.ShapeDtypeStruct((b, t // DN_CHUNK, ns, LANES), F32)],
        compiler_params=_params(("parallel", "parallel")),
        name="delta_wy",
    )(qkv, qkv, qkv, ba, conv_w, alog_row, dtb_row)


def _dscan_kernel(uf_ref, wf_ref, qgf_ref, af_ref, kdf_ref, egf_ref,
                  ur_ref, wr_ref, qgr_ref, ar_ref, kdr_ref, egr_ref, s0_ref,
                  of_ref, or_ref, sfin_ref, s_ref, *, nc, nb):
    i = pl.program_id(0)

    @pl.when(i == 0)
    def _():
        s_ref[...] = s0_ref[...]

    c, dk = DN_CHUNK, DN_HEAD_DIM
    dirs = ((uf_ref, wf_ref, qgf_ref, af_ref, kdf_ref, egf_ref, of_ref),
            (ur_ref, wr_ref, qgr_ref, ar_ref, kdr_ref, egr_ref, or_ref))

    def chunk_step(j, carry):
        wq_l, u_l, a_l, kd_l, eg_l, s_l, dst = [], [], [], [], [], [], []
        for r, (u_ref, w_ref, qg_ref, a_ref, kd_ref, eg_ref, o_ref) in enumerate(dirs):
            ch = nc - 1 - j if r else j
            rs = pl.ds(pl.multiple_of(ch * c, c), c)
            for b in range(nb):
                for h in range(DN_HEADS):
                    hs = slice(h * dk, (h + 1) * dk)
                    s = r * DN_HEADS + h
                    wq_l.append(jnp.concatenate([w_ref[b, rs, hs], qg_ref[b, rs, hs]], axis=0))
                    u_l.append(u_ref[b, rs, hs])
                    a_l.append(a_ref[b, rs, hs][:, :c])
                    kd_l.append(kd_ref[b, ch, h])
                    eg_l.append(eg_ref[b, ch, s:s + 1, :])
                    s_l.append(s_ref[b, s])
                    dst.append((o_ref, b, rs, hs, s))
        st = jnp.stack(s_l)
        ws_qs = _bmm(jnp.stack(wq_l), st.astype(BF16))
        v_new = (jnp.stack(u_l).astype(F32) - ws_qs[:, :c]).astype(BF16)
        o = ws_qs[:, c:] + _bmm(jnp.stack(a_l), v_new)
        s_new = st * jnp.stack(eg_l) + _bmm(jnp.stack(kd_l), v_new)
        for n_, (o_ref, b, rs, hs, s) in enumerate(dst):
            o_ref[b, rs, hs] = o[n_].astype(BF16)
            s_ref[b, s] = s_new[n_]
        return carry

    lax.fori_loop(0, nc, chunk_step, 0)

    @pl.when(i == pl.num_programs(0) - 1)
    def _():
        sfin_ref[...] = s_ref[...]


def _delta_scan(u, w, qg, a, kdt, eg, s0, *, ts):
    b, t, w2 = u.shape
    wd = w2 // N_DIR
    n = t // ts
    nc = ts // DN_CHUNK
    fwd = pl.BlockSpec((b, ts, wd), lambda i: (0, i, 0))
    rev = pl.BlockSpec((b, ts, wd), lambda i: (0, n - 1 - i, 1))
    kd_f = pl.BlockSpec((b, nc, DN_HEADS) + kdt.shape[3:], lambda i: (0, i, 0, 0, 0))
    kd_r = pl.BlockSpec((b, nc, DN_HEADS) + kdt.shape[3:], lambda i: (0, n - 1 - i, 1, 0, 0))
    eg_f = pl.BlockSpec((b, nc) + eg.shape[2:], lambda i: (0, i, 0, 0))
    eg_r = pl.BlockSpec((b, nc) + eg.shape[2:], lambda i: (0, n - 1 - i, 0, 0))
    out_f = pl.BlockSpec((b, ts, wd), lambda i: (0, i, 0))
    out_r = pl.BlockSpec((b, ts, wd), lambda i: (0, n - 1 - i, 0))
    return pl.pallas_call(
        functools.partial(_dscan_kernel, nc=nc, nb=b),
        grid=(n,),
        in_specs=[fwd, fwd, fwd, fwd, kd_f, eg_f, rev, rev, rev, rev, kd_r, eg_r, _resident(s0.shape)],
        out_specs=[out_f, out_r, pl.BlockSpec(s0.shape, lambda i: (0, 0, 0, 0))],
        out_shape=[jax.ShapeDtypeStruct((b, t, wd), BF16)] * 2 + [jax.ShapeDtypeStruct(s0.shape, F32)],
        scratch_shapes=[pltpu.VMEM(s0.shape, F32)],
        compiler_params=_params(("arbitrary",)),
        name="delta_scan",
    )(u, w, qg, a, kdt, eg, u, w, qg, a, kdt, eg, s0)


S5_POW_ROWS = 24


def _cexp(re, im):
    m = jnp.exp(re)
    return m * jnp.cos(im), m * jnp.sin(im)


def _s5_op_kernel(are_ref, aim_ref, ls_ref, btr_ref, bti_ref, ctr_ref, cti_ref,
                  win_ref, wm_ref, kbd_ref, lam_ref):
    l, p, n = S5_CHUNK, S5_GROUP, S5_STATE
    w = S5_BLK * n
    mask_w = (lax.broadcasted_iota(jnp.int32, (LANES, w), 0) // p
              == lax.broadcasted_iota(jnp.int32, (LANES, w), 1) // n)
    mask_c = (lax.broadcasted_iota(jnp.int32, (w, LANES), 0) // n
              == lax.broadcasted_iota(jnp.int32, (w, LANES), 1) // p)
    for r in range(N_DIR):
        are, aim = are_ref[r, 0], aim_ref[r, 0]
        dt = jnp.exp(ls_ref[r, 0])
        kk = lax.broadcasted_iota(jnp.int32, (S5_POW_ROWS, w), 0).astype(F32)
        pr, pi = _cexp(kk * (are * dt), kk * (aim * dt))
        lbr, lbi = pr[1:2], pi[1:2]
        den = are * are + aim * aim
        nr, ni = lbr - 1.0, lbi
        cr = (nr * are + ni * aim) / den
        ci = (ni * are - nr * aim) / den
        bre, bim = btr_ref[r, 0], bti_ref[r, 0]
        bmr = jnp.where(mask_w, jnp.concatenate([cr * bre - ci * bim] * S5_BLK, axis=0), 0.0)
        bmi = jnp.where(mask_w, jnp.concatenate([cr * bim + ci * bre] * S5_BLK, axis=0), 0.0)
        wr_l, wi_l = [], []
        for k in range(l):
            wr = bmr * pr[k:k + 1] - bmi * pi[k:k + 1]
            wi = bmr * pi[k:k + 1] + bmi * pr[k:k + 1]
            s = k if r else l - 1 - k
            win_ref[0, s * LANES:(s + 1) * LANES, (2 * r) * w:(2 * r + 1) * w] = wr.astype(BF16)
            win_ref[0, s * LANES:(s + 1) * LANES, (2 * r + 1) * w:(2 * r + 2) * w] = wi.astype(BF16)
            wr_l.append(wr)
            wi_l.append(wi)
        ctr = jnp.where(mask_c, ctr_ref[r, 0], 0.0)
        cti = jnp.where(mask_c, cti_ref[r, 0], 0.0)
        kbd_ref[0, r] = (_dot(jnp.concatenate(wr_l, axis=0).astype(BF16), ctr.astype(BF16))
                         - _dot(jnp.concatenate(wi_l, axis=0).astype(BF16), cti.astype(BF16)))
        zpad = jnp.zeros((LANES - S5_POW_ROWS, w), F32)
        ptr = jnp.concatenate([pr, zpad], axis=0).T
        pti = jnp.concatenate([pi, zpad], axis=0).T
        for s in range(l):
            e = l - s if r else s + 1
            er, ei = ptr[:, e:e + 1], pti[:, e:e + 1]
            wm_ref[0, (2 * r) * w:(2 * r + 1) * w, s * LANES:(s + 1) * LANES] = (ctr * er - cti * ei).astype(BF16)
            wm_ref[0, (2 * r + 1) * w:(2 * r + 2) * w, s * LANES:(s + 1) * LANES] = (
                -(ctr * ei + cti * er)).astype(BF16)
        lam_ref[0, r:r + 1, :] = pr[l:l + 1]
        lam_ref[1, r:r + 1, :] = pi[l:l + 1]


def _s5_toep_kernel(kbd_ref, wt_ref):
    l = S5_CHUNK
    k0 = (kbd_ref[0, 0, 0:LANES, :] + kbd_ref[0, 1, 0:LANES, :]).astype(BF16)
    kf = [kbd_ref[0, 0, d * LANES:(d + 1) * LANES, :].astype(BF16) for d in range(l)]
    kr = [kbd_ref[0, 1, d * LANES:(d + 1) * LANES, :].astype(BF16) for d in range(l)]
    for sp in range(l):
        for s in range(l):
            blk = kf[s - sp] if s > sp else kr[sp - s] if s < sp else k0
            wt_ref[0, sp * LANES:(sp + 1) * LANES, s * LANES:(s + 1) * LANES] = blk


def _s5_operators(a_re, a_im, log_step, b_re, b_im, c_re, c_im):
    g, n, p, l = S5_GROUPS, S5_STATE, S5_GROUP, S5_CHUNK
    nb, g8 = S5_NBLK, S5_BLK
    w = g8 * n
    lane_row = lambda a: a.reshape(N_DIR, nb, 1, w)
    ls = lane_row(jnp.repeat(log_step, n, axis=1))
    bt = lambda b: jnp.transpose(b.reshape(N_DIR, nb, g8, n, p), (0, 1, 4, 2, 3)).reshape(N_DIR, nb, p, w)
    ct = lambda c: jnp.tile(jnp.swapaxes(c, 2, 3).reshape(N_DIR, nb, w, p), (1, 1, 1, g8))
    blk = lambda r, c: pl.BlockSpec((N_DIR, 1, r, c), lambda j: (0, j, 0, 0))
    ops = lambda r, c: pl.BlockSpec((1, r, c), lambda j: (j, 0, 0))
    lw = l * LANES
    w_in, w_m, kbd, lam = pl.pallas_call(
        _s5_op_kernel,
        grid=(nb,),
        in_specs=[blk(1, w)] * 3 + [blk(p, w)] * 2 + [blk(w, LANES)] * 2,
        out_specs=[ops(lw, 2 * N_DIR * w), ops(2 * N_DIR * w, lw),
                   pl.BlockSpec((1, N_DIR, lw, LANES), lambda j: (j, 0, 0, 0)),
                   pl.BlockSpec((2, N_DIR, w), lambda j: (0, 0, j))],
        out_shape=[jax.ShapeDtypeStruct((nb, lw, 2 * N_DIR * w), BF16),
                   jax.ShapeDtypeStruct((nb, 2 * N_DIR * w, lw), BF16),
                   jax.ShapeDtypeStruct((nb, N_DIR, lw, LANES), F32),
                   jax.ShapeDtypeStruct((2, N_DIR, g * n), F32)],
        compiler_params=_params(("parallel",)),
        name="s5_params",
    )(lane_row(a_re), lane_row(a_im), ls, bt(b_re), bt(b_im), ct(c_re), ct(c_im))
    w_t = pl.pallas_call(
        _s5_toep_kernel,
        grid=(nb,),
        in_specs=[pl.BlockSpec((1, N_DIR, lw, LANES), lambda j: (j, 0, 0, 0))],
        out_specs=pl.BlockSpec((1, lw, lw), lambda j: (j, 0, 0)),
        out_shape=jax.ShapeDtypeStruct((nb, lw, lw), BF16),
        compiler_params=_params(("parallel",)),
        name="s5_toeplitz",
    )(kbd)
    return w_in, w_t, w_m, lam


def _chunk_rows(u_ref, nct):
    return jnp.concatenate([u_ref[0, pl.ds(s, nct, stride=S5_CHUNK), :] for s in range(S5_CHUNK)],
                           axis=1).astype(BF16)


def _s5_in_kernel(uc_ref, ul_ref, w_ref, *z_refs, ncc, ncl):
    v = jnp.concatenate([_chunk_rows(uc_ref, ncc), _chunk_rows(ul_ref, ncl)], axis=0)
    z = _dot(v, w_ref[0])
    wd = z_refs[0].shape[-1]
    for k in range(4):
        z_refs[k][...] = z[:ncc, k * wd:(k + 1) * wd]
        z_refs[4 + k][...] = z[ncc:, k * wd:(k + 1) * wd]


def _s5_chunk_in(u_c, u_l, w_in):
    b, tc, _ = u_c.shape
    t = u_l.shape[1]
    ncc, ncl = tc // S5_CHUNK, t // S5_CHUNK
    wd = S5_BLK * S5_STATE
    out = lambda rows: pl.BlockSpec((rows, wd), lambda j, i: (i, j))
    sds = lambda rows: jax.ShapeDtypeStruct((b * rows, S5_NBLK * wd), F32)
    outs = pl.pallas_call(
        functools.partial(_s5_in_kernel, ncc=ncc, ncl=ncl),
        grid=(S5_NBLK, b),
        in_specs=[pl.BlockSpec((1, tc, LANES), lambda j, i: (i, 0, j)),
                  pl.BlockSpec((1, t, LANES), lambda j, i: (i, 0, j)),
                  pl.BlockSpec((1,) + w_in.shape[1:], lambda j, i: (j, 0, 0))],
        out_specs=[out(ncc)] * 4 + [out(ncl)] * 4,
        out_shape=[sds(ncc)] * 4 + [sds(ncl)] * 4,
        compiler_params=_params(("parallel", "parallel")),
        name="s5_chunk_in",
    )(u_c, u_l, w_in)
    return list(outs[:4]), list(outs[4:])


SUBLANES = 8


def _cmul(ar, ai, br, bi):
    return ar * br - ai * bi, ar * bi + ai * br


def _s5_scan_tables(lr, li, row, rev):
    pw = [(lr, li)]
    for _ in range(SUBLANES - 1):
        pw.append(_cmul(pw[-1][0], pw[-1][1], lr, li))

    def by_row(power_of):
        tr, ti = jnp.zeros(row.shape, F32), jnp.zeros(row.shape, F32)
        for i in range(SUBLANES):
            k = power_of(i)
            if k:
                tr, ti = jnp.where(row == i, pw[k - 1][0], tr), jnp.where(row == i, pw[k - 1][1], ti)
        return tr, ti

    carry_w = by_row(lambda i: SUBLANES - i if rev else i + 1)
    steps = [by_row(lambda i, d=d: d if ((i <= SUBLANES - 1 - d) if rev else (i >= d)) else 0)
             for d in (1, 2, 4)]
    return carry_w, steps


def _s5_scan_tile(zr, zi, cr, ci, carry_w, steps, row, rev):
    yr, yi = zr, zi
    for d, (tr, ti) in zip((1, 2, 4), steps):
        sh = SUBLANES - d if rev else d
        ar, ai = _cmul(tr, ti, pltpu.roll(yr, sh, 0), pltpu.roll(yi, sh, 0))
        yr, yi = yr + ar, yi + ai
    ar, ai = _cmul(carry_w[0], carry_w[1], cr, ci)
    xr, xi = yr + ar, yi + ai
    sh, edge, last = (SUBLANES - 1, SUBLANES - 1, 0) if rev else (1, 0, SUBLANES - 1)
    er = jnp.where(row == edge, cr, pltpu.roll(xr, sh, 0))
    ei = jnp.where(row == edge, ci, pltpu.roll(xi, sh, 0))
    return er, ei, xr[last:last + 1], xi[last:last + 1]


def _s5_scan_kernel(cfr_ref, cfi_ref, crr_ref, cri_ref, zfr_ref, zfi_ref, zrr_ref, zri_ref, lam_ref,
                    xfr_ref, xfi_ref, xrr_ref, xri_ref, *, ncc, ncl, nb):
    tl = zfr_ref.shape[-1]
    row = lax.broadcasted_iota(jnp.int32, (SUBLANES, tl), 0)
    cw_f, st_f = _s5_scan_tables(lam_ref[0, 0:1, :], lam_ref[1, 0:1, :], row, False)
    cw_r, st_r = _s5_scan_tables(lam_ref[0, 1:2, :], lam_ref[1, 1:2, :], row, True)

    def make_step(zf, zr, nc, outs):
        nt = nc // SUBLANES

        def step(t, carry):
            new = []
            for b in range(nb):
                cfr, cfi, crr, cri = carry[b]
                rows = pl.ds(pl.multiple_of(b * nc + t * SUBLANES, SUBLANES), SUBLANES)
                er, ei, cfr, cfi = _s5_scan_tile(zf[0][rows, :], zf[1][rows, :], cfr, cfi, cw_f, st_f, row, False)
                if outs is not None:
                    outs[0][rows, :], outs[1][rows, :] = er, ei
                rows = pl.ds(pl.multiple_of(b * nc + (nt - 1 - t) * SUBLANES, SUBLANES), SUBLANES)
                er, ei, crr, cri = _s5_scan_tile(zr[0][rows, :], zr[1][rows, :], crr, cri, cw_r, st_r, row, True)
                if outs is not None:
                    outs[2][rows, :], outs[3][rows, :] = er, ei
                new.append((cfr, cfi, crr, cri))
            return tuple(new)

        return nt, step

    zero = jnp.zeros((1, tl), F32)
    carry = tuple((zero, zero, zero, zero) for _ in range(nb))
    nt, step = make_step((cfr_ref, cfi_ref), (crr_ref, cri_ref), ncc, None)
    carry = lax.fori_loop(0, nt, step, carry)
    nt, step = make_step((zfr_ref, zfi_ref), (zrr_ref, zri_ref), ncl, (xfr_ref, xfi_ref, xrr_ref, xri_ref))
    lax.fori_loop(0, nt, step, carry)


def _s5_state_scan(z_ctx, z_lat, lam, *, nb, tl):
    rc, w = z_ctx[0].shape
    rl = z_lat[0].shape[0]
    cb = pl.BlockSpec((rc, tl), lambda j: (0, j))
    zb = pl.BlockSpec((rl, tl), lambda j: (0, j))
    return pl.pallas_call(
        functools.partial(_s5_scan_kernel, ncc=rc // nb, ncl=rl // nb, nb=nb),
        grid=(w // tl,),
        in_specs=[cb] * 4 + [zb] * 4 + [pl.BlockSpec((2, N_DIR, tl), lambda j: (0, 0, j))],
        out_specs=[zb] * 4,
        out_shape=[jax.ShapeDtypeStruct((rl, w), F32)] * 4,
        compiler_params=_params(("parallel",)),
        name="s5_state_scan",
    )(*z_ctx, *z_lat, lam)


def _s5_out_kernel(u_ref, xfr_ref, xfi_ref, xrr_ref, xri_ref, wt_ref, wm_ref, y_ref, *, nct):
    x = jnp.concatenate([r[...].astype(BF16) for r in (xfr_ref, xfi_ref, xrr_ref, xri_ref)], axis=1)
    y = _dot(_chunk_rows(u_ref, nct), wt_ref[0]) + _dot(x, wm_ref[0])
    for s in range(S5_CHUNK):
        y_ref[0, pl.ds(s, nct, stride=S5_CHUNK), :] = y[:, s * LANES:(s + 1) * LANES]


def _s5_chunk_out(u, x4, w_t, w_m):
    b, t, _ = u.shape
    nct = t // S5_CHUNK
    wd = S5_BLK * S5_STATE
    xb = pl.BlockSpec((nct, wd), lambda j, i: (i, j))
    ub = pl.BlockSpec((1, t, LANES), lambda j, i: (i, 0, j))
    op = lambda a: pl.BlockSpec((1,) + a.shape[1:], lambda j, i: (j, 0, 0))
    return pl.pallas_call(
        functools.partial(_s5_out_kernel, nct=nct),
        grid=(S5_NBLK, b),
        in_specs=[ub] + [xb] * 4 + [op(w_t), op(w_m)],
        out_specs=ub,
        out_shape=jax.ShapeDtypeStruct(u.shape, F32),
        compiler_params=_params(("parallel", "parallel")),
        name="s5_chunk_out",
    )(u, *x4, w_t, w_m)


def _gelu_tanh(x):
    return 0.5 * x * (1.0 + jnp.tanh(0.7978845608028654 * (x + 0.044715 * x * x * x)))


def _merge_kernel(of_ref, or_ref, z_ref, ys_ref, u_ref, g_ref, x_ref, mod_ref,
                  dnw_ref, wa_ref, dsk_ref, wglu_ref, bglu_ref, wb_ref, wo_ref, n2w_ref,
                  xl_ref, h2_ref, *, d):
    o = of_ref[0].astype(F32) + or_ref[0].astype(F32)
    z = z_ref[0].astype(F32)
    heads = []
    for h in range(DN_HEADS):
        sl = slice(h * DN_HEAD_DIM, (h + 1) * DN_HEAD_DIM)
        oh = o[:, sl]
        on = oh * lax.rsqrt(jnp.mean(oh * oh, axis=-1, keepdims=True) + RMS_EPS) * dnw_ref[...]
        heads.append((on * _silu(z[:, sl])).astype(BF16))
    ya = _dot(jnp.concatenate(heads, axis=1), wa_ref[...])
    ys = ys_ref[0] + dsk_ref[...] * u_ref[0]
    zz = _dot(_gelu_tanh(ys).astype(BF16), wglu_ref[...]) + bglu_ref[...]
    yb = _dot((zz[:, :S5_WIDTH] * jax.nn.sigmoid(zz[:, S5_WIDTH:])).astype(BF16), wb_ref[...])
    gates = g_ref[0].astype(F32)
    mix = jax.nn.sigmoid(gates[:, :d]) * ya + jax.nn.sigmoid(gates[:, d:]) * yb
    xl = x_ref[0] + mod_ref[0, :, 2 * d:3 * d] * _dot(mix.astype(BF16), wo_ref[...])
    xl_ref[0] = xl
    hn = xl * lax.rsqrt(jnp.mean(xl * xl, axis=-1, keepdims=True) + RMS_EPS) * n2w_ref[...]
    h2_ref[0] = (hn * (1.0 + mod_ref[0, :, 4 * d:5 * d]) + mod_ref[0, :, 3 * d:4 * d]).astype(BF16)


def _mix_merge(o_f, o_r, z, ys, u, gates, x, mods, dn_norm_w, w_a_out, s5_d, w_glu, b_glu,
               w_b_out, w_o, norm2_w, *, tm):
    b, t, d = x.shape
    tok = lambda n: pl.BlockSpec((1, tm, n), lambda i, j: (i, j, 0))
    consts = [dn_norm_w.reshape(1, -1), w_a_out, s5_d.reshape(1, -1), w_glu, b_glu.reshape(1, -1),
              w_b_out, w_o, norm2_w.reshape(1, -1)]
    return pl.pallas_call(
        functools.partial(_merge_kernel, d=d),
        grid=(b, t // tm),
        in_specs=[tok(o_f.shape[-1]), tok(o_r.shape[-1]), tok(z.shape[-1]), tok(ys.shape[-1]),
                  tok(u.shape[-1]), tok(gates.shape[-1]), tok(d),
                  pl.BlockSpec((1, 1, mods.shape[-1]), lambda i, j: (i, 0, 0))]
        + [_resident(c.shape) for c in consts],
        out_specs=[tok(d), tok(d)],
        out_shape=[jax.ShapeDtypeStruct((b, t, d), F32), jax.ShapeDtypeStruct((b, t, d), BF16)],
        compiler_params=_params(("parallel", "parallel")),
        name="mix_merge",
    )(o_f, o_r, z, ys, u, gates, x, mods, *consts)


FFN_ROWS = 8
FFN_CB = 256


FFN_DOWN_GROUP = 6
FFN_AHEAD = 2


def _conv_row(e_ref, slot, part, r, cw):
    rows = [e_ref[slot, part, (r + i) * GRID_W:(r + i + 1) * GRID_W, :] for i in range(3)]
    taps = [(rows[0] * cw[j:j + 1, :] + rows[1] * cw[3 + j:4 + j, :] + rows[2] * cw[6 + j:7 + j, :]).astype(F32)
            for j in range(3)]
    left = pltpu.roll(taps[0], 1, 0)
    right = pltpu.roll(taps[2], GRID_W - 1, 0)
    sub = lax.broadcasted_iota(jnp.int32, (SUBLANES, left.shape[1]), 0)
    left = jnp.concatenate([jnp.where(sub == 0, 0.0, left[:SUBLANES]), left[SUBLANES:]], axis=0)
    right = jnp.concatenate([right[:-SUBLANES], jnp.where(sub == SUBLANES - 1, 0.0, right[-SUBLANES:])], axis=0)
    return left + taps[1] + right


def _ffn_kernel(h_ref, hp_ref, hn_ref, xl_ref, mod_ref, wup_ref, cw_ref, wd_ref, nfw_ref,
                o_ref, hext_ref, e_ref, act_ref, acc_ref, *, d, dff, nt):
    t = pl.program_id(1)
    n_out = h_ref.shape[1]
    hext_ref[0:GRID_W] = jnp.where(t == 0, jnp.zeros_like(hp_ref[0]), hp_ref[0])
    hext_ref[GRID_W:GRID_W + n_out] = h_ref[0]
    hext_ref[GRID_W + n_out:] = jnp.where(t == nt - 1, jnp.zeros_like(hn_ref[0]), hn_ref[0])
    ncb = dff // FFN_CB

    def up(k, slot):
        e_ref[slot, 0] = _dot(hext_ref[...], wup_ref[:, k * FFN_CB:(k + 1) * FFN_CB]).astype(BF16)
        e_ref[slot, 1] = _dot(hext_ref[...], wup_ref[:, dff + k * FFN_CB:dff + (k + 1) * FFN_CB]).astype(BF16)

    for k in range(FFN_AHEAD):
        up(k, k)
    done = 0
    for k in range(ncb):
        slot = k % (FFN_AHEAD + 1)
        if k + FFN_AHEAD < ncb:
            up(k + FFN_AHEAD, (k + FFN_AHEAD) % (FFN_AHEAD + 1))
        gs = slice(k * FFN_CB, (k + 1) * FFN_CB)
        cg = cw_ref[:, gs]
        cv = cw_ref[:, dff + k * FFN_CB:dff + (k + 1) * FFN_CB]
        for r in range(n_out // GRID_W):
            gate = _conv_row(e_ref, slot, 0, r, cg)
            val = _conv_row(e_ref, slot, 1, r, cv)
            act_ref[r * GRID_W:(r + 1) * GRID_W, gs] = (_silu(gate) * val).astype(BF16)
        if (k + 1) % FFN_DOWN_GROUP == 0 or k == ncb - 1:
            ks = slice(done * FFN_CB, (k + 1) * FFN_CB)
            part = _dot(act_ref[:, ks], wd_ref[ks, :])
            if done == 0:
                acc_ref[...] = part
            else:
                acc_ref[...] += part
            done = k + 1
    xo = xl_ref[0] + mod_ref[0, :, 5 * d:6 * d] * acc_ref[...]
    o_ref[0] = xo * lax.rsqrt(jnp.mean(xo * xo, axis=-1, keepdims=True) + RMS_EPS) * nfw_ref[...]


def _conv_ffn(h2, xl, mods, w_up, conv_w, w_down, norm_f_w):
    b, t, d = xl.shape
    dff = w_down.shape[0]
    tm = FFN_ROWS * GRID_W
    nt = t // tm
    nrow = t // GRID_W
    cw = conv_w.reshape(9, 2 * dff).astype(BF16)
    tok = lambda: pl.BlockSpec((1, tm, d), lambda i, j: (i, j, 0))
    return pl.pallas_call(
        functools.partial(_ffn_kernel, d=d, dff=dff, nt=nt),
        grid=(b, nt),
        in_specs=[tok(),
                  pl.BlockSpec((1, GRID_W, d), lambda i, j: (i, jnp.maximum(j * FFN_ROWS - 1, 0), 0)),
                  pl.BlockSpec((1, GRID_W, d), lambda i, j: (i, jnp.minimum((j + 1) * FFN_ROWS, nrow - 1), 0)),
                  tok(),
                  pl.BlockSpec((1, 1, mods.shape[-1]), lambda i, j: (i, 0, 0)),
                  _resident(w_up.shape), _resident(cw.shape), _resident(w_down.shape), _resident((1, d))],
        out_specs=tok(),
        out_shape=jax.ShapeDtypeStruct((b, t, d), F32),
        scratch_shapes=[pltpu.VMEM((tm + 2 * GRID_W, d), BF16),
                        pltpu.VMEM((FFN_AHEAD + 1, 2, tm + 2 * GRID_W, FFN_CB), BF16),
                        pltpu.VMEM((tm, dff), BF16),
                        pltpu.VMEM((tm, d), F32)],
        compiler_params=_params(("parallel", "parallel")),
        name="conv_ffn",
    )(h2, h2, h2, xl, mods, w_up.astype(BF16), cw, w_down.astype(BF16), norm_f_w.reshape(1, d))


def _lane_row(a):
    a = a.reshape(-1).astype(F32)
    return jnp.pad(a, (0, LANES - a.shape[0])).reshape(1, LANES)


def _wsplit_kernel(w_ref, qkv_ref, z_ref, ba_ref, u_ref, g_ref, *, offs):
    o_z, o_b, o_a, o_u, o_g = offs
    w = w_ref[...]
    pad = jnp.zeros((w.shape[0], LANES - (o_a - o_b)), F32)
    qkv_ref[...] = w[:, :o_z].astype(BF16)
    z_ref[...] = w[:, o_z:o_b].astype(BF16)
    ba_ref[...] = jnp.concatenate([w[:, o_b:o_a], pad, w[:, o_a:o_u], pad], axis=1).astype(BF16)
    u_ref[...] = w[:, o_u:o_g].astype(BF16)
    g_ref[...] = w[:, o_g:].astype(BF16)


def _split_in_proj(w, offs, *, tr=128):
    d, n = w.shape
    o_z, o_b, o_a, o_u, o_g = offs
    widths = (o_z, o_b - o_z, 2 * LANES, o_g - o_u, n - o_g)
    return pl.pallas_call(
        functools.partial(_wsplit_kernel, offs=offs),
        grid=(d // tr,),
        in_specs=[pl.BlockSpec((tr, n), lambda i: (i, 0))],
        out_specs=[pl.BlockSpec((tr, wd), lambda i: (i, 0)) for wd in widths],
        out_shape=[jax.ShapeDtypeStruct((d, wd), BF16) for wd in widths],
        compiler_params=_params(("parallel",)),
        name="split_in_proj",
    )(w)


def kernel(x, c, ctx, c_ctx, w_ada, b_ada, norm1_w, w_in, dn_conv_w, dn_a_log, dn_dt_bias, dn_norm_w,
           w_a_out, s5_a_re, s5_a_im, s5_log_step, s5_b_re, s5_b_im, s5_c_re, s5_c_im, s5_d, w_glu,
           b_glu, w_b_out, w_o, norm2_w, w_up, ffn_conv_w, w_down, norm_f_w):
    assert w_ada.shape[0] == 1, "single-layer block"
    b, t, d = x.shape
    tc = ctx.shape[1]
    nh = N_DIR * DN_HEADS
    assert b < 8, "batch rows and the context row share one 8-row modulation block"

    c_rows = jnp.zeros((8, d), F32).at[:b].set(c).at[b].set(c_ctx)
    mods = _modulation(c_rows, w_ada[0], b_ada[0]).reshape(8, 1, N_MOD * d)

    w = w_in[0]
    o_z, o_b, o_a = 3 * DN_WIDTH, 4 * DN_WIDTH, 4 * DN_WIDTH + nh
    o_u = o_a + nh
    o_g = o_u + S5_WIDTH
    wqkv, wz, wba, wu, wg = _split_in_proj(w, (o_z, o_b, o_a, o_u, o_g))

    qkv_l, ba_l, u_l, z_l, gates_l = _in_proj(x, mods, lambda i: i, norm1_w[0], wqkv, wba, wu, wz, wg, tm=1024)
    qkv_c, ba_c, u_c = _in_proj(ctx, mods, lambda i: b, norm1_w[0], wqkv, wba, wu, tm=tc)

    alog_row, dtb_row = _lane_row(dn_a_log[0]), _lane_row(dn_dt_bias[0])
    s0 = jnp.zeros((b, nh, DN_HEAD_DIM, DN_HEAD_DIM), F32)
    wy_c = _delta_wy(qkv_c, ba_c, dn_conv_w[0], alog_row, dtb_row, ts=tc)
    wy_l = _delta_wy(qkv_l, ba_l, dn_conv_w[0], alog_row, dtb_row, ts=256)
    _, _, s_ctx = _delta_scan(*wy_c, s0, ts=tc)
    o_f, o_r, _ = _delta_scan(*wy_l, s_ctx, ts=256)

    w_s5in, w_s5t, w_s5m, lam = _s5_operators(s5_a_re[0], s5_a_im[0], s5_log_step[0], s5_b_re[0],
                                              s5_b_im[0], s5_c_re[0], s5_c_im[0])
    zs_c, zs_l = _s5_chunk_in(u_c, u_l, w_s5in)
    x_l = _s5_state_scan(zs_c, zs_l, lam, nb=b, tl=256)
    y_s5 = _s5_chunk_out(u_l, x_l, w_s5t, w_s5m)

    xl, h2 = _mix_merge(o_f, o_r, z_l, y_s5, u_l, gates_l, x, mods, dn_norm_w[0], w_a_out[0].astype(BF16),
                        s5_d[0], w_glu[0].astype(BF16), b_glu[0], w_b_out[0].astype(BF16),
                        w_o[0].astype(BF16), norm2_w[0], tm=512)
    return _conv_ffn(h2, xl, mods, w_up[0], ffn_conv_w[0], w_down[0], norm_f_w)
```

```python
import functools

import jax
import jax.numpy as jnp
from jax import lax
from jax.experimental import pallas as pl
from jax.experimental.pallas import tpu as pltpu

F32 = jnp.float32
BF16 = jnp.bfloat16

GRID_W = 64
N_DIR = 2
DN_HEADS = 4
DN_HEAD_DIM = 128
DN_WIDTH = DN_HEADS * DN_HEAD_DIM
DN_CHUNK = 64
S5_WIDTH = 512
S5_GROUP = 16
S5_GROUPS = S5_WIDTH // S5_GROUP
S5_STATE = 64
S5_CHUNK = 16
N_MOD = 6
RMS_EPS = 1e-6
L2_EPS = 1e-6
LANES = 128
S5_BLK = LANES // S5_GROUP
S5_NBLK = S5_GROUPS // S5_BLK
VMEM_LIMIT = 56 * 1024 * 1024


def _dot(a, b):
    return jnp.dot(a, b, preferred_element_type=F32)


def _silu(x):
    return x * jax.nn.sigmoid(x)


def _softplus(x):
    return jnp.maximum(x, 0.0) + jnp.log(1.0 + jnp.exp(-jnp.abs(x)))


def _params(sem, vmem=VMEM_LIMIT, flags=None):
    return pltpu.CompilerParams(dimension_semantics=sem, vmem_limit_bytes=vmem, flags=flags)


def _resident(shape):
    nd = len(shape)
    return pl.BlockSpec(shape, lambda *_: (0,) * nd, pipeline_mode=pl.Buffered(1))


def _mod_kernel(c_ref, w_ref, b_ref, o_ref):
    sc = _silu(c_ref[...])
    o_ref[...] = _dot(sc.astype(BF16), w_ref[...].astype(BF16)) + b_ref[...]


def _modulation(c_rows, w_ada, b_ada):
    d, n = w_ada.shape
    tn = n // 4
    return pl.pallas_call(
        _mod_kernel,
        grid=(n // tn,),
        in_specs=[pl.BlockSpec(c_rows.shape, lambda j: (0, 0)),
                  pl.BlockSpec((d, tn), lambda j: (0, j)),
                  pl.BlockSpec((1, tn), lambda j: (0, j))],
        out_specs=pl.BlockSpec((c_rows.shape[0], tn), lambda j: (0, j)),
        out_shape=jax.ShapeDtypeStruct((c_rows.shape[0], n), F32),
        compiler_params=_params(("arbitrary",)),
        name="adaln_mod",
    )(c_rows, w_ada, b_ada.reshape(1, n))


def _inproj_kernel(x_ref, mod_ref, nw_ref, wqkv_ref, wba_ref, wu_ref, wz_ref, wg_ref,
                   qkv_ref, ba_ref, u_ref, z_ref, g_ref, *, d):
    x = x_ref[0]
    ms = jnp.mean(x * x, axis=-1, keepdims=True)
    h = x * lax.rsqrt(ms + RMS_EPS) * nw_ref[...]
    shift = mod_ref[0, :, 0:d]
    scale = mod_ref[0, :, d:2 * d]
    hb = (h * (1.0 + scale) + shift).astype(BF16)
    qkv_ref[0] = _dot(hb, wqkv_ref[...]).astype(BF16)
    ba_ref[0] = _dot(hb, wba_ref[...])
    u_ref[0] = _dot(hb, wu_ref[...])
    if z_ref is not None:
        z_ref[0] = _dot(hb, wz_ref[...]).astype(BF16)
        g_ref[0] = _dot(hb, wg_ref[...]).astype(BF16)


def _inproj_ctx_kernel(x_ref, mod_ref, nw_ref, wqkv_ref, wba_ref, wu_ref,
                       qkv_ref, ba_ref, u_ref, *, d):
    _inproj_kernel(x_ref, mod_ref, nw_ref, wqkv_ref, wba_ref, wu_ref, None, None,
                   qkv_ref, ba_ref, u_ref, None, None, d=d)


def _in_proj(x, mods, mod_row0, norm_w, wqkv, wba, wu, wz=None, wg=None, *, tm):
    b, t, d = x.shape
    full = wz is not None
    tok = lambda n: pl.BlockSpec((1, tm, n), lambda i, j: (i, j, 0))
    in_specs = [tok(d),
                pl.BlockSpec((1, 1, mods.shape[-1]), lambda i, j: (mod_row0(i), 0, 0)),
                _resident((1, d)), _resident(wqkv.shape), _resident(wba.shape),
                _resident(wu.shape)]
    args = [x, mods, norm_w.reshape(1, d), wqkv, wba, wu]
    out_specs = [tok(wqkv.shape[1]), tok(wba.shape[1]), tok(wu.shape[1])]
    out_shape = [jax.ShapeDtypeStruct((b, t, wqkv.shape[1]), BF16),
                 jax.ShapeDtypeStruct((b, t, wba.shape[1]), F32),
                 jax.ShapeDtypeStruct((b, t, wu.shape[1]), F32)]
    if full:
        in_specs += [_resident(wz.shape), _resident(wg.shape)]
        args += [wz, wg]
        out_specs += [tok(wz.shape[1]), tok(wg.shape[1])]
        out_shape += [jax.ShapeDtypeStruct((b, t, wz.shape[1]), BF16),
                      jax.ShapeDtypeStruct((b, t, wg.shape[1]), BF16)]
    body = functools.partial(_inproj_kernel if full else _inproj_ctx_kernel, d=d)
    return pl.pallas_call(
        body, grid=(b, t // tm), in_specs=in_specs, out_specs=out_specs, out_shape=out_shape,
        compiler_params=_params(("parallel", "parallel")),
        name="in_proj" if full else "in_proj_ctx",
    )(*args)


HALO = 16


def _delta_inputs(x_ref, xp_ref, xn_ref, ba_ref, cw_ref, alog_ref, dtb_ref, *, tm, nt):
    t = pl.program_id(1)
    x = x_ref[0].astype(F32)
    prow = jnp.where(t == 0, 0.0, xp_ref[0, HALO - 1:HALO, :].astype(F32))
    nrow = jnp.where(t == nt - 1, 0.0, xn_ref[0, 0:1, :].astype(F32))
    sub = lax.broadcasted_iota(jnp.int32, (SUBLANES, x.shape[1]), 0)
    xprev = pltpu.roll(x, 1, 0)
    xnext = pltpu.roll(x, tm - 1, 0)
    xprev = jnp.concatenate([jnp.where(sub == 0, prow, xprev[:SUBLANES]), xprev[SUBLANES:]], axis=0)
    xnext = jnp.concatenate([xnext[:-SUBLANES], jnp.where(sub == SUBLANES - 1, nrow, xnext[-SUBLANES:])], axis=0)
    y = _silu(xprev * cw_ref[0:1, :] + x * cw_ref[1:2, :] + xnext * cw_ref[2:3, :])
    q, k, v = [], [], []
    for h in range(DN_HEADS):
        qh = y[:, h * DN_HEAD_DIM:(h + 1) * DN_HEAD_DIM]
        kh = y[:, DN_WIDTH + h * DN_HEAD_DIM:DN_WIDTH + (h + 1) * DN_HEAD_DIM]
        qn = qh * lax.rsqrt(jnp.sum(qh * qh, axis=-1, keepdims=True) + L2_EPS)
        kn = kh * lax.rsqrt(jnp.sum(kh * kh, axis=-1, keepdims=True) + L2_EPS)
        q.append((qn * (DN_HEAD_DIM ** -0.5)).astype(BF16))
        k.append(kn.astype(BF16))
        v.append(y[:, 2 * DN_WIDTH + h * DN_HEAD_DIM:2 * DN_WIDTH + (h + 1) * DN_HEAD_DIM].astype(BF16))
    ba = ba_ref[0]
    beta = jax.nn.sigmoid(ba[:, 0:LANES])
    g = -jnp.exp(alog_ref[...]) * _softplus(ba[:, LANES:2 * LANES] + dtb_ref[...])
    return q, k, v, beta, g


def _bmm(a, b):
    return jnp.einsum('nik,nkj->nij', a, b, preferred_element_type=F32)


def _bmm_nt(a, b):
    return jnp.einsum('nik,njk->nij', a, b, preferred_element_type=F32)


def _dwy_kernel(x_ref, xp_ref, xn_ref, ba_ref, cw_ref, alog_ref, dtb_ref,
                u_ref, w_ref, qg_ref, a_ref, kdt_ref, eg_ref, *, nc, nt):
    c, dk = DN_CHUNK, DN_HEAD_DIM
    ts = nc * c
    q_h, k_h, v_h, beta, g = _delta_inputs(x_ref, xp_ref, xn_ref, ba_ref, cw_ref, alog_ref, dtb_ref,
                                           tm=ts, nt=nt)
    ii = lax.broadcasted_iota(jnp.int32, (c, c), 0)
    jj = lax.broadcasted_iota(jnp.int32, (c, c), 1)
    ti = lax.broadcasted_iota(jnp.int32, (ts, ts), 0)
    tj = lax.broadcasted_iota(jnp.int32, (ts, ts), 1)
    same_chunk = (ti // c) == (tj // c)
    blocks = [(ch, h) for ch in range(nc) for h in range(DN_HEADS)]
    k_l = [k_h[h][ch * c:(ch + 1) * c] for ch, h in blocks]
    q_l = [q_h[h][ch * c:(ch + 1) * c] for ch, h in blocks]
    v_l = [v_h[h][ch * c:(ch + 1) * c] for ch, h in blocks]
    kkqk = _bmm_nt(jnp.stack([jnp.concatenate([k_, q_], axis=0) for k_, q_ in zip(k_l, q_l)]),
                   jnp.stack(k_l))
    a_pad = jnp.zeros((c, dk - c), BF16)
    g_hi = g.astype(BF16)
    g_mid = (g - g_hi.astype(F32)).astype(BF16)
    g_lo = (g - g_hi.astype(F32) - g_mid.astype(F32)).astype(BF16)
    neg_l, rhs_l, kd_l, dst = [], [], [], []
    for r in range(N_DIR):
        incl = ii <= jj if r else ii >= jj
        strict = ii < jj if r else ii > jj
        tri = (same_chunk & (ti <= tj if r else ti >= tj)).astype(BF16)
        g_cum = _dot(tri, g_hi) + _dot(tri, g_mid) + _dot(tri, g_lo)
        g_cum_t = g_cum.T
        for n_, (ch, h) in enumerate(blocks):
            s = r * DN_HEADS + h
            rs = slice(ch * c, (ch + 1) * c)
            ls = slice(s * dk, (s + 1) * dk)
            g_c = jnp.broadcast_to(g_cum[rs, s:s + 1], (c, dk))
            b_c = jnp.broadcast_to(beta[rs, s:s + 1], (c, dk))
            g_r = g_cum_t[s:s + 1, rs]
            g_end = g_c[0:1] if r else g_c[c - 1:c]
            decay = jnp.where(incl, jnp.exp(jnp.where(incl, g_c[:, :c] - g_r, 0.0)), 0.0)
            neg_l.append(jnp.where(strict, kkqk[n_, :c] * (-b_c[:, :c]) * decay, 0.0))
            eg = jnp.exp(g_c)
            kf = k_l[n_].astype(F32)
            rhs_l.append(jnp.concatenate([(v_l[n_].astype(F32) * b_c).astype(BF16),
                                          (kf * (b_c * eg)).astype(BF16)], axis=1))
            qg_ref[0, rs, ls] = (q_l[n_].astype(F32) * eg).astype(BF16)
            a_ref[0, rs, ls] = jnp.concatenate([(kkqk[n_, c:] * decay).astype(BF16), a_pad], axis=1)
            kd_l.append((kf * jnp.exp(g_end - g_c)).astype(BF16))
            eg_ref[0, ch, s:s + 1, :] = jnp.exp(g_end)
            dst.append((rs, ls, ch, s))
    eye_k = (lax.broadcasted_iota(jnp.int32, (dk, dk), 0)
             == lax.broadcasted_iota(jnp.int32, (dk, dk), 1)).astype(BF16)
    kdt = _bmm_nt(jnp.broadcast_to(eye_k, (len(kd_l), dk, dk)), jnp.stack(kd_l))
    for n_, (_, _, ch, s) in enumerate(dst):
        kdt_ref[0, ch, s] = kdt[n_].astype(BF16)
    wi = lax.broadcasted_iota(jnp.int32, (c, 2 * c), 0)
    wj = lax.broadcasted_iota(jnp.int32, (c, 2 * c), 1)
    right = wj >= c
    eye_r = (wj - c == wi).astype(F32)
    a = jnp.stack(neg_l)
    a_wide = jnp.concatenate([a, jnp.zeros_like(a)], axis=2)
    ps = _bmm(a.astype(BF16), (a_wide + eye_r).astype(BF16)) + eye_r
    m = 2
    while m < c:
        ps = _bmm(ps[:, :, :c].astype(BF16), ps.astype(BF16)) + jnp.where(right, ps, 0.0)
        m *= 2
    rhs = jnp.stack(rhs_l)
    sol = _bmm(ps.astype(BF16), jnp.concatenate([jnp.zeros_like(rhs), rhs], axis=1))
    for n_, (rs, ls, _, _) in enumerate(dst):
        u_ref[0, rs, ls] = sol[n_, :, :dk].astype(BF16)
        w_ref[0, rs, ls] = sol[n_, :, dk:].astype(BF16)


def _delta_wy(qkv, ba, conv_w, alog_row, dtb_row, *, ts):
    b, t, c3 = qkv.shape
    w = c3 // 3
    nc = ts // DN_CHUNK
    ns = N_DIR * DN_HEADS
    nt = t // ts
    r = ts // HALO
    tok = lambda n: pl.BlockSpec((1, ts, n), lambda i, j: (i, j, 0))
    return pl.pallas_call(
        functools.partial(_dwy_kernel, nc=nc, nt=nt),
        grid=(b, nt),
        in_specs=[tok(c3),
                  pl.BlockSpec((1, HALO, c3), lambda i, j: (i, jnp.maximum(j * r - 1, 0), 0)),
                  pl.BlockSpec((1, HALO, c3), lambda i, j: (i, jnp.minimum((j + 1) * r, t // HALO - 1), 0)),
                  tok(ba.shape[-1]),
                  _resident(conv_w.shape), _resident(alog_row.shape), _resident(dtb_row.shape)],
        out_specs=[tok(N_DIR * w)] * 4
        + [pl.BlockSpec((1, nc, ns, DN_HEAD_DIM, DN_CHUNK), lambda i, j: (i, j, 0, 0, 0)),
           pl.BlockSpec((1, nc, ns, LANES), lambda i, j: (i, j, 0, 0))],
        out_shape=[jax.ShapeDtypeStruct((b, t, N_DIR * w), BF16)] * 4
        + [jax.ShapeDtypeStruct((b, t // DN_CHUNK, ns, DN_HEAD_DIM, DN_CHUNK), BF16),
           jax.ShapeDtypeStruct((b, t // DN_CHUNK, ns, LANES), F32)],
        compiler_params=_params(("parallel", "parallel")),
        name="delta_wy",
    )(qkv, qkv, qkv, ba, conv_w, alog_row, dtb_row)


def _dscan_kernel(uf_ref, wf_ref, qgf_ref, af_ref, kdf_ref, egf_ref,
                  ur_ref, wr_ref, qgr_ref, ar_ref, kdr_ref, egr_ref, s0_ref,
                  of_ref, or_ref, sfin_ref, s_ref, *, nc, nb):
    i = pl.program_id(0)

    @pl.when(i == 0)
    def _():
        s_ref[...] = s0_ref[...]

    c, dk = DN_CHUNK, DN_HEAD_DIM
    dirs = ((uf_ref, wf_ref, qgf_ref, af_ref, kdf_ref, egf_ref, of_ref),
            (ur_ref, wr_ref, qgr_ref, ar_ref, kdr_ref, egr_ref, or_ref))

    def chunk_step(j, carry):
        wq_l, u_l, a_l, kd_l, eg_l, s_l, dst = [], [], [], [], [], [], []
        for r, (u_ref, w_ref, qg_ref, a_ref, kd_ref, eg_ref, o_ref) in enumerate(dirs):
            ch = nc - 1 - j if r else j
            rs = pl.ds(pl.multiple_of(ch * c, c), c)
            for b in range(nb):
                for h in range(DN_HEADS):
                    hs = slice(h * dk, (h + 1) * dk)
                    s = r * DN_HEADS + h
                    wq_l.append(jnp.concatenate([w_ref[b, rs, hs], qg_ref[b, rs, hs]], axis=0))
                    u_l.append(u_ref[b, rs, hs])
                    a_l.append(a_ref[b, rs, hs][:, :c])
                    kd_l.append(kd_ref[b, ch, h])
                    eg_l.append(eg_ref[b, ch, s:s + 1, :])
                    s_l.append(s_ref[b, s])
                    dst.append((o_ref, b, rs, hs, s))
        st = jnp.stack(s_l)
        ws_qs = _bmm(jnp.stack(wq_l), st.astype(BF16))
        v_new = (jnp.stack(u_l).astype(F32) - ws_qs[:, :c]).astype(BF16)
        o = ws_qs[:, c:] + _bmm(jnp.stack(a_l), v_new)
        s_new = st * jnp.stack(eg_l) + _bmm(jnp.stack(kd_l), v_new)
        for n_, (o_ref, b, rs, hs, s) in enumerate(dst):
            o_ref[b, rs, hs] = o[n_].astype(BF16)
            s_ref[b, s] = s_new[n_]
        return carry

    lax.fori_loop(0, nc, chunk_step, 0)

    @pl.when(i == pl.num_programs(0) - 1)
    def _():
        sfin_ref[...] = s_ref[...]


def _delta_scan(u, w, qg, a, kdt, eg, s0, *, ts):
    b, t, w2 = u.shape
    wd = w2 // N_DIR
    n = t // ts
    nc = ts // DN_CHUNK
    fwd = pl.BlockSpec((b, ts, wd), lambda i: (0, i, 0))
    rev = pl.BlockSpec((b, ts, wd), lambda i: (0, n - 1 - i, 1))
    kd_f = pl.BlockSpec((b, nc, DN_HEADS) + kdt.shape[3:], lambda i: (0, i, 0, 0, 0))
    kd_r = pl.BlockSpec((b, nc, DN_HEADS) + kdt.shape[3:], lambda i: (0, n - 1 - i, 1, 0, 0))
    eg_f = pl.BlockSpec((b, nc) + eg.shape[2:], lambda i: (0, i, 0, 0))
    eg_r = pl.BlockSpec((b, nc) + eg.shape[2:], lambda i: (0, n - 1 - i, 0, 0))
    out_f = pl.BlockSpec((b, ts, wd), lambda i: (0, i, 0))
    out_r = pl.BlockSpec((b, ts, wd), lambda i: (0, n - 1 - i, 0))
    return pl.pallas_call(
        functools.partial(_dscan_kernel, nc=nc, nb=b),
        grid=(n,),
        in_specs=[fwd, fwd, fwd, fwd, kd_f, eg_f, rev, rev, rev, rev, kd_r, eg_r, _resident(s0.shape)],
        out_specs=[out_f, out_r, pl.BlockSpec(s0.shape, lambda i: (0, 0, 0, 0))],
        out_shape=[jax.ShapeDtypeStruct((b, t, wd), BF16)] * 2 + [jax.ShapeDtypeStruct(s0.shape, F32)],
        scratch_shapes=[pltpu.VMEM(s0.shape, F32)],
        compiler_params=_params(("arbitrary",)),
        name="delta_scan",
    )(u, w, qg, a, kdt, eg, u, w, qg, a, kdt, eg, s0)


S5_POW_ROWS = 24


def _cexp(re, im):
    m = jnp.exp(re)
    return m * jnp.cos(im), m * jnp.sin(im)


def _s5_op_kernel(are_ref, aim_ref, ls_ref, btr_ref, bti_ref, ctr_ref, cti_ref,
                  win_ref, wm_ref, kbd_ref, lam_ref):
    l, p, n = S5_CHUNK, S5_GROUP, S5_STATE
    w = S5_BLK * n
    mask_w = (lax.broadcasted_iota(jnp.int32, (LANES, w), 0) // p
              == lax.broadcasted_iota(jnp.int32, (LANES, w), 1) // n)
    mask_c = (lax.broadcasted_iota(jnp.int32, (w, LANES), 0) // n
              == lax.broadcasted_iota(jnp.int32, (w, LANES), 1) // p)
    for r in range(N_DIR):
        are, aim = are_ref[r, 0], aim_ref[r, 0]
        dt = jnp.exp(ls_ref[r, 0])
        kk = lax.broadcasted_iota(jnp.int32, (S5_POW_ROWS, w), 0).astype(F32)
        pr, pi = _cexp(kk * (are * dt), kk * (aim * dt))
        lbr, lbi = pr[1:2], pi[1:2]
        den = are * are + aim * aim
        nr, ni = lbr - 1.0, lbi
        cr = (nr * are + ni * aim) / den
        ci = (ni * are - nr * aim) / den
        bre, bim = btr_ref[r, 0], bti_ref[r, 0]
        bmr = jnp.where(mask_w, jnp.concatenate([cr * bre - ci * bim] * S5_BLK, axis=0), 0.0)
        bmi = jnp.where(mask_w, jnp.concatenate([cr * bim + ci * bre] * S5_BLK, axis=0), 0.0)
        wr_l, wi_l = [], []
        for k in range(l):
            wr = bmr * pr[k:k + 1] - bmi * pi[k:k + 1]
            wi = bmr * pi[k:k + 1] + bmi * pr[k:k + 1]
            s = k if r else l - 1 - k
            win_ref[0, s * LANES:(s + 1) * LANES, (2 * r) * w:(2 * r + 1) * w] = wr.astype(BF16)
            win_ref[0, s * LANES:(s + 1) * LANES, (2 * r + 1) * w:(2 * r + 2) * w] = wi.astype(BF16)
            wr_l.append(wr)
            wi_l.append(wi)
        ctr = jnp.where(mask_c, ctr_ref[r, 0], 0.0)
        cti = jnp.where(mask_c, cti_ref[r, 0], 0.0)
        kbd_ref[0, r] = (_dot(jnp.concatenate(wr_l, axis=0).astype(BF16), ctr.astype(BF16))
                         - _dot(jnp.concatenate(wi_l, axis=0).astype(BF16), cti.astype(BF16)))
        zpad = jnp.zeros((LANES - S5_POW_ROWS, w), F32)
        ptr = jnp.concatenate([pr, zpad], axis=0).T
        pti = jnp.concatenate([pi, zpad], axis=0).T
        for s in range(l):
            e = l - s if r else s + 1
            er, ei = ptr[:, e:e + 1], pti[:, e:e + 1]
            wm_ref[0, (2 * r) * w:(2 * r + 1) * w, s * LANES:(s + 1) * LANES] = (ctr * er - cti * ei).astype(BF16)
            wm_ref[0, (2 * r + 1) * w:(2 * r + 2) * w, s * LANES:(s + 1) * LANES] = (
                -(ctr * ei + cti * er)).astype(BF16)
        lam_ref[0, r:r + 1, :] = pr[l:l + 1]
        lam_ref[1, r:r + 1, :] = pi[l:l + 1]


def _s5_toep_kernel(kbd_ref, wt_ref):
    l = S5_CHUNK
    k0 = (kbd_ref[0, 0, 0:LANES, :] + kbd_ref[0, 1, 0:LANES, :]).astype(BF16)
    kf = [kbd_ref[0, 0, d * LANES:(d + 1) * LANES, :].astype(BF16) for d in range(l)]
    kr = [kbd_ref[0, 1, d * LANES:(d + 1) * LANES, :].astype(BF16) for d in range(l)]
    for sp in range(l):
        for s in range(l):
            blk = kf[s - sp] if s > sp else kr[sp - s] if s < sp else k0
            wt_ref[0, sp * LANES:(sp + 1) * LANES, s * LANES:(s + 1) * LANES] = blk


def _s5_operators(a_re, a_im, log_step, b_re, b_im, c_re, c_im):
    g, n, p, l = S5_GROUPS, S5_STATE, S5_GROUP, S5_CHUNK
    nb, g8 = S5_NBLK, S5_BLK
    w = g8 * n
    lane_row = lambda a: a.reshape(N_DIR, nb, 1, w)
    ls = lane_row(jnp.repeat(log_step, n, axis=1))
    bt = lambda b: jnp.transpose(b.reshape(N_DIR, nb, g8, n, p), (0, 1, 4, 2, 3)).reshape(N_DIR, nb, p, w)
    ct = lambda c: jnp.tile(jnp.swapaxes(c, 2, 3).reshape(N_DIR, nb, w, p), (1, 1, 1, g8))
    blk = lambda r, c: pl.BlockSpec((N_DIR, 1, r, c), lambda j: (0, j, 0, 0))
    ops = lambda r, c: pl.BlockSpec((1, r, c), lambda j: (j, 0, 0))
    lw = l * LANES
    w_in, w_m, kbd, lam = pl.pallas_call(
        _s5_op_kernel,
        grid=(nb,),
        in_specs=[blk(1, w)] * 3 + [blk(p, w)] * 2 + [blk(w, LANES)] * 2,
        out_specs=[ops(lw, 2 * N_DIR * w), ops(2 * N_DIR * w, lw),
                   pl.BlockSpec((1, N_DIR, lw, LANES), lambda j: (j, 0, 0, 0)),
                   pl.BlockSpec((2, N_DIR, w), lambda j: (0, 0, j))],
        out_shape=[jax.ShapeDtypeStruct((nb, lw, 2 * N_DIR * w), BF16),
                   jax.ShapeDtypeStruct((nb, 2 * N_DIR * w, lw), BF16),
                   jax.ShapeDtypeStruct((nb, N_DIR, lw, LANES), F32),
                   jax.ShapeDtypeStruct((2, N_DIR, g * n), F32)],
        compiler_params=_params(("parallel",)),
        name="s5_params",
    )(lane_row(a_re), lane_row(a_im), ls, bt(b_re), bt(b_im), ct(c_re), ct(c_im))
    w_t = pl.pallas_call(
        _s5_toep_kernel,
        grid=(nb,),
        in_specs=[pl.BlockSpec((1, N_DIR, lw, LANES), lambda j: (j, 0, 0, 0))],
        out_specs=pl.BlockSpec((1, lw, lw), lambda j: (j, 0, 0)),
        out_shape=jax.ShapeDtypeStruct((nb, lw, lw), BF16),
        compiler_params=_params(("parallel",)),
        name="s5_toeplitz",
    )(kbd)
    return w_in, w_t, w_m, lam


def _chunk_rows(u_ref, nct):
    return jnp.concatenate([u_ref[0, pl.ds(s, nct, stride=S5_CHUNK), :] for s in range(S5_CHUNK)],
                           axis=1).astype(BF16)


def _s5_in_kernel(uc_ref, ul_ref, w_ref, *z_refs, ncc, ncl):
    v = jnp.concatenate([_chunk_rows(uc_ref, ncc), _chunk_rows(ul_ref, ncl)], axis=0)
    z = _dot(v, w_ref[0])
    wd = z_refs[0].shape[-1]
    for k in range(4):
        z_refs[k][...] = z[:ncc, k * wd:(k + 1) * wd]
        z_refs[4 + k][...] = z[ncc:, k * wd:(k + 1) * wd]


def _s5_chunk_in(u_c, u_l, w_in):
    b, tc, _ = u_c.shape
    t = u_l.shape[1]
    ncc, ncl = tc // S5_CHUNK, t // S5_CHUNK
    wd = S5_BLK * S5_STATE
    out = lambda rows: pl.BlockSpec((rows, wd), lambda j, i: (i, j))
    sds = lambda rows: jax.ShapeDtypeStruct((b * rows, S5_NBLK * wd), F32)
    outs = pl.pallas_call(
        functools.partial(_s5_in_kernel, ncc=ncc, ncl=ncl),
        grid=(S5_NBLK, b),
        in_specs=[pl.BlockSpec((1, tc, LANES), lambda j, i: (i, 0, j)),
                  pl.BlockSpec((1, t, LANES), lambda j, i: (i, 0, j)),
                  pl.BlockSpec((1,) + w_in.shape[1:], lambda j, i: (j, 0, 0))],
        out_specs=[out(ncc)] * 4 + [out(ncl)] * 4,
        out_shape=[sds(ncc)] * 4 + [sds(ncl)] * 4,
        compiler_params=_params(("parallel", "parallel")),
        name="s5_chunk_in",
    )(u_c, u_l, w_in)
    return list(outs[:4]), list(outs[4:])


SUBLANES = 8


def _cmul(ar, ai, br, bi):
    return ar * br - ai * bi, ar * bi + ai * br


def _s5_scan_tables(lr, li, row, rev):
    pw = [(lr, li)]
    for _ in range(SUBLANES - 1):
        pw.append(_cmul(pw[-1][0], pw[-1][1], lr, li))

    def by_row(power_of):
        tr, ti = jnp.zeros(row.shape, F32), jnp.zeros(row.shape, F32)
        for i in range(SUBLANES):
            k = power_of(i)
            if k:
                tr, ti = jnp.where(row == i, pw[k - 1][0], tr), jnp.where(row == i, pw[k - 1][1], ti)
        return tr, ti

    carry_w = by_row(lambda i: SUBLANES - i if rev else i + 1)
    steps = [by_row(lambda i, d=d: d if ((i <= SUBLANES - 1 - d) if rev else (i >= d)) else 0)
             for d in (1, 2, 4)]
    return carry_w, steps


def _s5_scan_tile(zr, zi, cr, ci, carry_w, steps, row, rev):
    yr, yi = zr, zi
    for d, (tr, ti) in zip((1, 2, 4), steps):
        sh = SUBLANES - d if rev else d
        ar, ai = _cmul(tr, ti, pltpu.roll(yr, sh, 0), pltpu.roll(yi, sh, 0))
        yr, yi = yr + ar, yi + ai
    ar, ai = _cmul(carry_w[0], carry_w[1], cr, ci)
    xr, xi = yr + ar, yi + ai
    sh, edge, last = (SUBLANES - 1, SUBLANES - 1, 0) if rev else (1, 0, SUBLANES - 1)
    er = jnp.where(row == edge, cr, pltpu.roll(xr, sh, 0))
    ei = jnp.where(row == edge, ci, pltpu.roll(xi, sh, 0))
    return er, ei, xr[last:last + 1], xi[last:last + 1]


def _s5_scan_kernel(cfr_ref, cfi_ref, crr_ref, cri_ref, zfr_ref, zfi_ref, zrr_ref, zri_ref, lam_ref,
                    xfr_ref, xfi_ref, xrr_ref, xri_ref, *, ncc, ncl, nb):
    tl = zfr_ref.shape[-1]
    row = lax.broadcasted_iota(jnp.int32, (SUBLANES, tl), 0)
    cw_f, st_f = _s5_scan_tables(lam_ref[0, 0:1, :], lam_ref[1, 0:1, :], row, False)
    cw_r, st_r = _s5_scan_tables(lam_ref[0, 1:2, :], lam_ref[1, 1:2, :], row, True)

    def make_step(zf, zr, nc, outs):
        nt = nc // SUBLANES

        def step(t, carry):
            new = []
            for b in range(nb):
                cfr, cfi, crr, cri = carry[b]
                rows = pl.ds(pl.multiple_of(b * nc + t * SUBLANES, SUBLANES), SUBLANES)
                er, ei, cfr, cfi = _s5_scan_tile(zf[0][rows, :], zf[1][rows, :], cfr, cfi, cw_f, st_f, row, False)
                if outs is not None:
                    outs[0][rows, :], outs[1][rows, :] = er, ei
                rows = pl.ds(pl.multiple_of(b * nc + (nt - 1 - t) * SUBLANES, SUBLANES), SUBLANES)
                er, ei, crr, cri = _s5_scan_tile(zr[0][rows, :], zr[1][rows, :], crr, cri, cw_r, st_r, row, True)
                if outs is not None:
                    outs[2][rows, :], outs[3][rows, :] = er, ei
                new.append((cfr, cfi, crr, cri))
            return tuple(new)

        return nt, step

    zero = jnp.zeros((1, tl), F32)
    carry = tuple((zero, zero, zero, zero) for _ in range(nb))
    nt, step = make_step((cfr_ref, cfi_ref), (crr_ref, cri_ref), ncc, None)
    carry = lax.fori_loop(0, nt, step, carry)
    nt, step = make_step((zfr_ref, zfi_ref), (zrr_ref, zri_ref), ncl, (xfr_ref, xfi_ref, xrr_ref, xri_ref))
    lax.fori_loop(0, nt, step, carry)


def _s5_state_scan(z_ctx, z_lat, lam, *, nb, tl):
    rc, w = z_ctx[0].shape
    rl = z_lat[0].shape[0]
    cb = pl.BlockSpec((rc, tl), lambda j: (0, j))
    zb = pl.BlockSpec((rl, tl), lambda j: (0, j))
    return pl.pallas_call(
        functools.partial(_s5_scan_kernel, ncc=rc // nb, ncl=rl // nb, nb=nb),
        grid=(w // tl,),
        in_specs=[cb] * 4 + [zb] * 4 + [pl.BlockSpec((2, N_DIR, tl), lambda j: (0, 0, j))],
        out_specs=[zb] * 4,
        out_shape=[jax.ShapeDtypeStruct((rl, w), F32)] * 4,
        compiler_params=_params(("parallel",)),
        name="s5_state_scan",
    )(*z_ctx, *z_lat, lam)


def _s5_out_kernel(u_ref, xfr_ref, xfi_ref, xrr_ref, xri_ref, wt_ref, wm_ref, y_ref, *, nct):
    x = jnp.concatenate([r[...].astype(BF16) for r in (xfr_ref, xfi_ref, xrr_ref, xri_ref)], axis=1)
    y = _dot(_chunk_rows(u_ref, nct), wt_ref[0]) + _dot(x, wm_ref[0])
    for s in range(S5_CHUNK):
        y_ref[0, pl.ds(s, nct, stride=S5_CHUNK), :] = y[:, s * LANES:(s + 1) * LANES]


def _s5_chunk_out(u, x4, w_t, w_m):
    b, t, _ = u.shape
    nct = t // S5_CHUNK
    wd = S5_BLK * S5_STATE
    xb = pl.BlockSpec((nct, wd), lambda j, i: (i, j))
    ub = pl.BlockSpec((1, t, LANES), lambda j, i: (i, 0, j))
    op = lambda a: pl.BlockSpec((1,) + a.shape[1:], lambda j, i: (j, 0, 0))
    return pl.pallas_call(
        functools.partial(_s5_out_kernel, nct=nct),
        grid=(S5_NBLK, b),
        in_specs=[ub] + [xb] * 4 + [op(w_t), op(w_m)],
        out_specs=ub,
        out_shape=jax.ShapeDtypeStruct(u.shape, F32),
        compiler_params=_params(("parallel", "parallel")),
        name="s5_chunk_out",
    )(u, *x4, w_t, w_m)


def _gelu_tanh(x):
    return 0.5 * x * (1.0 + jnp.tanh(0.7978845608028654 * (x + 0.044715 * x * x * x)))


def _merge_kernel(of_ref, or_ref, z_ref, ys_ref, u_ref, g_ref, x_ref, mod_ref,
                  dnw_ref, wa_ref, dsk_ref, wglu_ref, bglu_ref, wb_ref, wo_ref, n2w_ref,
                  xl_ref, h2_ref, *, d):
    o = of_ref[0].astype(F32) + or_ref[0].astype(F32)
    z = z_ref[0].astype(F32)
    heads = []
    for h in range(DN_HEADS):
        sl = slice(h * DN_HEAD_DIM, (h + 1) * DN_HEAD_DIM)
        oh = o[:, sl]
        on = oh * lax.rsqrt(jnp.mean(oh * oh, axis=-1, keepdims=True) + RMS_EPS) * dnw_ref[...]
        heads.append((on * _silu(z[:, sl])).astype(BF16))
    ya = _dot(jnp.concatenate(heads, axis=1), wa_ref[...])
    ys = ys_ref[0] + dsk_ref[...] * u_ref[0]
    zz = _dot(_gelu_tanh(ys).astype(BF16), wglu_ref[...]) + bglu_ref[...]
    yb = _dot((zz[:, :S5_WIDTH] * jax.nn.sigmoid(zz[:, S5_WIDTH:])).astype(BF16), wb_ref[...])
    gates = g_ref[0].astype(F32)
    mix = jax.nn.sigmoid(gates[:, :d]) * ya + jax.nn.sigmoid(gates[:, d:]) * yb
    xl = x_ref[0] + mod_ref[0, :, 2 * d:3 * d] * _dot(mix.astype(BF16), wo_ref[...])
    xl_ref[0] = xl
    hn = xl * lax.rsqrt(jnp.mean(xl * xl, axis=-1, keepdims=True) + RMS_EPS) * n2w_ref[...]
    h2_ref[0] = (hn * (1.0 + mod_ref[0, :, 4 * d:5 * d]) + mod_ref[0, :, 3 * d:4 * d]).astype(BF16)


def _mix_merge(o_f, o_r, z, ys, u, gates, x, mods, dn_norm_w, w_a_out, s5_d, w_glu, b_glu,
               w_b_out, w_o, norm2_w, *, tm):
    b, t, d = x.shape
    tok = lambda n: pl.BlockSpec((1, tm, n), lambda i, j: (i, j, 0))
    consts = [dn_norm_w.reshape(1, -1), w_a_out, s5_d.reshape(1, -1), w_glu, b_glu.reshape(1, -1),
              w_b_out, w_o, norm2_w.reshape(1, -1)]
    return pl.pallas_call(
        functools.partial(_merge_kernel, d=d),
        grid=(b, t // tm),
        in_specs=[tok(o_f.shape[-1]), tok(o_r.shape[-1]), tok(z.shape[-1]), tok(ys.shape[-1]),
                  tok(u.shape[-1]), tok(gates.shape[-1]), tok(d),
                  pl.BlockSpec((1, 1, mods.shape[-1]), lambda i, j: (i, 0, 0))]
        + [_resident(c.shape) for c in consts],
        out_specs=[tok(d), tok(d)],
        out_shape=[jax.ShapeDtypeStruct((b, t, d), F32), jax.ShapeDtypeStruct((b, t, d), BF16)],
        compiler_params=_params(("parallel", "parallel")),
        name="mix_merge",
    )(o_f, o_r, z, ys, u, gates, x, mods, *consts)


FFN_ROWS = 8
FFN_CB = 256


FFN_DOWN_GROUP = 6
FFN_AHEAD = 2


def _conv_row(e_ref, slot, part, r, cw):
    rows = [e_ref[slot, part, (r + i) * GRID_W:(r + i + 1) * GRID_W, :] for i in range(3)]
    taps = [(rows[0] * cw[j:j + 1, :] + rows[1] * cw[3 + j:4 + j, :] + rows[2] * cw[6 + j:7 + j, :]).astype(F32)
            for j in range(3)]
    left = pltpu.roll(taps[0], 1, 0)
    right = pltpu.roll(taps[2], GRID_W - 1, 0)
    sub = lax.broadcasted_iota(jnp.int32, (SUBLANES, left.shape[1]), 0)
    left = jnp.concatenate([jnp.where(sub == 0, 0.0, left[:SUBLANES]), left[SUBLANES:]], axis=0)
    right = jnp.concatenate([right[:-SUBLANES], jnp.where(sub == SUBLANES - 1, 0.0, right[-SUBLANES:])], axis=0)
    return left + taps[1] + right


def _ffn_kernel(h_ref, hp_ref, hn_ref, xl_ref, mod_ref, wup_ref, cw_ref, wd_ref, nfw_ref,
                o_ref, hext_ref, e_ref, act_ref, acc_ref, *, d, dff, nt):
    t = pl.program_id(1)
    n_out = h_ref.shape[1]
    hext_ref[0:GRID_W] = jnp.where(t == 0, jnp.zeros_like(hp_ref[0]), hp_ref[0])
    hext_ref[GRID_W:GRID_W + n_out] = h_ref[0]
    hext_ref[GRID_W + n_out:] = jnp.where(t == nt - 1, jnp.zeros_like(hn_ref[0]), hn_ref[0])
    ncb = dff // FFN_CB

    def up(k, slot):
        e_ref[slot, 0] = _dot(hext_ref[...], wup_ref[:, k * FFN_CB:(k + 1) * FFN_CB]).astype(BF16)
        e_ref[slot, 1] = _dot(hext_ref[...], wup_ref[:, dff + k * FFN_CB:dff + (k + 1) * FFN_CB]).astype(BF16)

    for k in range(FFN_AHEAD):
        up(k, k)
    done = 0
    for k in range(ncb):
        slot = k % (FFN_AHEAD + 1)
        if k + FFN_AHEAD < ncb:
            up(k + FFN_AHEAD, (k + FFN_AHEAD) % (FFN_AHEAD + 1))
        gs = slice(k * FFN_CB, (k + 1) * FFN_CB)
        cg = cw_ref[:, gs]
        cv = cw_ref[:, dff + k * FFN_CB:dff + (k + 1) * FFN_CB]
        for r in range(n_out // GRID_W):
            gate = _conv_row(e_ref, slot, 0, r, cg)
            val = _conv_row(e_ref, slot, 1, r, cv)
            act_ref[r * GRID_W:(r + 1) * GRID_W, gs] = (_silu(gate) * val).astype(BF16)
        if (k + 1) % FFN_DOWN_GROUP == 0 or k == ncb - 1:
            ks = slice(done * FFN_CB, (k + 1) * FFN_CB)
            part = _dot(act_ref[:, ks], wd_ref[ks, :])
            if done == 0:
                acc_ref[...] = part
            else:
                acc_ref[...] += part
            done = k + 1
    xo = xl_ref[0] + mod_ref[0, :, 5 * d:6 * d] * acc_ref[...]
    o_ref[0] = xo * lax.rsqrt(jnp.mean(xo * xo, axis=-1, keepdims=True) + RMS_EPS) * nfw_ref[...]


def _conv_ffn(h2, xl, mods, w_up, conv_w, w_down, norm_f_w):
    b, t, d = xl.shape
    dff = w_down.shape[0]
    tm = FFN_ROWS * GRID_W
    nt = t // tm
    nrow = t // GRID_W
    cw = conv_w.reshape(9, 2 * dff).astype(BF16)
    tok = lambda: pl.BlockSpec((1, tm, d), lambda i, j: (i, j, 0))
    return pl.pallas_call(
        functools.partial(_ffn_kernel, d=d, dff=dff, nt=nt),
        grid=(b, nt),
        in_specs=[tok(),
                  pl.BlockSpec((1, GRID_W, d), lambda i, j: (i, jnp.maximum(j * FFN_ROWS - 1, 0), 0)),
                  pl.BlockSpec((1, GRID_W, d), lambda i, j: (i, jnp.minimum((j + 1) * FFN_ROWS, nrow - 1), 0)),
                  tok(),
                  pl.BlockSpec((1, 1, mods.shape[-1]), lambda i, j: (i, 0, 0)),
                  _resident(w_up.shape), _resident(cw.shape), _resident(w_down.shape), _resident((1, d))],
        out_specs=tok(),
        out_shape=jax.ShapeDtypeStruct((b, t, d), F32),
        scratch_shapes=[pltpu.VMEM((tm + 2 * GRID_W, d), BF16),
                        pltpu.VMEM((FFN_AHEAD + 1, 2, tm + 2 * GRID_W, FFN_CB), BF16),
                        pltpu.VMEM((tm, dff), BF16),
                        pltpu.VMEM((tm, d), F32)],
        compiler_params=_params(("parallel", "parallel")),
        name="conv_ffn",
    )(h2, h2, h2, xl, mods, w_up.astype(BF16), cw, w_down.astype(BF16), norm_f_w.reshape(1, d))


def _lane_row(a):
    a = a.reshape(-1).astype(F32)
    return jnp.pad(a, (0, LANES - a.shape[0])).reshape(1, LANES)


def _wsplit_kernel(w_ref, qkv_ref, z_ref, ba_ref, u_ref, g_ref, *, offs):
    o_z, o_b, o_a, o_u, o_g = offs
    w = w_ref[...]
    pad = jnp.zeros((w.shape[0], LANES - (o_a - o_b)), F32)
    qkv_ref[...] = w[:, :o_z].astype(BF16)
    z_ref[...] = w[:, o_z:o_b].astype(BF16)
    ba_ref[...] = jnp.concatenate([w[:, o_b:o_a], pad, w[:, o_a:o_u], pad], axis=1).astype(BF16)
    u_ref[...] = w[:, o_u:o_g].astype(BF16)
    g_ref[...] = w[:, o_g:].astype(BF16)


def _split_in_proj(w, offs, *, tr=128):
    d, n = w.shape
    o_z, o_b, o_a, o_u, o_g = offs
    widths = (o_z, o_b - o_z, 2 * LANES, o_g - o_u, n - o_g)
    return pl.pallas_call(
        functools.partial(_wsplit_kernel, offs=offs),
        grid=(d // tr,),
        in_specs=[pl.BlockSpec((tr, n), lambda i: (i, 0))],
        out_specs=[pl.BlockSpec((tr, wd), lambda i: (i, 0)) for wd in widths],
        out_shape=[jax.ShapeDtypeStruct((d, wd), BF16) for wd in widths],
        compiler_params=_params(("parallel",)),
        name="split_in_proj",
    )(w)


def kernel(x, c, ctx, c_ctx, w_ada, b_ada, norm1_w, w_in, dn_conv_w, dn_a_log, dn_dt_bias, dn_norm_w,
           w_a_out, s5_a_re, s5_a_im, s5_log_step, s5_b_re, s5_b_im, s5_c_re, s5_c_im, s5_d, w_glu,
           b_glu, w_b_out, w_o, norm2_w, w_up, ffn_conv_w, w_down, norm_f_w):
    assert w_ada.shape[0] == 1, "single-layer block"
    b, t, d = x.shape
    tc = ctx.shape[1]
    nh = N_DIR * DN_HEADS
    assert b < 8, "batch rows and the context row share one 8-row modulation block"

    c_rows = jnp.zeros((8, d), F32).at[:b].set(c).at[b].set(c_ctx)
    mods = _modulation(c_rows, w_ada[0], b_ada[0]).reshape(8, 1, N_MOD * d)

    w = w_in[0]
    o_z, o_b, o_a = 3 * DN_WIDTH, 4 * DN_WIDTH, 4 * DN_WIDTH + nh
    o_u = o_a + nh
    o_g = o_u + S5_WIDTH
    wqkv, wz, wba, wu, wg = _split_in_proj(w, (o_z, o_b, o_a, o_u, o_g))

    qkv_l, ba_l, u_l, z_l, gates_l = _in_proj(x, mods, lambda i: i, norm1_w[0], wqkv, wba, wu, wz, wg, tm=1024)
    qkv_c, ba_c, u_c = _in_proj(ctx, mods, lambda i: b, norm1_w[0], wqkv, wba, wu, tm=tc)

    alog_row, dtb_row = _lane_row(dn_a_log[0]), _lane_row(dn_dt_bias[0])
    s0 = jnp.zeros((b, nh, DN_HEAD_DIM, DN_HEAD_DIM), F32)
    wy_c = _delta_wy(qkv_c, ba_c, dn_conv_w[0], alog_row, dtb_row, ts=tc)
    wy_l = _delta_wy(qkv_l, ba_l, dn_conv_w[0], alog_row, dtb_row, ts=256)
    _, _, s_ctx = _delta_scan(*wy_c, s0, ts=tc)
    o_f, o_r, _ = _delta_scan(*wy_l, s_ctx, ts=256)

    w_s5in, w_s5t, w_s5m, lam = _s5_operators(s5_a_re[0], s5_a_im[0], s5_log_step[0], s5_b_re[0],
                                              s5_b_im[0], s5_c_re[0], s5_c_im[0])
    zs_c, zs_l = _s5_chunk_in(u_c, u_l, w_s5in)
    x_l = _s5_state_scan(zs_c, zs_l, lam, nb=b, tl=256)
    y_s5 = _s5_chunk_out(u_l, x_l, w_s5t, w_s5m)

    xl, h2 = _mix_merge(o_f, o_r, z_l, y_s5, u_l, gates_l, x, mods, dn_norm_w[0], w_a_out[0].astype(BF16),
                        s5_d[0], w_glu[0].astype(BF16), b_glu[0], w_b_out[0].astype(BF16),
                        w_o[0].astype(BF16), norm2_w[0], tm=512)
    return _conv_ffn(h2, xl, mods, w_up[0], ffn_conv_w[0], w_down[0], norm_f_w)
```

```python
import functools

import jax
import jax.numpy as jnp
from jax import lax
from jax.experimental import pallas as pl
from jax.experimental.pallas import tpu as pltpu

F32 = jnp.float32
BF16 = jnp.bfloat16

GRID_W = 64
N_DIR = 2
DN_HEADS = 4
DN_HEAD_DIM = 128
DN_WIDTH = DN_HEADS * DN_HEAD_DIM
DN_CHUNK = 64
S5_WIDTH = 512
S5_GROUP = 16
S5_GROUPS = S5_WIDTH // S5_GROUP
S5_STATE = 64
S5_CHUNK = 16
N_MOD = 6
RMS_EPS = 1e-6
L2_EPS = 1e-6
LANES = 128
S5_BLK = LANES // S5_GROUP
S5_NBLK = S5_GROUPS // S5_BLK
VMEM_LIMIT = 56 * 1024 * 1024


def _dot(a, b):
    return jnp.dot(a, b, preferred_element_type=F32)


def _silu(x):
    return x * jax.nn.sigmoid(x)


def _softplus(x):
    return jnp.maximum(x, 0.0) + jnp.log(1.0 + jnp.exp(-jnp.abs(x)))


def _params(sem, vmem=VMEM_LIMIT, flags=None):
    return pltpu.CompilerParams(dimension_semantics=sem, vmem_limit_bytes=vmem, flags=flags)


def _resident(shape):
    nd = len(shape)
    return pl.BlockSpec(shape, lambda *_: (0,) * nd, pipeline_mode=pl.Buffered(1))


def _mod_kernel(c_ref, w_ref, b_ref, o_ref):
    sc = _silu(c_ref[...])
    o_ref[...] = _dot(sc.astype(BF16), w_ref[...].astype(BF16)) + b_ref[...]


def _modulation(c_rows, w_ada, b_ada):
    d, n = w_ada.shape
    tn = n // 4
    return pl.pallas_call(
        _mod_kernel,
        grid=(n // tn,),
        in_specs=[pl.BlockSpec(c_rows.shape, lambda j: (0, 0)),
                  pl.BlockSpec((d, tn), lambda j: (0, j)),
                  pl.BlockSpec((1, tn), lambda j: (0, j))],
        out_specs=pl.BlockSpec((c_rows.shape[0], tn), lambda j: (0, j)),
        out_shape=jax.ShapeDtypeStruct((c_rows.shape[0], n), F32),
        compiler_params=_params(("arbitrary",)),
        name="adaln_mod",
    )(c_rows, w_ada, b_ada.reshape(1, n))


def _inproj_kernel(x_ref, mod_ref, nw_ref, wqkv_ref, wba_ref, wu_ref, wz_ref, wg_ref,
                   qkv_ref, ba_ref, u_ref, z_ref, g_ref, *, d):
    x = x_ref[0]
    ms = jnp.mean(x * x, axis=-1, keepdims=True)
    h = x * lax.rsqrt(ms + RMS_EPS) * nw_ref[...]
    shift = mod_ref[0, :, 0:d]
    scale = mod_ref[0, :, d:2 * d]
    hb = (h * (1.0 + scale) + shift).astype(BF16)
    qkv_ref[0] = _dot(hb, wqkv_ref[...]).astype(BF16)
    ba_ref[0] = _dot(hb, wba_ref[...])
    u_ref[0] = _dot(hb, wu_ref[...])
    if z_ref is not None:
        z_ref[0] = _dot(hb, wz_ref[...]).astype(BF16)
        g_ref[0] = _dot(hb, wg_ref[...]).astype(BF16)


def _inproj_ctx_kernel(x_ref, mod_ref, nw_ref, wqkv_ref, wba_ref, wu_ref,
                       qkv_ref, ba_ref, u_ref, *, d):
    _inproj_kernel(x_ref, mod_ref, nw_ref, wqkv_ref, wba_ref, wu_ref, None, None,
                   qkv_ref, ba_ref, u_ref, None, None, d=d)


def _in_proj(x, mods, mod_row0, norm_w, wqkv, wba, wu, wz=None, wg=None, *, tm):
    b, t, d = x.shape
    full = wz is not None
    tok = lambda n: pl.BlockSpec((1, tm, n), lambda i, j: (i, j, 0))
    in_specs = [tok(d),
                pl.BlockSpec((1, 1, mods.shape[-1]), lambda i, j: (mod_row0(i), 0, 0)),
                _resident((1, d)), _resident(wqkv.shape), _resident(wba.shape),
                _resident(wu.shape)]
    args = [x, mods, norm_w.reshape(1, d), wqkv, wba, wu]
    out_specs = [tok(wqkv.shape[1]), tok(wba.shape[1]), tok(wu.shape[1])]
    out_shape = [jax.ShapeDtypeStruct((b, t, wqkv.shape[1]), BF16),
                 jax.ShapeDtypeStruct((b, t, wba.shape[1]), F32),
                 jax.ShapeDtypeStruct((b, t, wu.shape[1]), F32)]
    if full:
        in_specs += [_resident(wz.shape), _resident(wg.shape)]
        args += [wz, wg]
        out_specs += [tok(wz.shape[1]), tok(wg.shape[1])]
        out_shape += [jax.ShapeDtypeStruct((b, t, wz.shape[1]), BF16),
                      jax.ShapeDtypeStruct((b, t, wg.shape[1]), BF16)]
    body = functools.partial(_inproj_kernel if full else _inproj_ctx_kernel, d=d)
    return pl.pallas_call(
        body, grid=(b, t // tm), in_specs=in_specs, out_specs=out_specs, out_shape=out_shape,
        compiler_params=_params(("parallel", "parallel")),
        name="in_proj" if full else "in_proj_ctx",
    )(*args)


HALO = 16


def _delta_inputs(x_ref, xp_ref, xn_ref, ba_ref, cw_ref, alog_ref, dtb_ref, *, tm, nt):
    t = pl.program_id(1)
    x = x_ref[0].astype(F32)
    prow = jnp.where(t == 0, 0.0, xp_ref[0, HALO - 1:HALO, :].astype(F32))
    nrow = jnp.where(t == nt - 1, 0.0, xn_ref[0, 0:1, :].astype(F32))
    sub = lax.broadcasted_iota(jnp.int32, (SUBLANES, x.shape[1]), 0)
    xprev = pltpu.roll(x, 1, 0)
    xnext = pltpu.roll(x, tm - 1, 0)
    xprev = jnp.concatenate([jnp.where(sub == 0, prow, xprev[:SUBLANES]), xprev[SUBLANES:]], axis=0)
    xnext = jnp.concatenate([xnext[:-SUBLANES], jnp.where(sub == SUBLANES - 1, nrow, xnext[-SUBLANES:])], axis=0)
    y = _silu(xprev * cw_ref[0:1, :] + x * cw_ref[1:2, :] + xnext * cw_ref[2:3, :])
    q, k, v = [], [], []
    for h in range(DN_HEADS):
        qh = y[:, h * DN_HEAD_DIM:(h + 1) * DN_HEAD_DIM]
        kh = y[:, DN_WIDTH + h * DN_HEAD_DIM:DN_WIDTH + (h + 1) * DN_HEAD_DIM]
        qn = qh * lax.rsqrt(jnp.sum(qh * qh, axis=-1, keepdims=True) + L2_EPS)
        kn = kh * lax.rsqrt(jnp.sum(kh * kh, axis=-1, keepdims=True) + L2_EPS)
        q.append((qn * (DN_HEAD_DIM ** -0.5)).astype(BF16))
        k.append(kn.astype(BF16))
        v.append(y[:, 2 * DN_WIDTH + h * DN_HEAD_DIM:2 * DN_WIDTH + (h + 1) * DN_HEAD_DIM].astype(BF16))
    ba = ba_ref[0]
    beta = jax.nn.sigmoid(ba[:, 0:LANES])
    g = -jnp.exp(alog_ref[...]) * _softplus(ba[:, LANES:2 * LANES] + dtb_ref[...])
    return q, k, v, beta, g


def _bmm(a, b):
    return jnp.einsum('nik,nkj->nij', a, b, preferred_element_type=F32)


def _bmm_nt(a, b):
    return jnp.einsum('nik,njk->nij', a, b, preferred_element_type=F32)


def _dwy_kernel(x_ref, xp_ref, xn_ref, ba_ref, cw_ref, alog_ref, dtb_ref,
                u_ref, w_ref, qg_ref, a_ref, kdt_ref, eg_ref, *, nc, nt):
    c, dk = DN_CHUNK, DN_HEAD_DIM
    ts = nc * c
    q_h, k_h, v_h, beta, g = _delta_inputs(x_ref, xp_ref, xn_ref, ba_ref, cw_ref, alog_ref, dtb_ref,
                                           tm=ts, nt=nt)
    ii = lax.broadcasted_iota(jnp.int32, (c, c), 0)
    jj = lax.broadcasted_iota(jnp.int32, (c, c), 1)
    ti = lax.broadcasted_iota(jnp.int32, (ts, ts), 0)
    tj = lax.broadcasted_iota(jnp.int32, (ts, ts), 1)
    same_chunk = (ti // c) == (tj // c)
    blocks = [(ch, h) for ch in range(nc) for h in range(DN_HEADS)]
    k_l = [k_h[h][ch * c:(ch + 1) * c] for ch, h in blocks]
    q_l = [q_h[h][ch * c:(ch + 1) * c] for ch, h in blocks]
    v_l = [v_h[h][ch * c:(ch + 1) * c] for ch, h in blocks]
    kkqk = _bmm_nt(jnp.stack([jnp.concatenate([k_, q_], axis=0) for k_, q_ in zip(k_l, q_l)]),
                   jnp.stack(k_l))
    a_pad = jnp.zeros((c, dk - c), BF16)
    g_hi = g.astype(BF16)
    g_mid = (g - g_hi.astype(F32)).astype(BF16)
    g_lo = (g - g_hi.astype(F32) - g_mid.astype(F32)).astype(BF16)
    neg_l, rhs_l, kd_l, dst = [], [], [], []
    for r in range(N_DIR):
        incl = ii <= jj if r else ii >= jj
        strict = ii < jj if r else ii > jj
        tri = (same_chunk & (ti <= tj if r else ti >= tj)).astype(BF16)
        g_cum = _dot(tri, g_hi) + _dot(tri, g_mid) + _dot(tri, g_lo)
        g_cum_t = g_cum.T
        for n_, (ch, h) in enumerate(blocks):
            s = r * DN_HEADS + h
            rs = slice(ch * c, (ch + 1) * c)
            ls = slice(s * dk, (s + 1) * dk)
            g_c = jnp.broadcast_to(g_cum[rs, s:s + 1], (c, dk))
            b_c = jnp.broadcast_to(beta[rs, s:s + 1], (c, dk))
            g_r = g_cum_t[s:s + 1, rs]
            g_end = g_c[0:1] if r else g_c[c - 1:c]
            decay = jnp.where(incl, jnp.exp(jnp.where(incl, g_c[:, :c] - g_r, 0.0)), 0.0)
            neg_l.append(jnp.where(strict, kkqk[n_, :c] * (-b_c[:, :c]) * decay, 0.0))
            eg = jnp.exp(g_c)
            kf = k_l[n_].astype(F32)
            rhs_l.append(jnp.concatenate([(v_l[n_].astype(F32) * b_c).astype(BF16),
                                          (kf * (b_c * eg)).astype(BF16)], axis=1))
            qg_ref[0, rs, ls] = (q_l[n_].astype(F32) * eg).astype(BF16)
            a_ref[0, rs, ls] = jnp.concatenate([(kkqk[n_, c:] * decay).astype(BF16), a_pad], axis=1)
            kd_l.append((kf * jnp.exp(g_end - g_c)).astype(BF16))
            eg_ref[0, ch, s:s + 1, :] = jnp.exp(g_end)
            dst.append((rs, ls, ch, s))
    eye_k = (lax.broadcasted_iota(jnp.int32, (dk, dk), 0)
             == lax.broadcasted_iota(jnp.int32, (dk, dk), 1)).astype(BF16)
    kdt = _bmm_nt(jnp.broadcast_to(eye_k, (len(kd_l), dk, dk)), jnp.stack(kd_l))
    for n_, (_, _, ch, s) in enumerate(dst):
        kdt_ref[0, ch, s] = kdt[n_].astype(BF16)
    wi = lax.broadcasted_iota(jnp.int32, (c, 2 * c), 0)
    wj = lax.broadcasted_iota(jnp.int32, (c, 2 * c), 1)
    right = wj >= c
    eye_r = (wj - c == wi).astype(F32)
    a = jnp.stack(neg_l)
    a_wide = jnp.concatenate([a, jnp.zeros_like(a)], axis=2)
    ps = _bmm(a.astype(BF16), (a_wide + eye_r).astype(BF16)) + eye_r
    m = 2
    while m < c:
        ps = _bmm(ps[:, :, :c].astype(BF16), ps.astype(BF16)) + jnp.where(right, ps, 0.0)
        m *= 2
    rhs = jnp.stack(rhs_l)
    sol = _bmm(ps.astype(BF16), jnp.concatenate([jnp.zeros_like(rhs), rhs], axis=1))
    for n_, (rs, ls, _, _) in enumerate(dst):
        u_ref[0, rs, ls] = sol[n_, :, :dk].astype(BF16)
        w_ref[0, rs, ls] = sol[n_, :, dk:].astype(BF16)


def _delta_wy(qkv, ba, conv_w, alog_row, dtb_row, *, ts):
    b, t, c3 = qkv.shape
    w = c3 // 3
    nc = ts // DN_CHUNK
    ns = N_DIR * DN_HEADS
    nt = t // ts
    r = ts // HALO
    tok = lambda n: pl.BlockSpec((1, ts, n), lambda i, j: (i, j, 0))
    return pl.pallas_call(
        functools.partial(_dwy_kernel, nc=nc, nt=nt),
        grid=(b, nt),
        in_specs=[tok(c3),
                  pl.BlockSpec((1, HALO, c3), lambda i, j: (i, jnp.maximum(j * r - 1, 0), 0)),
                  pl.BlockSpec((1, HALO, c3), lambda i, j: (i, jnp.minimum((j + 1) * r, t // HALO - 1), 0)),
                  tok(ba.shape[-1]),
                  _resident(conv_w.shape), _resident(alog_row.shape), _resident(dtb_row.shape)],
        out_specs=[tok(N_DIR * w)] * 4
        + [pl.BlockSpec((1, nc, ns, DN_HEAD_DIM, DN_CHUNK), lambda i, j: (i, j, 0, 0, 0)),
           pl.BlockSpec((1, nc, ns, LANES), lambda i, j: (i, j, 0, 0))],
        out_shape=[jax.ShapeDtypeStruct((b, t, N_DIR * w), BF16)] * 4
        + [jax.ShapeDtypeStruct((b, t // DN_CHUNK, ns, DN_HEAD_DIM, DN_CHUNK), BF16),
           jax.ShapeDtypeStruct((b, t // DN_CHUNK, ns, LANES), F32)],
        compiler_params=_params(("parallel", "parallel")),
        name="delta_wy",
    )(qkv, qkv, qkv, ba, conv_w, alog_row, dtb_row)


def _dscan_chunks(dirs, s_ref, *, nc, nb):
    c, dk = DN_CHUNK, DN_HEAD_DIM

    def chunk_step(j, carry):
        wq_l, u_l, a_l, kd_l, eg_l, s_l, dst = [], [], [], [], [], [], []
        for r, (u_ref, w_ref, qg_ref, a_ref, kd_ref, eg_ref, o_ref, lane0, kd0) in enumerate(dirs):
            ch = nc - 1 - j if r else j
            rs = pl.ds(pl.multiple_of(ch * c, c), c)
            for b in range(nb):
                for h in range(DN_HEADS):
                    hs = slice(lane0 + h * dk, lane0 + (h + 1) * dk)
                    s = r * DN_HEADS + h
                    wq_l.append(jnp.concatenate([w_ref[b, rs, hs], qg_ref[b, rs, hs]], axis=0))
                    u_l.append(u_ref[b, rs, hs])
                    a_l.append(a_ref[b, rs, hs][:, :c])
                    kd_l.append(kd_ref[b, ch, kd0 + h])
                    eg_l.append(eg_ref[b, ch, s:s + 1, :])
                    s_l.append(s_ref[b, s])
                    dst.append((o_ref, b, rs, slice(h * dk, (h + 1) * dk), s))
        st = jnp.stack(s_l)
        ws_qs = _bmm(jnp.stack(wq_l), st.astype(BF16))
        v_new = (jnp.stack(u_l).astype(F32) - ws_qs[:, :c]).astype(BF16)
        o = ws_qs[:, c:] + _bmm(jnp.stack(a_l), v_new)
        s_new = st * jnp.stack(eg_l) + _bmm(jnp.stack(kd_l), v_new)
        for n_, (o_ref, b, rs, hs, s) in enumerate(dst):
            if o_ref is not None:
                o_ref[b, rs, hs] = o[n_].astype(BF16)
            s_ref[b, s] = s_new[n_]
        return carry

    lax.fori_loop(0, nc, chunk_step, 0)


def _dscan_kernel(uc_ref, wc_ref, qgc_ref, ac_ref, kdc_ref, egc_ref,
                  uf_ref, wf_ref, qgf_ref, af_ref, kdf_ref, egf_ref,
                  ur_ref, wr_ref, qgr_ref, ar_ref, kdr_ref, egr_ref,
                  of_ref, or_ref, s_ref, *, ncc, nc, nb, wd):
    @pl.when(pl.program_id(0) == 0)
    def _():
        s_ref[...] = jnp.zeros_like(s_ref)
        ctx = (uc_ref, wc_ref, qgc_ref, ac_ref, kdc_ref, egc_ref, None)
        _dscan_chunks((ctx + (0, 0), ctx + (wd, DN_HEADS)), s_ref, nc=ncc, nb=nb)

    _dscan_chunks(((uf_ref, wf_ref, qgf_ref, af_ref, kdf_ref, egf_ref, of_ref, 0, 0),
                   (ur_ref, wr_ref, qgr_ref, ar_ref, kdr_ref, egr_ref, or_ref, 0, 0)), s_ref, nc=nc, nb=nb)


def _delta_scan(wy_c, wy_l, *, ts):
    u, _, _, _, kdt, eg = wy_l
    b, t, w2 = u.shape
    wd = w2 // N_DIR
    n = t // ts
    nc = ts // DN_CHUNK
    whole = lambda a: pl.BlockSpec(a.shape, lambda i: (0,) * a.ndim)
    fwd = pl.BlockSpec((b, ts, wd), lambda i: (0, i, 0))
    rev = pl.BlockSpec((b, ts, wd), lambda i: (0, n - 1 - i, 1))
    kd_f = pl.BlockSpec((b, nc, DN_HEADS) + kdt.shape[3:], lambda i: (0, i, 0, 0, 0))
    kd_r = pl.BlockSpec((b, nc, DN_HEADS) + kdt.shape[3:], lambda i: (0, n - 1 - i, 1, 0, 0))
    eg_f = pl.BlockSpec((b, nc) + eg.shape[2:], lambda i: (0, i, 0, 0))
    eg_r = pl.BlockSpec((b, nc) + eg.shape[2:], lambda i: (0, n - 1 - i, 0, 0))
    out_f = pl.BlockSpec((b, ts, wd), lambda i: (0, i, 0))
    out_r = pl.BlockSpec((b, ts, wd), lambda i: (0, n - 1 - i, 0))
    state = (b, N_DIR * DN_HEADS, DN_HEAD_DIM, DN_HEAD_DIM)
    return pl.pallas_call(
        functools.partial(_dscan_kernel, ncc=wy_c[0].shape[1] // DN_CHUNK, nc=nc, nb=b, wd=wd),
        grid=(n,),
        in_specs=[whole(a) for a in wy_c] + [fwd, fwd, fwd, fwd, kd_f, eg_f, rev, rev, rev, rev, kd_r, eg_r],
        out_specs=[out_f, out_r],
        out_shape=[jax.ShapeDtypeStruct((b, t, wd), BF16)] * 2,
        scratch_shapes=[pltpu.VMEM(state, F32)],
        compiler_params=_params(("arbitrary",)),
        name="delta_scan",
    )(*wy_c, *wy_l, *wy_l)


S5_POW_ROWS = 24


def _cexp(re, im):
    m = jnp.exp(re)
    return m * jnp.cos(im), m * jnp.sin(im)


def _s5_op_kernel(are_ref, aim_ref, ls_ref, btr_ref, bti_ref, ctr_ref, cti_ref,
                  win_ref, wm_ref, kbd_ref, lam_ref):
    l, p, n = S5_CHUNK, S5_GROUP, S5_STATE
    w = S5_BLK * n
    mask_w = (lax.broadcasted_iota(jnp.int32, (LANES, w), 0) // p
              == lax.broadcasted_iota(jnp.int32, (LANES, w), 1) // n)
    mask_c = (lax.broadcasted_iota(jnp.int32, (w, LANES), 0) // n
              == lax.broadcasted_iota(jnp.int32, (w, LANES), 1) // p)
    for r in range(N_DIR):
        are, aim = are_ref[r, 0], aim_ref[r, 0]
        dt = jnp.exp(ls_ref[r, 0])
        kk = lax.broadcasted_iota(jnp.int32, (S5_POW_ROWS, w), 0).astype(F32)
        pr, pi = _cexp(kk * (are * dt), kk * (aim * dt))
        lbr, lbi = pr[1:2], pi[1:2]
        den = are * are + aim * aim
        nr, ni = lbr - 1.0, lbi
        cr = (nr * are + ni * aim) / den
        ci = (ni * are - nr * aim) / den
        bre, bim = btr_ref[r, 0], bti_ref[r, 0]
        bmr = jnp.where(mask_w, jnp.concatenate([cr * bre - ci * bim] * S5_BLK, axis=0), 0.0)
        bmi = jnp.where(mask_w, jnp.concatenate([cr * bim + ci * bre] * S5_BLK, axis=0), 0.0)
        wr_l, wi_l = [], []
        for k in range(l):
            wr = bmr * pr[k:k + 1] - bmi * pi[k:k + 1]
            wi = bmr * pi[k:k + 1] + bmi * pr[k:k + 1]
            s = k if r else l - 1 - k
            win_ref[0, s * LANES:(s + 1) * LANES, (2 * r) * w:(2 * r + 1) * w] = wr.astype(BF16)
            win_ref[0, s * LANES:(s + 1) * LANES, (2 * r + 1) * w:(2 * r + 2) * w] = wi.astype(BF16)
            wr_l.append(wr)
            wi_l.append(wi)
        ctr = jnp.where(mask_c, ctr_ref[r, 0], 0.0)
        cti = jnp.where(mask_c, cti_ref[r, 0], 0.0)
        kbd_ref[0, r] = (_dot(jnp.concatenate(wr_l, axis=0).astype(BF16), ctr.astype(BF16))
                         - _dot(jnp.concatenate(wi_l, axis=0).astype(BF16), cti.astype(BF16)))
        zpad = jnp.zeros((LANES - S5_POW_ROWS, w), F32)
        ptr = jnp.concatenate([pr, zpad], axis=0).T
        pti = jnp.concatenate([pi, zpad], axis=0).T
        for s in range(l):
            e = l - s if r else s + 1
            er, ei = ptr[:, e:e + 1], pti[:, e:e + 1]
            wm_ref[0, (2 * r) * w:(2 * r + 1) * w, s * LANES:(s + 1) * LANES] = (ctr * er - cti * ei).astype(BF16)
            wm_ref[0, (2 * r + 1) * w:(2 * r + 2) * w, s * LANES:(s + 1) * LANES] = (
                -(ctr * ei + cti * er)).astype(BF16)
        lam_ref[0, r:r + 1, :] = pr[l:l + 1]
        lam_ref[1, r:r + 1, :] = pi[l:l + 1]


def _s5_toep_kernel(kbd_ref, wt_ref):
    l = S5_CHUNK
    k0 = (kbd_ref[0, 0, 0:LANES, :] + kbd_ref[0, 1, 0:LANES, :]).astype(BF16)
    kf = [kbd_ref[0, 0, d * LANES:(d + 1) * LANES, :].astype(BF16) for d in range(l)]
    kr = [kbd_ref[0, 1, d * LANES:(d + 1) * LANES, :].astype(BF16) for d in range(l)]
    for sp in range(l):
        for s in range(l):
            blk = kf[s - sp] if s > sp else kr[sp - s] if s < sp else k0
            wt_ref[0, sp * LANES:(sp + 1) * LANES, s * LANES:(s + 1) * LANES] = blk


def _s5_operators(a_re, a_im, log_step, b_re, b_im, c_re, c_im):
    g, n, p, l = S5_GROUPS, S5_STATE, S5_GROUP, S5_CHUNK
    nb, g8 = S5_NBLK, S5_BLK
    w = g8 * n
    lane_row = lambda a: a.reshape(N_DIR, nb, 1, w)
    ls = lane_row(jnp.repeat(log_step, n, axis=1))
    bt = lambda b: jnp.transpose(b.reshape(N_DIR, nb, g8, n, p), (0, 1, 4, 2, 3)).reshape(N_DIR, nb, p, w)
    ct = lambda c: jnp.tile(jnp.swapaxes(c, 2, 3).reshape(N_DIR, nb, w, p), (1, 1, 1, g8))
    blk = lambda r, c: pl.BlockSpec((N_DIR, 1, r, c), lambda j: (0, j, 0, 0))
    ops = lambda r, c: pl.BlockSpec((1, r, c), lambda j: (j, 0, 0))
    lw = l * LANES
    w_in, w_m, kbd, lam = pl.pallas_call(
        _s5_op_kernel,
        grid=(nb,),
        in_specs=[blk(1, w)] * 3 + [blk(p, w)] * 2 + [blk(w, LANES)] * 2,
        out_specs=[ops(lw, 2 * N_DIR * w), ops(2 * N_DIR * w, lw),
                   pl.BlockSpec((1, N_DIR, lw, LANES), lambda j: (j, 0, 0, 0)),
                   pl.BlockSpec((2, N_DIR, w), lambda j: (0, 0, j))],
        out_shape=[jax.ShapeDtypeStruct((nb, lw, 2 * N_DIR * w), BF16),
                   jax.ShapeDtypeStruct((nb, 2 * N_DIR * w, lw), BF16),
                   jax.ShapeDtypeStruct((nb, N_DIR, lw, LANES), F32),
                   jax.ShapeDtypeStruct((2, N_DIR, g * n), F32)],
        compiler_params=_params(("parallel",)),
        name="s5_params",
    )(lane_row(a_re), lane_row(a_im), ls, bt(b_re), bt(b_im), ct(c_re), ct(c_im))
    w_t = pl.pallas_call(
        _s5_toep_kernel,
        grid=(nb,),
        in_specs=[pl.BlockSpec((1, N_DIR, lw, LANES), lambda j: (j, 0, 0, 0))],
        out_specs=pl.BlockSpec((1, lw, lw), lambda j: (j, 0, 0)),
        out_shape=jax.ShapeDtypeStruct((nb, lw, lw), BF16),
        compiler_params=_params(("parallel",)),
        name="s5_toeplitz",
    )(kbd)
    return w_in, w_t, w_m, lam


def _chunk_rows(u_ref, nct):
    return jnp.concatenate([u_ref[0, pl.ds(s, nct, stride=S5_CHUNK), :] for s in range(S5_CHUNK)],
                           axis=1).astype(BF16)


def _s5_in_kernel(uc_ref, ul_ref, w_ref, *z_refs, ncc, ncl):
    v = jnp.concatenate([_chunk_rows(uc_ref, ncc), _chunk_rows(ul_ref, ncl)], axis=0)
    z = _dot(v, w_ref[0])
    wd = z_refs[0].shape[-1]
    for k in range(4):
        z_refs[k][...] = z[:ncc, k * wd:(k + 1) * wd]
        z_refs[4 + k][...] = z[ncc:, k * wd:(k + 1) * wd]


def _s5_chunk_in(u_c, u_l, w_in):
    b, tc, _ = u_c.shape
    t = u_l.shape[1]
    ncc, ncl = tc // S5_CHUNK, t // S5_CHUNK
    wd = S5_BLK * S5_STATE
    out = lambda rows: pl.BlockSpec((rows, wd), lambda j, i: (i, j))
    sds = lambda rows: jax.ShapeDtypeStruct((b * rows, S5_NBLK * wd), F32)
    outs = pl.pallas_call(
        functools.partial(_s5_in_kernel, ncc=ncc, ncl=ncl),
        grid=(S5_NBLK, b),
        in_specs=[pl.BlockSpec((1, tc, LANES), lambda j, i: (i, 0, j)),
                  pl.BlockSpec((1, t, LANES), lambda j, i: (i, 0, j)),
                  pl.BlockSpec((1,) + w_in.shape[1:], lambda j, i: (j, 0, 0))],
        out_specs=[out(ncc)] * 4 + [out(ncl)] * 4,
        out_shape=[sds(ncc)] * 4 + [sds(ncl)] * 4,
        compiler_params=_params(("parallel", "parallel")),
        name="s5_chunk_in",
    )(u_c, u_l, w_in)
    return list(outs[:4]), list(outs[4:])


SUBLANES = 8


def _cmul(ar, ai, br, bi):
    return ar * br - ai * bi, ar * bi + ai * br


def _s5_scan_tables(lr, li, row, rev):
    pw = [(lr, li)]
    for _ in range(SUBLANES - 1):
        pw.append(_cmul(pw[-1][0], pw[-1][1], lr, li))

    def by_row(power_of):
        tr, ti = jnp.zeros(row.shape, F32), jnp.zeros(row.shape, F32)
        for i in range(SUBLANES):
            k = power_of(i)
            if k:
                tr, ti = jnp.where(row == i, pw[k - 1][0], tr), jnp.where(row == i, pw[k - 1][1], ti)
        return tr, ti

    carry_w = by_row(lambda i: SUBLANES - i if rev else i + 1)
    steps = [by_row(lambda i, d=d: d if ((i <= SUBLANES - 1 - d) if rev else (i >= d)) else 0)
             for d in (1, 2, 4)]
    return carry_w, steps


def _s5_scan_tile(zr, zi, cr, ci, carry_w, steps, row, rev):
    yr, yi = zr, zi
    for d, (tr, ti) in zip((1, 2, 4), steps):
        sh = SUBLANES - d if rev else d
        ar, ai = _cmul(tr, ti, pltpu.roll(yr, sh, 0), pltpu.roll(yi, sh, 0))
        yr, yi = yr + ar, yi + ai
    ar, ai = _cmul(carry_w[0], carry_w[1], cr, ci)
    xr, xi = yr + ar, yi + ai
    sh, edge, last = (SUBLANES - 1, SUBLANES - 1, 0) if rev else (1, 0, SUBLANES - 1)
    er = jnp.where(row == edge, cr, pltpu.roll(xr, sh, 0))
    ei = jnp.where(row == edge, ci, pltpu.roll(xi, sh, 0))
    return er, ei, xr[last:last + 1], xi[last:last + 1]


def _s5_scan_kernel(cfr_ref, cfi_ref, crr_ref, cri_ref, zfr_ref, zfi_ref, zrr_ref, zri_ref, lam_ref,
                    xfr_ref, xfi_ref, xrr_ref, xri_ref, *, ncc, ncl, nb):
    tl = zfr_ref.shape[-1]
    row = lax.broadcasted_iota(jnp.int32, (SUBLANES, tl), 0)
    cw_f, st_f = _s5_scan_tables(lam_ref[0, 0:1, :], lam_ref[1, 0:1, :], row, False)
    cw_r, st_r = _s5_scan_tables(lam_ref[0, 1:2, :], lam_ref[1, 1:2, :], row, True)

    def make_step(zf, zr, nc, outs):
        nt = nc // SUBLANES

        def step(t, carry):
            new = []
            for b in range(nb):
                cfr, cfi, crr, cri = carry[b]
                rows = pl.ds(pl.multiple_of(b * nc + t * SUBLANES, SUBLANES), SUBLANES)
                er, ei, cfr, cfi = _s5_scan_tile(zf[0][rows, :], zf[1][rows, :], cfr, cfi, cw_f, st_f, row, False)
                if outs is not None:
                    outs[0][rows, :], outs[1][rows, :] = er, ei
                rows = pl.ds(pl.multiple_of(b * nc + (nt - 1 - t) * SUBLANES, SUBLANES), SUBLANES)
                er, ei, crr, cri = _s5_scan_tile(zr[0][rows, :], zr[1][rows, :], crr, cri, cw_r, st_r, row, True)
                if outs is not None:
                    outs[2][rows, :], outs[3][rows, :] = er, ei
                new.append((cfr, cfi, crr, cri))
            return tuple(new)

        return nt, step

    zero = jnp.zeros((1, tl), F32)
    carry = tuple((zero, zero, zero, zero) for _ in range(nb))
    nt, step = make_step((cfr_ref, cfi_ref), (crr_ref, cri_ref), ncc, None)
    carry = lax.fori_loop(0, nt, step, carry)
    nt, step = make_step((zfr_ref, zfi_ref), (zrr_ref, zri_ref), ncl, (xfr_ref, xfi_ref, xrr_ref, xri_ref))
    lax.fori_loop(0, nt, step, carry)


def _s5_state_scan(z_ctx, z_lat, lam, *, nb, tl):
    rc, w = z_ctx[0].shape
    rl = z_lat[0].shape[0]
    cb = pl.BlockSpec((rc, tl), lambda j: (0, j))
    zb = pl.BlockSpec((rl, tl), lambda j: (0, j))
    return pl.pallas_call(
        functools.partial(_s5_scan_kernel, ncc=rc // nb, ncl=rl // nb, nb=nb),
        grid=(w // tl,),
        in_specs=[cb] * 4 + [zb] * 4 + [pl.BlockSpec((2, N_DIR, tl), lambda j: (0, 0, j))],
        out_specs=[zb] * 4,
        out_shape=[jax.ShapeDtypeStruct((rl, w), F32)] * 4,
        compiler_params=_params(("parallel",)),
        name="s5_state_scan",
    )(*z_ctx, *z_lat, lam)


def _s5_out_kernel(u_ref, xfr_ref, xfi_ref, xrr_ref, xri_ref, wt_ref, wm_ref, y_ref, *, nct):
    x = jnp.concatenate([r[...].astype(BF16) for r in (xfr_ref, xfi_ref, xrr_ref, xri_ref)], axis=1)
    y = _dot(_chunk_rows(u_ref, nct), wt_ref[0]) + _dot(x, wm_ref[0])
    for s in range(S5_CHUNK):
        y_ref[0, pl.ds(s, nct, stride=S5_CHUNK), :] = y[:, s * LANES:(s + 1) * LANES]


def _s5_chunk_out(u, x4, w_t, w_m):
    b, t, _ = u.shape
    nct = t // S5_CHUNK
    wd = S5_BLK * S5_STATE
    xb = pl.BlockSpec((nct, wd), lambda j, i: (i, j))
    ub = pl.BlockSpec((1, t, LANES), lambda j, i: (i, 0, j))
    op = lambda a: pl.BlockSpec((1,) + a.shape[1:], lambda j, i: (j, 0, 0))
    return pl.pallas_call(
        functools.partial(_s5_out_kernel, nct=nct),
        grid=(S5_NBLK, b),
        in_specs=[ub] + [xb] * 4 + [op(w_t), op(w_m)],
        out_specs=ub,
        out_shape=jax.ShapeDtypeStruct(u.shape, F32),
        compiler_params=_params(("parallel", "parallel")),
        name="s5_chunk_out",
    )(u, *x4, w_t, w_m)


def _gelu_tanh(x):
    return 0.5 * x * (1.0 + jnp.tanh(0.7978845608028654 * (x + 0.044715 * x * x * x)))


def _merge_kernel(of_ref, or_ref, z_ref, ys_ref, u_ref, g_ref, x_ref, mod_ref,
                  dnw_ref, wa_ref, dsk_ref, wglu_ref, bglu_ref, wb_ref, wo_ref, n2w_ref,
                  xl_ref, h2_ref, *, d):
    o = of_ref[0].astype(F32) + or_ref[0].astype(F32)
    z = z_ref[0].astype(F32)
    heads = []
    for h in range(DN_HEADS):
        sl = slice(h * DN_HEAD_DIM, (h + 1) * DN_HEAD_DIM)
        oh = o[:, sl]
        on = oh * lax.rsqrt(jnp.mean(oh * oh, axis=-1, keepdims=True) + RMS_EPS) * dnw_ref[...]
        heads.append((on * _silu(z[:, sl])).astype(BF16))
    ya = _dot(jnp.concatenate(heads, axis=1), wa_ref[...])
    ys = ys_ref[0] + dsk_ref[...] * u_ref[0]
    zz = _dot(_gelu_tanh(ys).astype(BF16), wglu_ref[...]) + bglu_ref[...]
    yb = _dot((zz[:, :S5_WIDTH] * jax.nn.sigmoid(zz[:, S5_WIDTH:])).astype(BF16), wb_ref[...])
    gates = g_ref[0].astype(F32)
    mix = jax.nn.sigmoid(gates[:, :d]) * ya + jax.nn.sigmoid(gates[:, d:]) * yb
    xl = x_ref[0] + mod_ref[0, :, 2 * d:3 * d] * _dot(mix.astype(BF16), wo_ref[...])
    xl_ref[0] = xl
    hn = xl * lax.rsqrt(jnp.mean(xl * xl, axis=-1, keepdims=True) + RMS_EPS) * n2w_ref[...]
    h2_ref[0] = (hn * (1.0 + mod_ref[0, :, 4 * d:5 * d]) + mod_ref[0, :, 3 * d:4 * d]).astype(BF16)


def _mix_merge(o_f, o_r, z, ys, u, gates, x, mods, dn_norm_w, w_a_out, s5_d, w_glu, b_glu,
               w_b_out, w_o, norm2_w, *, tm):
    b, t, d = x.shape
    tok = lambda n: pl.BlockSpec((1, tm, n), lambda i, j: (i, j, 0))
    consts = [dn_norm_w.reshape(1, -1), w_a_out, s5_d.reshape(1, -1), w_glu, b_glu.reshape(1, -1),
              w_b_out, w_o, norm2_w.reshape(1, -1)]
    return pl.pallas_call(
        functools.partial(_merge_kernel, d=d),
        grid=(b, t // tm),
        in_specs=[tok(o_f.shape[-1]), tok(o_r.shape[-1]), tok(z.shape[-1]), tok(ys.shape[-1]),
                  tok(u.shape[-1]), tok(gates.shape[-1]), tok(d),
                  pl.BlockSpec((1, 1, mods.shape[-1]), lambda i, j: (i, 0, 0))]
        + [_resident(c.shape) for c in consts],
        out_specs=[tok(d), tok(d)],
        out_shape=[jax.ShapeDtypeStruct((b, t, d), F32), jax.ShapeDtypeStruct((b, t, d), BF16)],
        compiler_params=_params(("parallel", "parallel")),
        name="mix_merge",
    )(o_f, o_r, z, ys, u, gates, x, mods, *consts)


FFN_ROWS = 8
FFN_CB = 256


FFN_DOWN_GROUP = 6
FFN_AHEAD = 2


def _conv_row(e_ref, slot, part, r, cw):
    rows = [e_ref[slot, part, (r + i) * GRID_W:(r + i + 1) * GRID_W, :] for i in range(3)]
    taps = [(rows[0] * cw[j:j + 1, :] + rows[1] * cw[3 + j:4 + j, :] + rows[2] * cw[6 + j:7 + j, :]).astype(F32)
            for j in range(3)]
    left = pltpu.roll(taps[0], 1, 0)
    right = pltpu.roll(taps[2], GRID_W - 1, 0)
    sub = lax.broadcasted_iota(jnp.int32, (SUBLANES, left.shape[1]), 0)
    left = jnp.concatenate([jnp.where(sub == 0, 0.0, left[:SUBLANES]), left[SUBLANES:]], axis=0)
    right = jnp.concatenate([right[:-SUBLANES], jnp.where(sub == SUBLANES - 1, 0.0, right[-SUBLANES:])], axis=0)
    return left + taps[1] + right


def _ffn_kernel(h_ref, hp_ref, hn_ref, xl_ref, mod_ref, wup_ref, cw_ref, wd_ref, nfw_ref,
                o_ref, hext_ref, e_ref, act_ref, acc_ref, *, d, dff, nt):
    t = pl.program_id(1)
    n_out = h_ref.shape[1]
    hext_ref[0:GRID_W] = jnp.where(t == 0, jnp.zeros_like(hp_ref[0]), hp_ref[0])
    hext_ref[GRID_W:GRID_W + n_out] = h_ref[0]
    hext_ref[GRID_W + n_out:] = jnp.where(t == nt - 1, jnp.zeros_like(hn_ref[0]), hn_ref[0])
    ncb = dff // FFN_CB

    def up(k, slot):
        e_ref[slot, 0] = _dot(hext_ref[...], wup_ref[:, k * FFN_CB:(k + 1) * FFN_CB]).astype(BF16)
        e_ref[slot, 1] = _dot(hext_ref[...], wup_ref[:, dff + k * FFN_CB:dff + (k + 1) * FFN_CB]).astype(BF16)

    for k in range(FFN_AHEAD):
        up(k, k)
    done = 0
    for k in range(ncb):
        slot = k % (FFN_AHEAD + 1)
        if k + FFN_AHEAD < ncb:
            up(k + FFN_AHEAD, (k + FFN_AHEAD) % (FFN_AHEAD + 1))
        gs = slice(k * FFN_CB, (k + 1) * FFN_CB)
        cg = cw_ref[:, gs]
        cv = cw_ref[:, dff + k * FFN_CB:dff + (k + 1) * FFN_CB]
        for r in range(n_out // GRID_W):
            gate = _conv_row(e_ref, slot, 0, r, cg)
            val = _conv_row(e_ref, slot, 1, r, cv)
            act_ref[r * GRID_W:(r + 1) * GRID_W, gs] = (_silu(gate) * val).astype(BF16)
        if (k + 1) % FFN_DOWN_GROUP == 0 or k == ncb - 1:
            ks = slice(done * FFN_CB, (k + 1) * FFN_CB)
            part = _dot(act_ref[:, ks], wd_ref[ks, :])
            if done == 0:
                acc_ref[...] = part
            else:
                acc_ref[...] += part
            done = k + 1
    xo = xl_ref[0] + mod_ref[0, :, 5 * d:6 * d] * acc_ref[...]
    o_ref[0] = xo * lax.rsqrt(jnp.mean(xo * xo, axis=-1, keepdims=True) + RMS_EPS) * nfw_ref[...]


def _conv_ffn(h2, xl, mods, w_up, conv_w, w_down, norm_f_w):
    b, t, d = xl.shape
    dff = w_down.shape[0]
    tm = FFN_ROWS * GRID_W
    nt = t // tm
    nrow = t // GRID_W
    cw = conv_w.reshape(9, 2 * dff).astype(BF16)
    tok = lambda: pl.BlockSpec((1, tm, d), lambda i, j: (i, j, 0))
    return pl.pallas_call(
        functools.partial(_ffn_kernel, d=d, dff=dff, nt=nt),
        grid=(b, nt),
        in_specs=[tok(),
                  pl.BlockSpec((1, GRID_W, d), lambda i, j: (i, jnp.maximum(j * FFN_ROWS - 1, 0), 0)),
                  pl.BlockSpec((1, GRID_W, d), lambda i, j: (i, jnp.minimum((j + 1) * FFN_ROWS, nrow - 1), 0)),
                  tok(),
                  pl.BlockSpec((1, 1, mods.shape[-1]), lambda i, j: (i, 0, 0)),
                  _resident(w_up.shape), _resident(cw.shape), _resident(w_down.shape), _resident((1, d))],
        out_specs=tok(),
        out_shape=jax.ShapeDtypeStruct((b, t, d), F32),
        scratch_shapes=[pltpu.VMEM((tm + 2 * GRID_W, d), BF16),
                        pltpu.VMEM((FFN_AHEAD + 1, 2, tm + 2 * GRID_W, FFN_CB), BF16),
                        pltpu.VMEM((tm, dff), BF16),
                        pltpu.VMEM((tm, d), F32)],
        compiler_params=_params(("parallel", "parallel")),
        name="conv_ffn",
    )(h2, h2, h2, xl, mods, w_up.astype(BF16), cw, w_down.astype(BF16), norm_f_w.reshape(1, d))


def _lane_row(a):
    a = a.reshape(-1).astype(F32)
    return jnp.pad(a, (0, LANES - a.shape[0])).reshape(1, LANES)


def _wsplit_kernel(w_ref, qkv_ref, z_ref, ba_ref, u_ref, g_ref, *, offs):
    o_z, o_b, o_a, o_u, o_g = offs
    w = w_ref[...]
    pad = jnp.zeros((w.shape[0], LANES - (o_a - o_b)), F32)
    qkv_ref[...] = w[:, :o_z].astype(BF16)
    z_ref[...] = w[:, o_z:o_b].astype(BF16)
    ba_ref[...] = jnp.concatenate([w[:, o_b:o_a], pad, w[:, o_a:o_u], pad], axis=1).astype(BF16)
    u_ref[...] = w[:, o_u:o_g].astype(BF16)
    g_ref[...] = w[:, o_g:].astype(BF16)


def _split_in_proj(w, offs, *, tr=128):
    d, n = w.shape
    o_z, o_b, o_a, o_u, o_g = offs
    widths = (o_z, o_b - o_z, 2 * LANES, o_g - o_u, n - o_g)
    return pl.pallas_call(
        functools.partial(_wsplit_kernel, offs=offs),
        grid=(d // tr,),
        in_specs=[pl.BlockSpec((tr, n), lambda i: (i, 0))],
        out_specs=[pl.BlockSpec((tr, wd), lambda i: (i, 0)) for wd in widths],
        out_shape=[jax.ShapeDtypeStruct((d, wd), BF16) for wd in widths],
        compiler_params=_params(("parallel",)),
        name="split_in_proj",
    )(w)


def kernel(x, c, ctx, c_ctx, w_ada, b_ada, norm1_w, w_in, dn_conv_w, dn_a_log, dn_dt_bias, dn_norm_w,
           w_a_out, s5_a_re, s5_a_im, s5_log_step, s5_b_re, s5_b_im, s5_c_re, s5_c_im, s5_d, w_glu,
           b_glu, w_b_out, w_o, norm2_w, w_up, ffn_conv_w, w_down, norm_f_w):
    assert w_ada.shape[0] == 1, "single-layer block"
    b, t, d = x.shape
    tc = ctx.shape[1]
    nh = N_DIR * DN_HEADS
    assert b < 8, "batch rows and the context row share one 8-row modulation block"

    c_rows = jnp.zeros((8, d), F32).at[:b].set(c).at[b].set(c_ctx)
    mods = _modulation(c_rows, w_ada[0], b_ada[0]).reshape(8, 1, N_MOD * d)

    w = w_in[0]
    o_z, o_b, o_a = 3 * DN_WIDTH, 4 * DN_WIDTH, 4 * DN_WIDTH + nh
    o_u = o_a + nh
    o_g = o_u + S5_WIDTH
    wqkv, wz, wba, wu, wg = _split_in_proj(w, (o_z, o_b, o_a, o_u, o_g))

    qkv_l, ba_l, u_l, z_l, gates_l = _in_proj(x, mods, lambda i: i, norm1_w[0], wqkv, wba, wu, wz, wg, tm=1024)
    qkv_c, ba_c, u_c = _in_proj(ctx, mods, lambda i: b, norm1_w[0], wqkv, wba, wu, tm=tc)

    alog_row, dtb_row = _lane_row(dn_a_log[0]), _lane_row(dn_dt_bias[0])
    wy_c = _delta_wy(qkv_c, ba_c, dn_conv_w[0], alog_row, dtb_row, ts=tc)
    wy_l = _delta_wy(qkv_l, ba_l, dn_conv_w[0], alog_row, dtb_row, ts=256)
    o_f, o_r = _delta_scan(wy_c, wy_l, ts=256)

    w_s5in, w_s5t, w_s5m, lam = _s5_operators(s5_a_re[0], s5_a_im[0], s5_log_step[0], s5_b_re[0],
                                              s5_b_im[0], s5_c_re[0], s5_c_im[0])
    zs_c, zs_l = _s5_chunk_in(u_c, u_l, w_s5in)
    x_l = _s5_state_scan(zs_c, zs_l, lam, nb=b, tl=256)
    y_s5 = _s5_chunk_out(u_l, x_l, w_s5t, w_s5m)

    xl, h2 = _mix_merge(o_f, o_r, z_l, y_s5, u_l, gates_l, x, mods, dn_norm_w[0], w_a_out[0].astype(BF16),
                        s5_d[0], w_glu[0].astype(BF16), b_glu[0], w_b_out[0].astype(BF16),
                        w_o[0].astype(BF16), norm2_w[0], tm=512)
    return _conv_ffn(h2, xl, mods, w_up[0], ffn_conv_w[0], w_down[0], norm_f_w)
```
